```python
import jax, jax.numpy as jnp
from jax import lax
import numpy as np

D_MODEL = 4096
BATCH = 2
SEQ = 4096
DEPTH = 1

N_META = 16
MIX_WIDTH = D_MODEL
RET_WIDTH = MIX_WIDTH // 2
RET_HEADS = 8
RET_HEAD_DIM = RET_WIDTH // RET_HEADS
CONV_WIDTH = MIX_WIDTH - RET_WIDTH
CONV_K = 3
IN_COLS = 4 * RET_WIDTH + 3 * CONV_WIDTH
CHUNK = 128
ROPE_BASE = 10000.0
N_EXPERTS = 32
TOP_K = 4
D_FF = D_MODEL // 2
SWIGLU_LIMIT = 7.0
SWIGLU_ALPHA = 1.702
MOE_BLOCK = 128
NORM_EPS = 1e-5
GN_EPS = 1e-6

kernel_name = "hymba_retnet_shortconv_moe_layer"


def rms_norm(x, g):
    xf = x.astype(jnp.float32)
    y = xf * lax.rsqrt(jnp.mean(xf * xf, axis=-1, keepdims=True) + NORM_EPS)
    return (y * g.astype(jnp.float32)).astype(x.dtype)


def rotary(t, pos):
    half = t.shape[-1] // 2
    inv = ROPE_BASE ** (-jnp.arange(half, dtype=jnp.float32) / half)
    ang = pos.astype(jnp.float32)[:, None] * inv[None, :]
    cos = jnp.cos(ang)[None, :, None, :]
    sin = jnp.sin(ang)[None, :, None, :]
    t1, t2 = t[..., :half], t[..., half:]
    return jnp.concatenate([t1 * cos - t2 * sin, t2 * cos + t1 * sin], axis=-1)


def retention_chunkwise(q, k, v):
    bn, L, H, dk = q.shape
    dv = v.shape[-1]
    pad = (-L) % CHUNK
    nc = (L + pad) // CHUNK
    lg = jnp.log1p(-jnp.exp2(-5.0 - jnp.arange(H, dtype=jnp.float32)))
    i = jnp.arange(CHUNK, dtype=jnp.float32)
    rel = i[:, None] - i[None, :]
    causal = rel >= 0
    dmask = jnp.where(causal[None], jnp.exp(jnp.where(causal, rel, 0.0)[None] * lg[:, None, None]), 0.0)
    q_dec = jnp.exp((i + 1.0)[None, :] * lg[:, None])
    k_dec = jnp.exp((CHUNK - 1.0 - i)[None, :] * lg[:, None])
    c_dec = jnp.exp(CHUNK * lg)

    def to_chunks(t):
        d = t.shape[-1]
        t = jnp.pad(t, ((0, 0), (pad, 0), (0, 0), (0, 0)))
        return t.reshape(bn, nc, CHUNK, H, d).transpose(1, 0, 3, 2, 4)

    def step(state, inp):
        qb, kb, vb = inp
        scores = jnp.einsum('bhid,bhjd->bhij', qb, kb) * dmask
        inner = jnp.einsum('bhij,bhje->bhie', scores, vb)
        cross = jnp.einsum('bhid,bhde->bhie', qb * q_dec[..., None], state)
        state = state * c_dec[:, None, None] + jnp.einsum('bhjd,bhje->bhde', kb * k_dec[..., None], vb)
        return state, inner + cross

    s0 = jnp.zeros((bn, H, dk, dv), jnp.float32)
    _, out = lax.scan(step, s0, (to_chunks(q), to_chunks(k), to_chunks(v)))
    out = out.transpose(1, 0, 3, 2, 4).reshape(bn, nc * CHUNK, H, dv)
    return out[:, pad:]


def head_group_norm(y):
    mu = jnp.mean(y, axis=-1, keepdims=True)
    var = jnp.mean(jnp.square(y - mu), axis=-1, keepdims=True)
    return (y - mu) * lax.rsqrt(var + GN_EPS)


def causal_depthwise_conv(u, w):
    c = u.shape[-1]
    rhs = w.astype(u.dtype)[:, None, :]
    return lax.conv_general_dilated(u, rhs, window_strides=(1,), padding=[(CONV_K - 1, 0)],
                                    dimension_numbers=('NWC', 'WIO', 'NWC'), feature_group_count=c)


def moe_ffn(m, router_w, router_b, w_gate, b_gate, w_up, b_up, w_down, b_down):
    bn, L, D = m.shape
    tok = m.reshape(-1, D)
    n = tok.shape[0]
    logits = (tok @ router_w).astype(jnp.float32) + router_b.astype(jnp.float32)
    top_val, top_idx = lax.top_k(logits, TOP_K)
    gates = jax.nn.softmax(top_val, axis=-1)
    nk = n * TOP_K
    flat_e = top_idx.reshape(-1).astype(jnp.int32)
    flat_t = jnp.repeat(jnp.arange(n, dtype=jnp.int32), TOP_K)
    flat_g = gates.reshape(-1)
    order = jnp.argsort(flat_e, stable=True)
    se, st, sg = flat_e[order], flat_t[order], flat_g[order]
    counts = jnp.bincount(flat_e, length=N_EXPERTS).astype(jnp.int32)
    starts = jnp.cumsum(counts) - counts
    pcounts = (counts + MOE_BLOCK - 1) // MOE_BLOCK * MOE_BLOCK
    pends = jnp.cumsum(pcounts)
    pstarts = pends - pcounts
    dest = pstarts[se] + (jnp.arange(nk, dtype=jnp.int32) - starts[se])
    n_blocks = -(-(nk + N_EXPERTS * (MOE_BLOCK - 1)) // MOE_BLOCK)
    rows = n_blocks * MOE_BLOCK
    row_t = jnp.zeros((rows,), jnp.int32).at[dest].set(st)
    row_g = jnp.zeros((rows,), jnp.float32).at[dest].set(sg)
    block_e = jnp.minimum(jnp.searchsorted(pends, jnp.arange(n_blocks, dtype=jnp.int32) * MOE_BLOCK, side='right'),
                          N_EXPERTS - 1).astype(jnp.int32)

    def body(acc, blk):
        t_b, g_b, e = blk
        xb = tok[t_b]
        gt = xb @ w_gate[e] + b_gate[e]
        up = xb @ w_up[e] + b_up[e]
        gt = jnp.minimum(gt, SWIGLU_LIMIT)
        up = jnp.clip(up, -SWIGLU_LIMIT, SWIGLU_LIMIT)
        hid = (up + 1.0) * gt * jax.nn.sigmoid(SWIGLU_ALPHA * gt)
        yb = hid @ w_down[e] + b_down[e]
        acc = acc.at[t_b].add((yb * g_b[:, None]).astype(acc.dtype))
        return acc, None

    acc0 = jnp.zeros((n, D), m.dtype)
    out, _ = lax.scan(body, acc0, (row_t.reshape(n_blocks, MOE_BLOCK), row_g.reshape(n_blocks, MOE_BLOCK), block_e))
    return out.reshape(bn, L, D)


def setup_inputs(seed: int = 0) -> dict:
    key = jax.random.key(seed)
    ks = jax.random.split(key, 17)
    f32 = jnp.float32

    def nrm(k, shape, scale):
        return jax.random.normal(k, shape, f32) * scale

    return {
        "x": nrm(ks[0], (BATCH, SEQ, D_MODEL), 1.0),
        "meta_tokens": nrm(ks[1], (N_META, D_MODEL), 1.0),
        "norm_mix": 1.0 + nrm(ks[2], (DEPTH, D_MODEL), 0.01),
        "w_in": nrm(ks[3], (DEPTH, D_MODEL, IN_COLS), D_MODEL ** -0.5),
        "conv_w": nrm(ks[4], (DEPTH, CONV_K, CONV_WIDTH), CONV_K ** -0.5),
        "w_out": nrm(ks[5], (DEPTH, MIX_WIDTH, D_MODEL), MIX_WIDTH ** -0.5),
        "norm_ffn": 1.0 + nrm(ks[6], (DEPTH, D_MODEL), 0.01),
        "router_w": nrm(ks[7], (DEPTH, D_MODEL, N_EXPERTS), D_MODEL ** -0.5),
        "router_b": nrm(ks[8], (DEPTH, N_EXPERTS), 0.01),
        "w_gate": nrm(ks[9], (DEPTH, N_EXPERTS, D_MODEL, D_FF), D_MODEL ** -0.5),
        "b_gate": nrm(ks[10], (DEPTH, N_EXPERTS, D_FF), 0.01),
        "w_up": nrm(ks[11], (DEPTH, N_EXPERTS, D_MODEL, D_FF), D_MODEL ** -0.5),
        "b_up": nrm(ks[12], (DEPTH, N_EXPERTS, D_FF), 0.01),
        "w_down": nrm(ks[13], (DEPTH, N_EXPERTS, D_FF, D_MODEL), D_FF ** -0.5),
        "b_down": nrm(ks[14], (DEPTH, N_EXPERTS, D_MODEL), 0.01),
        "norm_final": 1.0 + nrm(ks[15], (D_MODEL,), 0.01),
    }


def reference(x, meta_tokens, norm_mix, w_in, conv_w, w_out, norm_ffn, router_w, router_b,
              w_gate, b_gate, w_up, b_up, w_down, b_down, norm_final):
    bn = x.shape[0]
    meta = jnp.broadcast_to(meta_tokens.astype(x.dtype)[None], (bn, N_META, D_MODEL))
    h = jnp.concatenate([meta, x], axis=1)
    L = h.shape[1]
    pos = jnp.arange(L, dtype=jnp.int32)
    r, c = RET_WIDTH, CONV_WIDTH
    split_at = [r, 2 * r, 3 * r, 4 * r, 4 * r + c, 4 * r + 2 * c]
    for l in range(DEPTH):
        a = rms_norm(h, norm_mix[l])
        p = a @ w_in[l]
        q, k, v, g, cb, cc, ch = jnp.split(p, split_at, axis=-1)
        shp = (bn, L, RET_HEADS, RET_HEAD_DIM)
        q = rotary(q.reshape(shp).astype(jnp.float32), pos)
        k = rotary(k.reshape(shp).astype(jnp.float32), pos) * (RET_HEAD_DIM ** -0.5)
        v = v.reshape(shp).astype(jnp.float32)
        y = head_group_norm(retention_chunkwise(q, k, v)).reshape(bn, L, RET_WIDTH)
        ret_out = (jax.nn.silu(g.astype(jnp.float32)) * y).astype(h.dtype)
        conv_out = cb * causal_depthwise_conv(cc * ch, conv_w[l])
        mix = jnp.concatenate([ret_out, conv_out.astype(h.dtype)], axis=-1)
        h = h + mix @ w_out[l]
        m = rms_norm(h, norm_ffn[l])
        h = h + moe_ffn(m, router_w[l], router_b[l], w_gate[l], b_gate[l], w_up[l], b_up[l], w_down[l], b_down[l])
    out = rms_norm(h, norm_final)
    return out[:, N_META:]
```

```python
import functools

import jax
import jax.numpy as jnp
from jax import lax
from jax.experimental import pallas as pl
from jax.experimental.pallas import tpu as pltpu

F32 = jnp.float32
BF16 = jnp.bfloat16

N_META = 16
RET_HEADS = 8
CHUNK = 128
CONV_K = 3
ROPE_BASE = 10000.0
N_EXPERTS = 32
TOP_K = 4
SWIGLU_LIMIT = 7.0
SWIGLU_ALPHA = 1.702
NORM_EPS = 1e-5
GN_EPS = 1e-6

LANES = 128
SUBLANES = 8
VMEM_LIMIT = 58 * 1024 * 1024

GROUP_ROWS = 1280
SUB_ROWS = 128
ROUTE_LANES = 128


def _params(sem, vmem=VMEM_LIMIT):
    return pltpu.CompilerParams(dimension_semantics=sem, vmem_limit_bytes=vmem)


def _prenorm_kernel(x_ref, g_ref, o_ref):
    x = x_ref[...]
    ms = jnp.mean(x * x, axis=-1, keepdims=True)
    o_ref[...] = (x * lax.rsqrt(ms + NORM_EPS) * g_ref[...]).astype(o_ref.dtype)


def _prenorm(x2d, g, tm):
    n, d = x2d.shape
    return pl.pallas_call(
        _prenorm_kernel,
        grid=(n // tm,),
        in_specs=[pl.BlockSpec((tm, d), lambda i: (i, 0)),
                  pl.BlockSpec((1, d), lambda i: (0, 0))],
        out_specs=pl.BlockSpec((tm, d), lambda i: (i, 0)),
        out_shape=jax.ShapeDtypeStruct((n, d), BF16),
        compiler_params=_params(("arbitrary",)),
        name="prenorm",
    )(x2d, g)


def _inproj_kernel(a_ref, am_ref, w_ref, o_ref, om_ref, wb_ref):
    @pl.when(pl.program_id(1) == 0)
    def _():
        wb_ref[...] = w_ref[...].astype(BF16)
        om_ref[...] = jnp.dot(am_ref[...], wb_ref[...], preferred_element_type=F32)

    o_ref[...] = jnp.dot(a_ref[...], wb_ref[...], preferred_element_type=F32)


def _inproj(a, a_meta, w, tm, tn):
    n, d = a.shape
    c = w.shape[1]
    nm = a_meta.shape[0]
    return pl.pallas_call(
        _inproj_kernel,
        grid=(c // tn, n // tm),
        in_specs=[pl.BlockSpec((tm, d), lambda j, i: (i, 0)),
                  pl.BlockSpec((nm, d), lambda j, i: (0, 0)),
                  pl.BlockSpec((d, tn), lambda j, i: (0, j))],
        out_specs=[pl.BlockSpec((tm, tn), lambda j, i: (i, j)),
                   pl.BlockSpec((nm, tn), lambda j, i: (0, j))],
        out_shape=[jax.ShapeDtypeStruct((n, c), F32),
                   jax.ShapeDtypeStruct((nm, c), F32)],
        scratch_shapes=[pltpu.VMEM((d, tn), BF16)],
        compiler_params=_params(("arbitrary", "arbitrary")),
        name="inproj",
    )(a, a_meta, w)


def _rotary_halves(ref, h, dh, cos, sin):
    half = dh // 2
    t1 = ref[:, h * dh:h * dh + half]
    t2 = ref[:, h * dh + half:(h + 1) * dh]
    return t1 * cos - t2 * sin, t2 * cos + t1 * sin


def _state_update(k1, k2, kdec, v_bf):
    kd = jnp.concatenate([k1 * kdec, k2 * kdec], axis=1).astype(BF16)
    return lax.dot_general(kd, v_bf, (((0,), (0,)), ((), ())), preferred_element_type=F32)


def _meta_kernel(k_ref, v_ref, cc_ref, ch_ref, cos_ref, sin_ref, kdec_ref, s0_ref, u0_ref, *, dh):
    cos = cos_ref[...]
    sin = sin_ref[...]
    scale = dh ** -0.5
    for h in range(RET_HEADS):
        k1, k2 = _rotary_halves(k_ref, h, dh, cos, sin)
        v_bf = v_ref[:, h * dh:(h + 1) * dh].astype(BF16)
        s0_ref[h] = _state_update(k1 * scale, k2 * scale, kdec_ref[h], v_bf)
    u0_ref[...] = cc_ref[CHUNK - SUBLANES:CHUNK, :] * ch_ref[CHUNK - SUBLANES:CHUNK, :]


def _meta_state(pm_pad, cosm, sinm, kdec, r, dh):
    col = lambda cb: pl.BlockSpec((CHUNK, r), lambda i: (0, cb))
    full2 = pl.BlockSpec((CHUNK, LANES), lambda i: (0, 0))
    return pl.pallas_call(
        functools.partial(_meta_kernel, dh=dh),
        grid=(1,),
        in_specs=[col(1), col(2), col(5), col(6), full2, full2,
                  pl.BlockSpec((RET_HEADS, CHUNK, LANES), lambda i: (0, 0, 0))],
        out_specs=[pl.BlockSpec((RET_HEADS, dh, dh), lambda i: (0, 0, 0)),
                   pl.BlockSpec((SUBLANES, r), lambda i: (0, 0))],
        out_shape=[jax.ShapeDtypeStruct((RET_HEADS, dh, dh), F32),
                   jax.ShapeDtypeStruct((SUBLANES, r), F32)],
        compiler_params=_params(("arbitrary",)),
        name="meta_state",
    )(pm_pad, pm_pad, pm_pad, pm_pad, cosm, sinm, kdec)


def _mixer_kernel(cdec_ref, q_ref, k_ref, v_ref, g_ref, cb_ref, cc_ref, ch_ref,
                  cos_ref, sin_ref, dmask_ref, qdec_ref, kdec_ref, cw_ref, s0_ref, u0_ref,
                  mix_ref, state_ref, uext_ref, *, dh, r):
    @pl.when(pl.program_id(1) == 0)
    def _():
        state_ref[...] = s0_ref[...]
        uext_ref[0:SUBLANES, :] = u0_ref[...]

    cos = cos_ref[...]
    sin = sin_ref[...]
    scale = dh ** -0.5
    for h in range(RET_HEADS):
        q1, q2 = _rotary_halves(q_ref, h, dh, cos, sin)
        k1, k2 = _rotary_halves(k_ref, h, dh, cos, sin)
        k1 = k1 * scale
        k2 = k2 * scale
        qb = jnp.concatenate([q1, q2], axis=1).astype(BF16)
        kb = jnp.concatenate([k1, k2], axis=1).astype(BF16)
        v_bf = v_ref[:, h * dh:(h + 1) * dh].astype(BF16)
        scores = lax.dot_general(qb, kb, (((1,), (1,)), ((), ())), preferred_element_type=F32)
        sm = (scores * dmask_ref[h]).astype(BF16)
        inner = jnp.dot(sm, v_bf, preferred_element_type=F32)
        qdec = qdec_ref[h]
        qd = jnp.concatenate([q1 * qdec, q2 * qdec], axis=1).astype(BF16)
        st = state_ref[h]
        cross = jnp.dot(qd, st.astype(BF16), preferred_element_type=F32)
        state_ref[h] = st * cdec_ref[h] + _state_update(k1, k2, kdec_ref[h], v_bf)
        y = inner + cross
        mu = jnp.mean(y, axis=-1, keepdims=True)
        yc = y - mu
        var = jnp.mean(yc * yc, axis=-1, keepdims=True)
        yn = yc * lax.rsqrt(var + GN_EPS)
        gh = g_ref[:, h * dh:(h + 1) * dh]
        mix_ref[:, h * dh:(h + 1) * dh] = (gh * jax.nn.sigmoid(gh) * yn).astype(mix_ref.dtype)

    cw = 512
    for cs in range(0, r, cw):
        sl = slice(cs, cs + cw)
        u = cc_ref[:, sl] * ch_ref[:, sl]
        uext_ref[SUBLANES:SUBLANES + CHUNK, sl] = u
        u1 = uext_ref[SUBLANES - 1:SUBLANES - 1 + CHUNK, sl]
        u2 = uext_ref[SUBLANES - 2:SUBLANES - 2 + CHUNK, sl]
        conv = cw_ref[0:1, sl] * u2 + cw_ref[1:2, sl] * u1 + cw_ref[2:3, sl] * u
        mix_ref[:, r + cs:r + cs + cw] = (cb_ref[:, sl] * conv).astype(mix_ref.dtype)
        uext_ref[0:SUBLANES, sl] = uext_ref[CHUNK:CHUNK + SUBLANES, sl]


def _mixer(p, cos, sin, dmask, qdec, kdec, cdec, conv_w, s0, u0, bsz, seq, r, dh):
    nc = seq // CHUNK
    d = 2 * r
    col = lambda cb: pl.BlockSpec((CHUNK, r), lambda b, c: (b * nc + c, cb))
    tab = pl.BlockSpec((CHUNK, LANES), lambda b, c: (c, 0))
    hconst = pl.BlockSpec((RET_HEADS, CHUNK, LANES), lambda b, c: (0, 0, 0))
    return pl.pallas_call(
        functools.partial(_mixer_kernel, dh=dh, r=r),
        grid=(bsz, nc),
        in_specs=[pl.BlockSpec(memory_space=pltpu.SMEM),
                  col(0), col(1), col(2), col(3), col(4), col(5), col(6),
                  tab, tab, hconst, hconst, hconst,
                  pl.BlockSpec((CONV_K, r), lambda b, c: (0, 0)),
                  pl.BlockSpec((RET_HEADS, dh, dh), lambda b, c: (0, 0, 0)),
                  pl.BlockSpec((SUBLANES, r), lambda b, c: (0, 0))],
        out_specs=pl.BlockSpec((CHUNK, d), lambda b, c: (b * nc + c, 0)),
        out_shape=jax.ShapeDtypeStruct((bsz * seq, d), BF16),
        scratch_shapes=[pltpu.VMEM((RET_HEADS, dh, dh), F32),
                        pltpu.VMEM((CHUNK + 2 * SUBLANES, r), F32)],
        compiler_params=_params(("arbitrary", "arbitrary")),
        name="mixer",
    )(cdec, p, p, p, p, p, p, p, cos, sin, dmask, qdec, kdec, conv_w, s0, u0)


def _outproj_kernel(a_ref, w_ref, x_ref, o_ref, wb_ref):
    @pl.when(pl.program_id(1) == 0)
    def _():
        wb_ref[...] = w_ref[...].astype(BF16)

    o_ref[...] = x_ref[...] + jnp.dot(a_ref[...], wb_ref[...], preferred_element_type=F32)


def _outproj(mix, w, x2d, tm, tn):
    n, d = mix.shape
    c = w.shape[1]
    return pl.pallas_call(
        _outproj_kernel,
        grid=(c // tn, n // tm),
        in_specs=[pl.BlockSpec((tm, d), lambda j, i: (i, 0)),
                  pl.BlockSpec((d, tn), lambda j, i: (0, j)),
                  pl.BlockSpec((tm, tn), lambda j, i: (i, j))],
        out_specs=pl.BlockSpec((tm, tn), lambda j, i: (i, j)),
        out_shape=jax.ShapeDtypeStruct((n, c), F32),
        scratch_shapes=[pltpu.VMEM((d, tn), BF16)],
        compiler_params=_params(("arbitrary", "arbitrary")),
        name="outproj",
    )(mix, w, x2d)


def _router_kernel(h_ref, g_ref, wh_ref, wl_ref, b_ref, m_ref, route_ref, cnt_ref, carry_ref, *, tr):
    @pl.when(pl.program_id(0) == 0)
    def _():
        carry_ref[...] = jnp.zeros_like(carry_ref)

    x = h_ref[...]
    ms = jnp.mean(x * x, axis=-1, keepdims=True)
    m = x * lax.rsqrt(ms + NORM_EPS) * g_ref[...]
    m_ref[...] = m

    mh = m.astype(BF16)
    ml = (m - mh.astype(F32)).astype(BF16)
    wh = wh_ref[...]
    logits = (jnp.dot(mh, wh, preferred_element_type=F32)
              + jnp.dot(ml, wh, preferred_element_type=F32)
              + jnp.dot(mh, wl_ref[...], preferred_element_type=F32)
              + b_ref[...])

    lane = lax.broadcasted_iota(jnp.int32, (tr, ROUTE_LANES), 1)
    lane_f = lane.astype(F32)
    work = logits
    vals, onehots = [], []
    for _ in range(TOP_K):
        mx = jnp.max(work, axis=-1, keepdims=True)
        idx = jnp.min(jnp.where(work == mx, lane_f, float(ROUTE_LANES)), axis=-1, keepdims=True)
        oh = lane_f == idx
        vals.append(mx)
        onehots.append(oh)
        work = jnp.where(oh, -jnp.inf, work)

    exps = [jnp.exp(v - vals[0]) for v in vals]
    denom = exps[0] + exps[1] + exps[2] + exps[3]
    gates = [e / denom for e in exps]

    chosen = onehots[0] | onehots[1] | onehots[2] | onehots[3]
    cmat = jnp.where(chosen, 1.0, 0.0).astype(BF16)
    row = lax.broadcasted_iota(jnp.int32, (tr, tr), 0)
    colm = lax.broadcasted_iota(jnp.int32, (tr, tr), 1)
    lower = jnp.where(colm < row, 1.0, 0.0).astype(BF16)
    carry = carry_ref[0:1, :]
    rank_e = jnp.dot(lower, cmat, preferred_element_type=F32) + carry
    new_carry = carry + jnp.sum(cmat.astype(F32), axis=0, keepdims=True)
    carry_ref[...] = jnp.broadcast_to(new_carry, carry_ref.shape)
    cnt_ref[...] = jnp.broadcast_to(new_carry, cnt_ref.shape)

    out = jnp.zeros((tr, ROUTE_LANES), F32)
    for k in range(TOP_K):
        idx_k = jnp.sum(jnp.where(onehots[k], lane_f, 0.0), axis=-1, keepdims=True)
        rank_k = jnp.sum(jnp.where(onehots[k], rank_e, 0.0), axis=-1, keepdims=True)
        out = jnp.where(lane == k, gates[k], out)
        out = jnp.where(lane == TOP_K + k, idx_k, out)
        out = jnp.where(lane == 2 * TOP_K + k, rank_k, out)
    route_ref[...] = out


def _router(h2, g, wh, wl, b, tr):
    n, d = h2.shape
    return pl.pallas_call(
        functools.partial(_router_kernel, tr=tr),
        grid=(n // tr,),
        in_specs=[pl.BlockSpec((tr, d), lambda i: (i, 0)),
                  pl.BlockSpec((1, d), lambda i: (0, 0)),
                  pl.BlockSpec((d, ROUTE_LANES), lambda i: (0, 0)),
                  pl.BlockSpec((d, ROUTE_LANES), lambda i: (0, 0)),
                  pl.BlockSpec((1, ROUTE_LANES), lambda i: (0, 0))],
        out_specs=[pl.BlockSpec((tr, d), lambda i: (i, 0)),
                   pl.BlockSpec((tr, ROUTE_LANES), lambda i: (i, 0)),
                   pl.BlockSpec((SUBLANES, ROUTE_LANES), lambda i: (0, 0))],
        out_shape=[jax.ShapeDtypeStruct((n, d), F32),
                   jax.ShapeDtypeStruct((n, ROUTE_LANES), F32),
                   jax.ShapeDtypeStruct((SUBLANES, ROUTE_LANES), F32)],
        scratch_shapes=[pltpu.VMEM((SUBLANES, ROUTE_LANES), F32)],
        compiler_params=_params(("arbitrary",)),
        name="router",
    )(h2, g, wh, wl, b)


def _row_copy(src_ref, src_row, dst_ref, dst_row, sem):
    return pltpu.make_async_copy(src_ref.at[pl.ds(src_row, 1), :], dst_ref.at[pl.ds(dst_row, 1), :], sem)


def _dispatch_kernel(dest_ref, gcnt_ref, m_ref, xg_ref, zero_ref, sem, zsem, *, tt, n_groups_max):
    @pl.when(pl.program_id(0) == 0)
    def _():
        zero_ref[...] = jnp.zeros_like(zero_ref)

        def group_body(g, carry):
            cnt = gcnt_ref[g]
            end = ((cnt + SUB_ROWS - 1) // SUB_ROWS) * SUB_ROWS

            def start_body(rr, c):
                _row_copy(zero_ref, 0, xg_ref, g * GROUP_ROWS + rr, zsem).start()
                return c

            def wait_body(rr, c):
                _row_copy(zero_ref, 0, xg_ref, g * GROUP_ROWS + rr, zsem).wait()
                return c

            lax.fori_loop(cnt, end, start_body, 0)
            lax.fori_loop(cnt, end, wait_body, 0)
            return carry

        lax.fori_loop(0, n_groups_max, group_body, 0)

    def start_body(t, c):
        for k in range(TOP_K):
            _row_copy(m_ref, t, xg_ref, dest_ref[t * TOP_K + k], sem).start()
        return c

    def wait_body(t, c):
        for k in range(TOP_K):
            _row_copy(m_ref, t, xg_ref, dest_ref[t * TOP_K + k], sem).wait()
        return c

    lax.fori_loop(0, tt, start_body, 0)
    lax.fori_loop(0, tt, wait_body, 0)


def _dispatch(dest_flat, gcnt, m, n_groups_max, tt):
    n, d = m.shape
    rows = n_groups_max * GROUP_ROWS
    return pl.pallas_call(
        functools.partial(_dispatch_kernel, tt=tt, n_groups_max=n_groups_max),
        grid=(n // tt,),
        in_specs=[pl.BlockSpec((tt * TOP_K,), lambda i: (i,), memory_space=pltpu.SMEM),
                  pl.BlockSpec(memory_space=pltpu.SMEM),
                  pl.BlockSpec((tt, d), lambda i: (i, 0))],
        out_specs=pl.BlockSpec(memory_space=pl.ANY),
        out_shape=jax.ShapeDtypeStruct((rows, d), F32),
        scratch_shapes=[pltpu.VMEM((SUBLANES, d), F32),
                        pltpu.SemaphoreType.DMA(()),
                        pltpu.SemaphoreType.DMA(())],
        compiler_params=_params(("arbitrary",)),
        name="dispatch",
    )(dest_flat, gcnt, m)


def _gate_up_kernel(gexp_ref, gblk_ref, gact_ref, gcnt_ref,
                    x_ref, wg_ref, wu_ref, bg_ref, bu_ref, o_ref, xb_ref, wb_ref, *, tf):
    g = pl.program_id(0)
    j = pl.program_id(1)

    @pl.when(gact_ref[g] == 1)
    def _():
        nsub = (gcnt_ref[g] + SUB_ROWS - 1) // SUB_ROWS

        @pl.when(j == 0)
        def _():
            def cast_body(s, c):
                rows = pl.ds(pl.multiple_of(s * SUB_ROWS, SUB_ROWS), SUB_ROWS)
                xb_ref[rows, :] = x_ref[rows, :].astype(BF16)
                return c

            lax.fori_loop(0, nsub, cast_body, 0)

        wb_ref[:, 0:tf] = wg_ref[0].astype(BF16)
        wb_ref[:, tf:2 * tf] = wu_ref[0].astype(BF16)
        bg = bg_ref[0]
        bu = bu_ref[0]

        def sub_body(s, c):
            rows = pl.ds(pl.multiple_of(s * SUB_ROWS, SUB_ROWS), SUB_ROWS)
            z = jnp.dot(xb_ref[rows, :], wb_ref[...], preferred_element_type=F32)
            gt = jnp.minimum(z[:, 0:tf] + bg, SWIGLU_LIMIT)
            up = jnp.clip(z[:, tf:2 * tf] + bu, -SWIGLU_LIMIT, SWIGLU_LIMIT)
            hid = (up + 1.0) * gt * jax.nn.sigmoid(SWIGLU_ALPHA * gt)
            o_ref[rows, :] = hid.astype(o_ref.dtype)
            return c

        def zero_body(s, c):
            rows = pl.ds(pl.multiple_of(s * SUB_ROWS, SUB_ROWS), SUB_ROWS)
            o_ref[rows, :] = jnp.zeros((SUB_ROWS, tf), o_ref.dtype)
            return c

        lax.fori_loop(0, nsub, sub_body, 0)
        lax.fori_loop(nsub, GROUP_ROWS // SUB_ROWS, zero_body, 0)


def _gate_up(gexp, gblk, gact, gcnt, xg, w_gate, w_up, b_gate, b_up, n_groups_max, tf):
    d = xg.shape[1]
    f = w_gate.shape[2]
    nj = f // tf

    def jj(j, act):
        return j * act + (nj - 1) * (1 - act)

    grid_spec = pltpu.PrefetchScalarGridSpec(
        num_scalar_prefetch=4,
        grid=(n_groups_max, nj),
        in_specs=[
            pl.BlockSpec((GROUP_ROWS, d), lambda g, j, ge, gb, ga, gc: (gb[g], 0),
                         pipeline_mode=pl.Buffered(1)),
            pl.BlockSpec((1, d, tf), lambda g, j, ge, gb, ga, gc: (ge[g], 0, jj(j, ga[g]))),
            pl.BlockSpec((1, d, tf), lambda g, j, ge, gb, ga, gc: (ge[g], 0, jj(j, ga[g]))),
            pl.BlockSpec((1, 1, tf), lambda g, j, ge, gb, ga, gc: (ge[g], 0, jj(j, ga[g]))),
            pl.BlockSpec((1, 1, tf), lambda g, j, ge, gb, ga, gc: (ge[g], 0, jj(j, ga[g]))),
        ],
        out_specs=pl.BlockSpec((GROUP_ROWS, tf), lambda g, j, ge, gb, ga, gc: (gb[g], jj(j, ga[g]))),
        scratch_shapes=[pltpu.VMEM((GROUP_ROWS, d), BF16),
                        pltpu.VMEM((d, 2 * tf), BF16)],
    )
    return pl.pallas_call(
        functools.partial(_gate_up_kernel, tf=tf),
        grid_spec=grid_spec,
        out_shape=jax.ShapeDtypeStruct((n_groups_max * GROUP_ROWS, f), BF16),
        compiler_params=_params(("arbitrary", "arbitrary")),
        name="expert_gate_up",
    )(gexp, gblk, gact, gcnt, xg, w_gate, w_up, b_gate, b_up)


def _down_kernel(gexp_ref, gblk_ref, gact_ref, gcnt_ref, h_ref, w_ref, b_ref, o_ref, wb_ref, *, tn):
    g = pl.program_id(0)

    @pl.when(gact_ref[g] == 1)
    def _():
        nsub = (gcnt_ref[g] + SUB_ROWS - 1) // SUB_ROWS
        wb_ref[...] = w_ref[0].astype(BF16)
        bd = b_ref[0]

        def sub_body(s, c):
            rows = pl.ds(pl.multiple_of(s * SUB_ROWS, SUB_ROWS), SUB_ROWS)
            o_ref[rows, :] = jnp.dot(h_ref[rows, :], wb_ref[...], preferred_element_type=F32) + bd
            return c

        def zero_body(s, c):
            rows = pl.ds(pl.multiple_of(s * SUB_ROWS, SUB_ROWS), SUB_ROWS)
            o_ref[rows, :] = jnp.zeros((SUB_ROWS, tn), o_ref.dtype)
            return c

        lax.fori_loop(0, nsub, sub_body, 0)
        lax.fori_loop(nsub, GROUP_ROWS // SUB_ROWS, zero_body, 0)


def _down(gexp, gblk, gact, gcnt, hid, w_down, b_down, n_groups_max, tn):
    f = hid.shape[1]
    d = w_down.shape[2]
    nn = d // tn

    def jj(j, act):
        return j * act + (nn - 1) * (1 - act)

    grid_spec = pltpu.PrefetchScalarGridSpec(
        num_scalar_prefetch=4,
        grid=(n_groups_max, nn),
        in_specs=[
            pl.BlockSpec((GROUP_ROWS, f), lambda g, j, ge, gb, ga, gc: (gb[g], 0)),
            pl.BlockSpec((1, f, tn), lambda g, j, ge, gb, ga, gc: (ge[g], 0, jj(j, ga[g]))),
            pl.BlockSpec((1, 1, tn), lambda g, j, ge, gb, ga, gc: (ge[g], 0, jj(j, ga[g]))),
        ],
        out_specs=pl.BlockSpec((GROUP_ROWS, tn), lambda g, j, ge, gb, ga, gc: (gb[g], jj(j, ga[g]))),
        scratch_shapes=[pltpu.VMEM((f, tn), BF16)],
    )
    return pl.pallas_call(
        functools.partial(_down_kernel, tn=tn),
        grid_spec=grid_spec,
        out_shape=jax.ShapeDtypeStruct((n_groups_max * GROUP_ROWS, d), F32),
        compiler_params=_params(("arbitrary", "arbitrary")),
        name="expert_down",
    )(gexp, gblk, gact, gcnt, hid, w_down, b_down)


def _combine_kernel(dcur_ref, dnext_ref, route_ref, h_ref, g_ref, yg_ref, o_ref, buf_ref, sem, *, tt):
    i = pl.program_id(0)
    n_steps = pl.num_programs(0)
    slot = lax.rem(i, 2)

    def start_tile(dref, sl):
        def body(t, c):
            for k in range(TOP_K):
                pltpu.make_async_copy(yg_ref.at[pl.ds(dref[t * TOP_K + k], 1), :],
                                      buf_ref.at[sl, k, pl.ds(t, 1), :], sem.at[sl]).start()
            return c

        lax.fori_loop(0, tt, body, 0)

    @pl.when(i == 0)
    def _():
        start_tile(dcur_ref, 0)

    @pl.when(i + 1 < n_steps)
    def _():
        start_tile(dnext_ref, 1 - slot)

    def wait_body(t, c):
        for k in range(TOP_K):
            pltpu.make_async_copy(yg_ref.at[pl.ds(0, 1), :],
                                  buf_ref.at[slot, k, pl.ds(t, 1), :], sem.at[slot]).wait()
        return c

    lax.fori_loop(0, tt, wait_body, 0)

    route = route_ref[...]
    acc = h_ref[...]
    for k in range(TOP_K):
        acc = acc + buf_ref[slot, k] * route[:, k:k + 1]
    ms = jnp.mean(acc * acc, axis=-1, keepdims=True)
    o_ref[...] = acc * lax.rsqrt(ms + NORM_EPS) * g_ref[...]


def _combine(dest_flat, route, h2, g, yg, tt):
    n, d = h2.shape
    steps = n // tt
    return pl.pallas_call(
        functools.partial(_combine_kernel, tt=tt),
        grid=(steps,),
        in_specs=[pl.BlockSpec((tt * TOP_K,), lambda i: (i,), memory_space=pltpu.SMEM),
                  pl.BlockSpec((tt * TOP_K,), lambda i: (jnp.minimum(i + 1, steps - 1),),
                               memory_space=pltpu.SMEM),
                  pl.BlockSpec((tt, ROUTE_LANES), lambda i: (i, 0)),
                  pl.BlockSpec((tt, d), lambda i: (i, 0)),
                  pl.BlockSpec((1, d), lambda i: (0, 0)),
                  pl.BlockSpec(memory_space=pl.ANY)],
        out_specs=pl.BlockSpec((tt, d), lambda i: (i, 0)),
        out_shape=jax.ShapeDtypeStruct((n, d), F32),
        scratch_shapes=[pltpu.VMEM((2, TOP_K, tt, d), F32),
                        pltpu.SemaphoreType.DMA((2,))],
        compiler_params=_params(("arbitrary",)),
        name="combine",
    )(dest_flat, dest_flat, route, h2, g, yg)


def _retention_tables(dh):
    lg = jnp.log1p(-jnp.exp2(-5.0 - jnp.arange(RET_HEADS, dtype=F32)))
    i = jnp.arange(CHUNK, dtype=F32)
    rel = i[:, None] - i[None, :]
    causal = rel >= 0
    dmask = jnp.where(causal[None], jnp.exp(jnp.where(causal, rel, 0.0)[None] * lg[:, None, None]), 0.0)
    q_dec = jnp.exp((i + 1.0)[None, :] * lg[:, None])
    k_dec = jnp.exp((CHUNK - 1.0 - i)[None, :] * lg[:, None])
    c_dec = jnp.exp(CHUNK * lg)
    bcast = lambda t: jnp.broadcast_to(t[:, :, None], (RET_HEADS, CHUNK, LANES))
    return dmask, bcast(q_dec), bcast(k_dec), c_dec


def _rotary_tables(length, dh):
    half = dh // 2
    inv = ROPE_BASE ** (-jnp.arange(half, dtype=F32) / half)
    ang = jnp.arange(length, dtype=jnp.int32).astype(F32)[:, None] * inv[None, :]
    return jnp.cos(ang), jnp.sin(ang)


def kernel(x, meta_tokens, norm_mix, w_in, conv_w, w_out, norm_ffn, router_w, router_b,
           w_gate, b_gate, w_up, b_up, w_down, b_down, norm_final):
    bsz, seq, d = x.shape
    assert w_in.shape[0] == 1, "single-layer stack expected"
    assert meta_tokens.shape[0] == N_META and seq % CHUNK == 0
    r = d // 2
    dh = r // RET_HEADS
    assert dh // 2 == LANES
    n = bsz * seq
    x2d = x.reshape(n, d)

    a = _prenorm(x2d, norm_mix[0][None, :], 256)
    a_meta = _prenorm(meta_tokens.astype(x.dtype), norm_mix[0][None, :], N_META)
    p, pm = _inproj(a, a_meta, w_in[0], min(1024, n), 512)

    cos, sin = _rotary_tables(N_META + seq, dh)
    pad_meta = lambda t: jnp.pad(t, ((CHUNK - N_META, 0), (0, 0)))
    dmask, qdec, kdec, cdec = _retention_tables(dh)
    s0, u0 = _meta_state(pad_meta(pm), pad_meta(cos[:N_META]), pad_meta(sin[:N_META]), kdec, r, dh)
    mix = _mixer(p, cos[N_META:], sin[N_META:], dmask, qdec, kdec, cdec, conv_w[0], s0, u0,
                 bsz, seq, r, dh)
    h2 = _outproj(mix, w_out[0], x2d, min(1024, n), 512)

    rw = jnp.pad(router_w[0], ((0, 0), (0, ROUTE_LANES - N_EXPERTS)))
    rwh = rw.astype(BF16)
    rwl = (rw - rwh.astype(F32)).astype(BF16)
    rb = jnp.pad(router_b[0].astype(F32), (0, ROUTE_LANES - N_EXPERTS), constant_values=-1e30)[None, :]
    m, route, cnt = _router(h2, norm_ffn[0][None, :], rwh, rwl, rb, 256)

    ids = route[:, TOP_K:2 * TOP_K].astype(jnp.int32)
    rank = route[:, 2 * TOP_K:3 * TOP_K].astype(jnp.int32)
    counts = cnt[0, :N_EXPERTS].astype(jnp.int32)
    n_groups_max = N_EXPERTS + (n * TOP_K) // GROUP_ROWS
    ngrp = (counts + GROUP_ROWS - 1) // GROUP_ROWS
    gend = jnp.cumsum(ngrp)
    gstart = gend - ngrp
    n_groups = gend[-1]
    dest = (gstart[ids] + rank // GROUP_ROWS) * GROUP_ROWS + rank % GROUP_ROWS
    dest_flat = dest.reshape(-1).astype(jnp.int32)
    gidx = jnp.arange(n_groups_max, dtype=jnp.int32)
    gact = (gidx < n_groups).astype(jnp.int32)
    gblk = jnp.minimum(gidx, n_groups - 1).astype(jnp.int32)
    gexp = jnp.minimum(jnp.searchsorted(gend, gblk, side="right"), N_EXPERTS - 1).astype(jnp.int32)
    gcnt = (jnp.minimum(counts[gexp] - (gblk - gstart[gexp]) * GROUP_ROWS, GROUP_ROWS) * gact).astype(jnp.int32)

    xg = _dispatch(dest_flat, gcnt, m, n_groups_max, 128)
    hid = _gate_up(gexp, gblk, gact, gcnt, xg, w_gate[0], w_up[0],
                   b_gate[0][:, None, :], b_up[0][:, None, :], n_groups_max, 256)
    yg = _down(gexp, gblk, gact, gcnt, hid, w_down[0], b_down[0][:, None, :], n_groups_max, 1024)
    out = _combine(dest_flat, route, h2, norm_final[None, :], yg, 128)
    return out.reshape(bsz, seq, d)
```

```python
import functools

import jax
import jax.numpy as jnp
from jax import lax
from jax.experimental import pallas as pl
from jax.experimental.pallas import tpu as pltpu

F32 = jnp.float32
BF16 = jnp.bfloat16

N_META = 16
RET_HEADS = 8
CHUNK = 128
CONV_K = 3
ROPE_BASE = 10000.0
N_EXPERTS = 32
TOP_K = 4
SWIGLU_LIMIT = 7.0
SWIGLU_ALPHA = 1.702
NORM_EPS = 1e-5
GN_EPS = 1e-6

LANES = 128
SUBLANES = 8
VMEM_LIMIT = 58 * 1024 * 1024

SUB_ROWS = 128
GROUP_SUBS = 9
GROUP_ROWS = GROUP_SUBS * SUB_ROWS
GATE_UP_COLS = 256
DOWN_COLS = 512
ROUTE_LANES = 128


def _params(sem, vmem=VMEM_LIMIT):
    return pltpu.CompilerParams(dimension_semantics=sem, vmem_limit_bytes=vmem)


def _prenorm_kernel(x_ref, g_ref, o_ref):
    x = x_ref[...]
    ms = jnp.mean(x * x, axis=-1, keepdims=True)
    o_ref[...] = (x * lax.rsqrt(ms + NORM_EPS) * g_ref[...]).astype(o_ref.dtype)


def _prenorm(x2d, g, tm):
    n, d = x2d.shape
    return pl.pallas_call(
        _prenorm_kernel,
        grid=(n // tm,),
        in_specs=[pl.BlockSpec((tm, d), lambda i: (i, 0)),
                  pl.BlockSpec((1, d), lambda i: (0, 0))],
        out_specs=pl.BlockSpec((tm, d), lambda i: (i, 0)),
        out_shape=jax.ShapeDtypeStruct((n, d), BF16),
        compiler_params=_params(("arbitrary",)),
        name="prenorm",
    )(x2d, g)


def _inproj_kernel(a_ref, am_ref, w_ref, o_ref, om_ref, wb_ref):
    @pl.when(pl.program_id(1) == 0)
    def _():
        wb_ref[...] = w_ref[...].astype(BF16)
        om_ref[...] = jnp.dot(am_ref[...], wb_ref[...], preferred_element_type=F32)

    o_ref[...] = jnp.dot(a_ref[...], wb_ref[...], preferred_element_type=F32)


def _inproj(a, a_meta, w, tm, tn):
    n, d = a.shape
    c = w.shape[1]
    nm = a_meta.shape[0]
    return pl.pallas_call(
        _inproj_kernel,
        grid=(c // tn, n // tm),
        in_specs=[pl.BlockSpec((tm, d), lambda j, i: (i, 0)),
                  pl.BlockSpec((nm, d), lambda j, i: (0, 0)),
                  pl.BlockSpec((d, tn), lambda j, i: (0, j))],
        out_specs=[pl.BlockSpec((tm, tn), lambda j, i: (i, j)),
                   pl.BlockSpec((nm, tn), lambda j, i: (0, j))],
        out_shape=[jax.ShapeDtypeStruct((n, c), F32),
                   jax.ShapeDtypeStruct((nm, c), F32)],
        scratch_shapes=[pltpu.VMEM((d, tn), BF16)],
        compiler_params=_params(("arbitrary", "arbitrary")),
        name="inproj",
    )(a, a_meta, w)


def _rotary_halves(ref, h, dh, cos, sin):
    half = dh // 2
    t1 = ref[:, h * dh:h * dh + half]
    t2 = ref[:, h * dh + half:(h + 1) * dh]
    return t1 * cos - t2 * sin, t2 * cos + t1 * sin


def _state_update(k1, k2, kdec, v_bf):
    kd = jnp.concatenate([k1 * kdec, k2 * kdec], axis=1).astype(BF16)
    return lax.dot_general(kd, v_bf, (((0,), (0,)), ((), ())), preferred_element_type=F32)


def _meta_kernel(k_ref, v_ref, cc_ref, ch_ref, cos_ref, sin_ref, kdec_ref, s0_ref, u0_ref, *, dh):
    cos = cos_ref[...]
    sin = sin_ref[...]
    scale = dh ** -0.5
    for h in range(RET_HEADS):
        k1, k2 = _rotary_halves(k_ref, h, dh, cos, sin)
        v_bf = v_ref[:, h * dh:(h + 1) * dh].astype(BF16)
        s0_ref[h] = _state_update(k1 * scale, k2 * scale, kdec_ref[h], v_bf)
    u0_ref[...] = cc_ref[CHUNK - SUBLANES:CHUNK, :] * ch_ref[CHUNK - SUBLANES:CHUNK, :]


def _meta_state(pm_pad, cosm, sinm, kdec, r, dh):
    col = lambda cb: pl.BlockSpec((CHUNK, r), lambda i: (0, cb))
    full2 = pl.BlockSpec((CHUNK, LANES), lambda i: (0, 0))
    return pl.pallas_call(
        functools.partial(_meta_kernel, dh=dh),
        grid=(1,),
        in_specs=[col(1), col(2), col(5), col(6), full2, full2,
                  pl.BlockSpec((RET_HEADS, CHUNK, LANES), lambda i: (0, 0, 0))],
        out_specs=[pl.BlockSpec((RET_HEADS, dh, dh), lambda i: (0, 0, 0)),
                   pl.BlockSpec((SUBLANES, r), lambda i: (0, 0))],
        out_shape=[jax.ShapeDtypeStruct((RET_HEADS, dh, dh), F32),
                   jax.ShapeDtypeStruct((SUBLANES, r), F32)],
        compiler_params=_params(("arbitrary",)),
        name="meta_state",
    )(pm_pad, pm_pad, pm_pad, pm_pad, cosm, sinm, kdec)


def _mixer_kernel(cdec_ref, q_ref, k_ref, v_ref, g_ref, cb_ref, cc_ref, ch_ref,
                  cos_ref, sin_ref, dmask_ref, qdec_ref, kdec_ref, cw_ref, s0_ref, u0_ref,
                  mix_ref, state_ref, uext_ref, *, dh, r):
    @pl.when(pl.program_id(1) == 0)
    def _():
        state_ref[...] = s0_ref[...]
        uext_ref[0:SUBLANES, :] = u0_ref[...]

    cos = cos_ref[...]
    sin = sin_ref[...]
    scale = dh ** -0.5
    for h in range(RET_HEADS):
        q1, q2 = _rotary_halves(q_ref, h, dh, cos, sin)
        k1, k2 = _rotary_halves(k_ref, h, dh, cos, sin)
        k1 = k1 * scale
        k2 = k2 * scale
        qb = jnp.concatenate([q1, q2], axis=1).astype(BF16)
        kb = jnp.concatenate([k1, k2], axis=1).astype(BF16)
        v_bf = v_ref[:, h * dh:(h + 1) * dh].astype(BF16)
        scores = lax.dot_general(qb, kb, (((1,), (1,)), ((), ())), preferred_element_type=F32)
        sm = (scores * dmask_ref[h]).astype(BF16)
        inner = jnp.dot(sm, v_bf, preferred_element_type=F32)
        qdec = qdec_ref[h]
        qd = jnp.concatenate([q1 * qdec, q2 * qdec], axis=1).astype(BF16)
        st = state_ref[h]
        cross = jnp.dot(qd, st.astype(BF16), preferred_element_type=F32)
        state_ref[h] = st * cdec_ref[h] + _state_update(k1, k2, kdec_ref[h], v_bf)
        y = inner + cross
        mu = jnp.mean(y, axis=-1, keepdims=True)
        yc = y - mu
        var = jnp.mean(yc * yc, axis=-1, keepdims=True)
        yn = yc * lax.rsqrt(var + GN_EPS)
        gh = g_ref[:, h * dh:(h + 1) * dh]
        mix_ref[:, h * dh:(h + 1) * dh] = (gh * jax.nn.sigmoid(gh) * yn).astype(mix_ref.dtype)

    cw = 512
    for cs in range(0, r, cw):
        sl = slice(cs, cs + cw)
        u = cc_ref[:, sl] * ch_ref[:, sl]
        uext_ref[SUBLANES:SUBLANES + CHUNK, sl] = u
        u1 = uext_ref[SUBLANES - 1:SUBLANES - 1 + CHUNK, sl]
        u2 = uext_ref[SUBLANES - 2:SUBLANES - 2 + CHUNK, sl]
        conv = cw_ref[0:1, sl] * u2 + cw_ref[1:2, sl] * u1 + cw_ref[2:3, sl] * u
        mix_ref[:, r + cs:r + cs + cw] = (cb_ref[:, sl] * conv).astype(mix_ref.dtype)
        uext_ref[0:SUBLANES, sl] = uext_ref[CHUNK:CHUNK + SUBLANES, sl]


def _mixer(p, cos, sin, dmask, qdec, kdec, cdec, conv_w, s0, u0, bsz, seq, r, dh):
    nc = seq // CHUNK
    d = 2 * r
    col = lambda cb: pl.BlockSpec((CHUNK, r), lambda b, c: (b * nc + c, cb))
    tab = pl.BlockSpec((CHUNK, LANES), lambda b, c: (c, 0))
    hconst = pl.BlockSpec((RET_HEADS, CHUNK, LANES), lambda b, c: (0, 0, 0))
    return pl.pallas_call(
        functools.partial(_mixer_kernel, dh=dh, r=r),
        grid=(bsz, nc),
        in_specs=[pl.BlockSpec(memory_space=pltpu.SMEM),
                  col(0), col(1), col(2), col(3), col(4), col(5), col(6),
                  tab, tab, hconst, hconst, hconst,
                  pl.BlockSpec((CONV_K, r), lambda b, c: (0, 0)),
                  pl.BlockSpec((RET_HEADS, dh, dh), lambda b, c: (0, 0, 0)),
                  pl.BlockSpec((SUBLANES, r), lambda b, c: (0, 0))],
        out_specs=pl.BlockSpec((CHUNK, d), lambda b, c: (b * nc + c, 0)),
        out_shape=jax.ShapeDtypeStruct((bsz * seq, d), BF16),
        scratch_shapes=[pltpu.VMEM((RET_HEADS, dh, dh), F32),
                        pltpu.VMEM((CHUNK + 2 * SUBLANES, r), F32)],
        compiler_params=_params(("arbitrary", "arbitrary")),
        name="mixer",
    )(cdec, p, p, p, p, p, p, p, cos, sin, dmask, qdec, kdec, conv_w, s0, u0)


def _outproj_kernel(a_ref, w_ref, x_ref, o_ref, wb_ref):
    @pl.when(pl.program_id(1) == 0)
    def _():
        wb_ref[...] = w_ref[...].astype(BF16)

    o_ref[...] = x_ref[...] + jnp.dot(a_ref[...], wb_ref[...], preferred_element_type=F32)


def _outproj(mix, w, x2d, tm, tn):
    n, d = mix.shape
    c = w.shape[1]
    return pl.pallas_call(
        _outproj_kernel,
        grid=(c // tn, n // tm),
        in_specs=[pl.BlockSpec((tm, d), lambda j, i: (i, 0)),
                  pl.BlockSpec((d, tn), lambda j, i: (0, j)),
                  pl.BlockSpec((tm, tn), lambda j, i: (i, j))],
        out_specs=pl.BlockSpec((tm, tn), lambda j, i: (i, j)),
        out_shape=jax.ShapeDtypeStruct((n, c), F32),
        scratch_shapes=[pltpu.VMEM((d, tn), BF16)],
        compiler_params=_params(("arbitrary", "arbitrary")),
        name="outproj",
    )(mix, w, x2d)


def _router_kernel(h_ref, g_ref, wh_ref, wl_ref, b_ref, m_ref, route_ref, cnt_ref, carry_ref, *, tr):
    @pl.when(pl.program_id(0) == 0)
    def _():
        carry_ref[...] = jnp.zeros_like(carry_ref)

    x = h_ref[...]
    ms = jnp.mean(x * x, axis=-1, keepdims=True)
    m = x * lax.rsqrt(ms + NORM_EPS) * g_ref[...]
    m_ref[...] = m

    mh = m.astype(BF16)
    ml = (m - mh.astype(F32)).astype(BF16)
    wh = wh_ref[...]
    logits = (jnp.dot(mh, wh, preferred_element_type=F32)
              + jnp.dot(ml, wh, preferred_element_type=F32)
              + jnp.dot(mh, wl_ref[...], preferred_element_type=F32)
              + b_ref[...])

    lane = lax.broadcasted_iota(jnp.int32, (tr, ROUTE_LANES), 1)
    lane_f = lane.astype(F32)
    work = logits
    vals, onehots = [], []
    for _ in range(TOP_K):
        mx = jnp.max(work, axis=-1, keepdims=True)
        idx = jnp.min(jnp.where(work == mx, lane_f, float(ROUTE_LANES)), axis=-1, keepdims=True)
        oh = lane_f == idx
        vals.append(mx)
        onehots.append(oh)
        work = jnp.where(oh, -jnp.inf, work)

    exps = [jnp.exp(v - vals[0]) for v in vals]
    denom = exps[0] + exps[1] + exps[2] + exps[3]
    gates = [e / denom for e in exps]

    chosen = onehots[0] | onehots[1] | onehots[2] | onehots[3]
    cmat = jnp.where(chosen, 1.0, 0.0).astype(BF16)
    row = lax.broadcasted_iota(jnp.int32, (tr, tr), 0)
    colm = lax.broadcasted_iota(jnp.int32, (tr, tr), 1)
    lower = jnp.where(colm < row, 1.0, 0.0).astype(BF16)
    carry = carry_ref[0:1, :]
    rank_e = jnp.dot(lower, cmat, preferred_element_type=F32) + carry
    new_carry = carry + jnp.sum(cmat.astype(F32), axis=0, keepdims=True)
    carry_ref[...] = jnp.broadcast_to(new_carry, carry_ref.shape)
    cnt_ref[...] = jnp.broadcast_to(new_carry, cnt_ref.shape)

    out = jnp.zeros((tr, ROUTE_LANES), F32)
    for k in range(TOP_K):
        idx_k = jnp.sum(jnp.where(onehots[k], lane_f, 0.0), axis=-1, keepdims=True)
        rank_k = jnp.sum(jnp.where(onehots[k], rank_e, 0.0), axis=-1, keepdims=True)
        out = jnp.where(lane == k, gates[k], out)
        out = jnp.where(lane == TOP_K + k, idx_k, out)
        out = jnp.where(lane == 2 * TOP_K + k, rank_k, out)
    route_ref[...] = out


def _router(h2, g, wh, wl, b, tr):
    n, d = h2.shape
    return pl.pallas_call(
        functools.partial(_router_kernel, tr=tr),
        grid=(n // tr,),
        in_specs=[pl.BlockSpec((tr, d), lambda i: (i, 0)),
                  pl.BlockSpec((1, d), lambda i: (0, 0)),
                  pl.BlockSpec((d, ROUTE_LANES), lambda i: (0, 0)),
                  pl.BlockSpec((d, ROUTE_LANES), lambda i: (0, 0)),
                  pl.BlockSpec((1, ROUTE_LANES), lambda i: (0, 0))],
        out_specs=[pl.BlockSpec((tr, d), lambda i: (i, 0)),
                   pl.BlockSpec((tr, ROUTE_LANES), lambda i: (i, 0)),
                   pl.BlockSpec((SUBLANES, ROUTE_LANES), lambda i: (0, 0))],
        out_shape=[jax.ShapeDtypeStruct((n, d), F32),
                   jax.ShapeDtypeStruct((n, ROUTE_LANES), F32),
                   jax.ShapeDtypeStruct((SUBLANES, ROUTE_LANES), F32)],
        scratch_shapes=[pltpu.VMEM((SUBLANES, ROUTE_LANES), F32)],
        compiler_params=_params(("arbitrary",)),
        name="router",
    )(h2, g, wh, wl, b)


def _row_copy(src_ref, src_row, dst_ref, dst_row, sem):
    return pltpu.make_async_copy(src_ref.at[pl.ds(src_row, 1), :], dst_ref.at[pl.ds(dst_row, 1), :], sem)


def _dispatch_kernel(dest_ref, zlo_ref, zhi_ref, m_ref, xg_ref, zero_ref, sem, zsem, *, tt):
    @pl.when(pl.program_id(0) == 0)
    def _():
        zero_ref[...] = jnp.zeros_like(zero_ref)

        def range_body(e, carry):
            def start_body(rr, c):
                _row_copy(zero_ref, 0, xg_ref, rr, zsem).start()
                return c

            def wait_body(rr, c):
                _row_copy(zero_ref, 0, xg_ref, rr, zsem).wait()
                return c

            lax.fori_loop(zlo_ref[e], zhi_ref[e], start_body, 0)
            lax.fori_loop(zlo_ref[e], zhi_ref[e], wait_body, 0)
            return carry

        lax.fori_loop(0, N_EXPERTS + 1, range_body, 0)

    def start_body(t, c):
        for k in range(TOP_K):
            _row_copy(m_ref, t, xg_ref, dest_ref[t * TOP_K + k], sem).start(priority=k % 2)
        return c

    def wait_body(t, c):
        for k in range(TOP_K):
            _row_copy(m_ref, t, xg_ref, dest_ref[t * TOP_K + k], sem).wait()
        return c

    lax.fori_loop(0, tt, start_body, 0)
    lax.fori_loop(0, tt, wait_body, 0)


def _dispatch(dest_flat, zlo, zhi, m, rows, tt):
    n, d = m.shape
    return pl.pallas_call(
        functools.partial(_dispatch_kernel, tt=tt),
        grid=(n // tt,),
        in_specs=[pl.BlockSpec((tt * TOP_K,), lambda i: (i,), memory_space=pltpu.SMEM),
                  pl.BlockSpec(memory_space=pltpu.SMEM),
                  pl.BlockSpec(memory_space=pltpu.SMEM),
                  pl.BlockSpec((tt, d), lambda i: (i, 0))],
        out_specs=pl.BlockSpec(memory_space=pl.ANY),
        out_shape=jax.ShapeDtypeStruct((rows, d), F32),
        scratch_shapes=[pltpu.VMEM((SUBLANES, d), F32),
                        pltpu.SemaphoreType.DMA(()),
                        pltpu.SemaphoreType.DMA(())],
        compiler_params=_params(("arbitrary",)),
        name="dispatch",
    )(dest_flat, zlo, zhi, m)


def _for_row_blocks(nsub, block_fn):
    n4 = nsub // 4
    rem = nsub - n4 * 4

    def body(i, c):
        block_fn(pl.multiple_of(i * (4 * SUB_ROWS), 4 * SUB_ROWS), 4 * SUB_ROWS)
        return c

    lax.fori_loop(0, n4, body, 0)
    base = n4 * (4 * SUB_ROWS)

    @pl.when(rem >= 2)
    def _():
        block_fn(pl.multiple_of(base, SUB_ROWS), 2 * SUB_ROWS)

    @pl.when(rem % 2 == 1)
    def _():
        block_fn(pl.multiple_of(base + (rem // 2) * (2 * SUB_ROWS), SUB_ROWS), SUB_ROWS)


def _expert_kernel(gexp_ref, gsub_ref, gns_ref, tail_ref,
                   xg_ref, wg_ref, wu_ref, bg_ref, bu_ref, wd_ref, bd_ref, yg_ref,
                   xb_ref, hid_ref, wb_ref, wdb_ref, stage_ref, ybuf_ref, pend_ref,
                   xsem, ysem, *, nj, nn):
    g = pl.program_id(0)
    s = pl.program_id(1)
    tf = GATE_UP_COLS
    tn = DOWN_COLS
    ns = gns_ref[g]
    row0 = gsub_ref[g] * SUB_ROWS

    def aligned(v, m):
        return v if isinstance(v, int) else pl.multiple_of(v, m)

    def y_copy(sub, slot, col, first_row):
        return pltpu.make_async_copy(
            ybuf_ref.at[slot, pl.ds(aligned(sub * SUB_ROWS, SUB_ROWS), SUB_ROWS), :],
            yg_ref.at[pl.ds(aligned(first_row + sub * SUB_ROWS, SUB_ROWS), SUB_ROWS),
                      pl.ds(aligned(col * tn, tn), tn)],
            ysem.at[slot])

    def wait_pending(slot):
        def body(i, c):
            y_copy(0, slot, 0, 0).wait()
            return c

        lax.fori_loop(0, pend_ref[slot], body, 0)
        pend_ref[slot] = 0

    @pl.when((g == 0) & (s == 0))
    def _():
        pend_ref[0] = 0
        pend_ref[1] = 0
        ybuf_ref[0, 0:SUB_ROWS, :] = jnp.zeros((SUB_ROWS, tn), F32)
        lo = tail_ref[0]
        hi = tail_ref[1]

        def start_body(i, c):
            for col in range(nn):
                y_copy(0, 0, col, i * SUB_ROWS).start()
            return c

        def wait_body(i, c):
            for col in range(nn):
                y_copy(0, 0, col, i * SUB_ROWS).wait()
            return c

        lax.fori_loop(lo, hi, start_body, 0)
        lax.fori_loop(lo, hi, wait_body, 0)

    @pl.when(ns > 0)
    def _():
        @pl.when(s == 0)
        def _():
            def x_copy(sub, slot):
                return pltpu.make_async_copy(
                    xg_ref.at[pl.ds(pl.multiple_of(row0 + sub * SUB_ROWS, SUB_ROWS), SUB_ROWS), :],
                    stage_ref.at[slot], xsem.at[slot])

            x_copy(0, 0).start()

            def body(sub, c):
                slot = lax.rem(sub, 2)

                @pl.when(sub + 1 < ns)
                def _():
                    x_copy(sub + 1, 1 - slot).start()

                x_copy(sub, slot).wait()
                rows = pl.ds(pl.multiple_of(sub * SUB_ROWS, SUB_ROWS), SUB_ROWS)
                xb_ref[rows, :] = stage_ref[slot].astype(BF16)
                return c

            lax.fori_loop(0, ns, body, 0)

        @pl.when(s < nj)
        def _():
            wb_ref[:, 0:tf] = wg_ref[0].astype(BF16)
            wb_ref[:, tf:2 * tf] = wu_ref[0].astype(BF16)
            bg = bg_ref[0]
            bu = bu_ref[0]

            def block(start, rows_n):
                rows = pl.ds(start, rows_n)
                z = jnp.dot(xb_ref[rows, :], wb_ref[...], preferred_element_type=F32)
                gt = jnp.minimum(z[:, 0:tf] + bg, SWIGLU_LIMIT)
                up = jnp.clip(z[:, tf:2 * tf] + bu, -SWIGLU_LIMIT, SWIGLU_LIMIT)
                hid = (up + 1.0) * gt * jax.nn.sigmoid(SWIGLU_ALPHA * gt)
                hid_ref[s, rows, :] = hid.astype(hid_ref.dtype)

            _for_row_blocks(ns, block)

        @pl.when(s >= nj)
        def _():
            col = s - nj
            slot = lax.rem(col, 2)
            wdb_ref[...] = wd_ref[0].astype(BF16)
            bd = bd_ref[0]
            wait_pending(slot)

            def block(start, rows_n):
                rows = pl.ds(start, rows_n)
                acc = jnp.dot(hid_ref[0, rows, :], wdb_ref[0:tf, :], preferred_element_type=F32)
                for c in range(1, nj):
                    acc = acc + jnp.dot(hid_ref[c, rows, :], wdb_ref[c * tf:(c + 1) * tf, :],
                                        preferred_element_type=F32)
                ybuf_ref[slot, rows, :] = acc + bd

            _for_row_blocks(ns, block)

            def start_body(sub, c):
                y_copy(sub, slot, col, row0).start()
                return c

            lax.fori_loop(0, ns, start_body, 0)
            pend_ref[slot] = ns

    @pl.when((g == pl.num_programs(0) - 1) & (s == nj + nn - 1))
    def _():
        wait_pending(0)
        wait_pending(1)


def _experts(gexp, gsub, gns, tail, xg, w_gate, w_up, b_gate, b_up, w_down, b_down, n_groups_max):
    rows, d = xg.shape
    f = w_gate.shape[2]
    tf, tn = GATE_UP_COLS, DOWN_COLS
    nj, nn = f // tf, d // tn

    def up_idx(g, s, ge, gs, gn, tl):
        act = jnp.minimum(gn[g], 1)
        return (ge[g], 0, jnp.minimum(s, nj - 1) * act + (nj - 1) * (1 - act))

    def down_idx(g, s, ge, gs, gn, tl):
        act = jnp.minimum(gn[g], 1)
        return (ge[g], 0, jnp.maximum(s - nj, 0) * act + (nn - 1) * (1 - act))

    grid_spec = pltpu.PrefetchScalarGridSpec(
        num_scalar_prefetch=4,
        grid=(n_groups_max, nj + nn),
        in_specs=[
            pl.BlockSpec(memory_space=pl.ANY),
            pl.BlockSpec((1, d, tf), up_idx),
            pl.BlockSpec((1, d, tf), up_idx),
            pl.BlockSpec((1, 1, tf), up_idx),
            pl.BlockSpec((1, 1, tf), up_idx),
            pl.BlockSpec((1, f, tn), down_idx),
            pl.BlockSpec((1, 1, tn), down_idx),
        ],
        out_specs=pl.BlockSpec(memory_space=pl.ANY),
        scratch_shapes=[pltpu.VMEM((GROUP_ROWS, d), BF16),
                        pltpu.VMEM((nj, GROUP_ROWS, tf), BF16),
                        pltpu.VMEM((d, 2 * tf), BF16),
                        pltpu.VMEM((f, tn), BF16),
                        pltpu.VMEM((2, SUB_ROWS, d), F32),
                        pltpu.VMEM((2, GROUP_ROWS, tn), F32),
                        pltpu.SMEM((2,), jnp.int32),
                        pltpu.SemaphoreType.DMA((2,)),
                        pltpu.SemaphoreType.DMA((2,))],
    )
    return pl.pallas_call(
        functools.partial(_expert_kernel, nj=nj, nn=nn),
        grid_spec=grid_spec,
        out_shape=jax.ShapeDtypeStruct((rows, d), F32),
        compiler_params=_params(("arbitrary", "arbitrary")),
        name="experts",
    )(gexp, gsub, gns, tail, xg, w_gate, w_up, b_gate, b_up, w_down, b_down)


def _combine_kernel(dcur_ref, dnext_ref, route_ref, h_ref, g_ref, yg_ref, o_ref, buf_ref, sem, *, tt):
    i = pl.program_id(0)
    n_steps = pl.num_programs(0)
    slot = lax.rem(i, 2)

    def start_tile(dref, sl):
        def body(t, c):
            for k in range(TOP_K):
                pltpu.make_async_copy(yg_ref.at[pl.ds(dref[t * TOP_K + k], 1), :],
                                      buf_ref.at[sl, k, pl.ds(t, 1), :], sem.at[sl]).start()
            return c

        lax.fori_loop(0, tt, body, 0)

    @pl.when(i == 0)
    def _():
        start_tile(dcur_ref, 0)

    @pl.when(i + 1 < n_steps)
    def _():
        start_tile(dnext_ref, 1 - slot)

    def wait_body(t, c):
        for k in range(TOP_K):
            pltpu.make_async_copy(yg_ref.at[pl.ds(0, 1), :],
                                  buf_ref.at[slot, k, pl.ds(t, 1), :], sem.at[slot]).wait()
        return c

    lax.fori_loop(0, tt, wait_body, 0)

    route = route_ref[...]
    acc = h_ref[...]
    for k in range(TOP_K):
        acc = acc + buf_ref[slot, k] * route[:, k:k + 1]
    ms = jnp.mean(acc * acc, axis=-1, keepdims=True)
    o_ref[...] = acc * lax.rsqrt(ms + NORM_EPS) * g_ref[...]


def _combine(dest_flat, route, h2, g, yg, tt):
    n, d = h2.shape
    steps = n // tt
    return pl.pallas_call(
        functools.partial(_combine_kernel, tt=tt),
        grid=(steps,),
        in_specs=[pl.BlockSpec((tt * TOP_K,), lambda i: (i,), memory_space=pltpu.SMEM),
                  pl.BlockSpec((tt * TOP_K,), lambda i: (jnp.minimum(i + 1, steps - 1),),
                               memory_space=pltpu.SMEM),
                  pl.BlockSpec((tt, ROUTE_LANES), lambda i: (i, 0)),
                  pl.BlockSpec((tt, d), lambda i: (i, 0)),
                  pl.BlockSpec((1, d), lambda i: (0, 0)),
                  pl.BlockSpec(memory_space=pl.ANY)],
        out_specs=pl.BlockSpec((tt, d), lambda i: (i, 0)),
        out_shape=jax.ShapeDtypeStruct((n, d), F32),
        scratch_shapes=[pltpu.VMEM((2, TOP_K, tt, d), F32),
                        pltpu.SemaphoreType.DMA((2,))],
        compiler_params=_params(("arbitrary",)),
        name="combine",
    )(dest_flat, dest_flat, route, h2, g, yg)


def _retention_tables(dh):
    lg = jnp.log1p(-jnp.exp2(-5.0 - jnp.arange(RET_HEADS, dtype=F32)))
    i = jnp.arange(CHUNK, dtype=F32)
    rel = i[:, None] - i[None, :]
    causal = rel >= 0
    dmask = jnp.where(causal[None], jnp.exp(jnp.where(causal, rel, 0.0)[None] * lg[:, None, None]), 0.0)
    q_dec = jnp.exp((i + 1.0)[None, :] * lg[:, None])
    k_dec = jnp.exp((CHUNK - 1.0 - i)[None, :] * lg[:, None])
    c_dec = jnp.exp(CHUNK * lg)
    bcast = lambda t: jnp.broadcast_to(t[:, :, None], (RET_HEADS, CHUNK, LANES))
    return dmask, bcast(q_dec), bcast(k_dec), c_dec


def _rotary_tables(length, dh):
    half = dh // 2
    inv = ROPE_BASE ** (-jnp.arange(half, dtype=F32) / half)
    ang = jnp.arange(length, dtype=jnp.int32).astype(F32)[:, None] * inv[None, :]
    return jnp.cos(ang), jnp.sin(ang)


def kernel(x, meta_tokens, norm_mix, w_in, conv_w, w_out, norm_ffn, router_w, router_b,
           w_gate, b_gate, w_up, b_up, w_down, b_down, norm_final):
    bsz, seq, d = x.shape
    assert w_in.shape[0] == 1, "single-layer stack expected"
    assert meta_tokens.shape[0] == N_META and seq % CHUNK == 0
    r = d // 2
    dh = r // RET_HEADS
    assert dh // 2 == LANES
    n = bsz * seq
    x2d = x.reshape(n, d)

    a = _prenorm(x2d, norm_mix[0][None, :], 256)
    a_meta = _prenorm(meta_tokens.astype(x.dtype), norm_mix[0][None, :], N_META)
    p, pm = _inproj(a, a_meta, w_in[0], min(1024, n), 512)

    cos, sin = _rotary_tables(N_META + seq, dh)
    pad_meta = lambda t: jnp.pad(t, ((CHUNK - N_META, 0), (0, 0)))
    dmask, qdec, kdec, cdec = _retention_tables(dh)
    s0, u0 = _meta_state(pad_meta(pm), pad_meta(cos[:N_META]), pad_meta(sin[:N_META]), kdec, r, dh)
    mix = _mixer(p, cos[N_META:], sin[N_META:], dmask, qdec, kdec, cdec, conv_w[0], s0, u0,
                 bsz, seq, r, dh)
    h2 = _outproj(mix, w_out[0], x2d, min(1024, n), 512)

    rw = jnp.pad(router_w[0], ((0, 0), (0, ROUTE_LANES - N_EXPERTS)))
    rwh = rw.astype(BF16)
    rwl = (rw - rwh.astype(F32)).astype(BF16)
    rb = jnp.pad(router_b[0].astype(F32), (0, ROUTE_LANES - N_EXPERTS), constant_values=-1e30)[None, :]
    m, route, cnt = _router(h2, norm_ffn[0][None, :], rwh, rwl, rb, 256)

    ids = route[:, TOP_K:2 * TOP_K].astype(jnp.int32)
    rank = route[:, 2 * TOP_K:3 * TOP_K].astype(jnp.int32)
    counts = cnt[0, :N_EXPERTS].astype(jnp.int32)

    total_sub = (n * TOP_K + N_EXPERTS * (SUB_ROWS - 1) + SUB_ROWS - 1) // SUB_ROWS
    psub = (counts + SUB_ROWS - 1) // SUB_ROWS
    pend = jnp.cumsum(psub) * SUB_ROWS
    pstart = pend - psub * SUB_ROWS
    dest_flat = (pstart[ids] + rank).reshape(-1).astype(jnp.int32)
    zlo = jnp.concatenate([pstart + counts, pend[-1:]]).astype(jnp.int32)
    zhi = jnp.concatenate([pend, jnp.full((1,), total_sub * SUB_ROWS)]).astype(jnp.int32)

    n_groups_max = N_EXPERTS + total_sub // GROUP_SUBS
    ngrp = (psub + GROUP_SUBS - 1) // GROUP_SUBS
    gend = jnp.cumsum(ngrp)
    gstart = gend - ngrp
    n_groups = gend[-1]
    gidx = jnp.arange(n_groups_max, dtype=jnp.int32)
    glast = jnp.minimum(gidx, n_groups - 1)
    gexp = jnp.minimum(jnp.searchsorted(gend, glast, side="right"), N_EXPERTS - 1).astype(jnp.int32)
    gq = glast - gstart[gexp]
    gsub = (pstart[gexp] // SUB_ROWS + gq * GROUP_SUBS).astype(jnp.int32)
    gns = jnp.where(gidx < n_groups, jnp.minimum(psub[gexp] - gq * GROUP_SUBS, GROUP_SUBS), 0).astype(jnp.int32)
    tail = jnp.stack([pend[-1] // SUB_ROWS, jnp.asarray(total_sub, pend.dtype)]).astype(jnp.int32)

    xg = _dispatch(dest_flat, zlo, zhi, m, total_sub * SUB_ROWS, 128)
    yg = _experts(gexp, gsub, gns, tail, xg, w_gate[0], w_up[0], b_gate[0][:, None, :],
                  b_up[0][:, None, :], w_down[0], b_down[0][:, None, :], n_groups_max)
    out = _combine(dest_flat, route, h2, norm_final[None, :], yg, 128)
    return out.reshape(bsz, seq, d)
```

```python
import functools

import jax
import jax.numpy as jnp
from jax import lax
from jax.experimental import pallas as pl
from jax.experimental.pallas import tpu as pltpu

F32 = jnp.float32
BF16 = jnp.bfloat16

N_META = 16
RET_HEADS = 8
CHUNK = 128
CONV_K = 3
ROPE_BASE = 10000.0
N_EXPERTS = 32
TOP_K = 4
SWIGLU_LIMIT = 7.0
SWIGLU_ALPHA = 1.702
NORM_EPS = 1e-5
GN_EPS = 1e-6

LANES = 128
SUBLANES = 8
VMEM_LIMIT = 58 * 1024 * 1024

SUB_ROWS = 128
GROUP_SUBS = 9
GROUP_ROWS = GROUP_SUBS * SUB_ROWS
GATE_UP_COLS = 256
DOWN_COLS = 512
ROUTE_LANES = 128


def _params(sem, vmem=VMEM_LIMIT):
    return pltpu.CompilerParams(dimension_semantics=sem, vmem_limit_bytes=vmem)


def _prenorm_kernel(x_ref, g_ref, o_ref):
    x = x_ref[...]
    ms = jnp.mean(x * x, axis=-1, keepdims=True)
    o_ref[...] = (x * lax.rsqrt(ms + NORM_EPS) * g_ref[...]).astype(o_ref.dtype)


def _prenorm(x2d, g, tm):
    n, d = x2d.shape
    return pl.pallas_call(
        _prenorm_kernel,
        grid=(n // tm,),
        in_specs=[pl.BlockSpec((tm, d), lambda i: (i, 0)),
                  pl.BlockSpec((1, d), lambda i: (0, 0))],
        out_specs=pl.BlockSpec((tm, d), lambda i: (i, 0)),
        out_shape=jax.ShapeDtypeStruct((n, d), BF16),
        compiler_params=_params(("arbitrary",)),
        name="prenorm",
    )(x2d, g)


def _inproj_kernel(a_ref, am_ref, w_ref, o_ref, om_ref, wb_ref):
    @pl.when(pl.program_id(1) == 0)
    def _():
        wb_ref[...] = w_ref[...].astype(BF16)
        om_ref[...] = jnp.dot(am_ref[...], wb_ref[...], preferred_element_type=F32)

    o_ref[...] = jnp.dot(a_ref[...], wb_ref[...], preferred_element_type=F32)


def _inproj(a, a_meta, w, tm, tn):
    n, d = a.shape
    c = w.shape[1]
    nm = a_meta.shape[0]
    return pl.pallas_call(
        _inproj_kernel,
        grid=(c // tn, n // tm),
        in_specs=[pl.BlockSpec((tm, d), lambda j, i: (i, 0)),
                  pl.BlockSpec((nm, d), lambda j, i: (0, 0)),
                  pl.BlockSpec((d, tn), lambda j, i: (0, j))],
        out_specs=[pl.BlockSpec((tm, tn), lambda j, i: (i, j)),
                   pl.BlockSpec((nm, tn), lambda j, i: (0, j))],
        out_shape=[jax.ShapeDtypeStruct((n, c), F32),
                   jax.ShapeDtypeStruct((nm, c), F32)],
        scratch_shapes=[pltpu.VMEM((d, tn), BF16)],
        compiler_params=_params(("arbitrary", "arbitrary")),
        name="inproj",
    )(a, a_meta, w)


def _rotary_halves(ref, h, dh, cos, sin):
    half = dh // 2
    t1 = ref[:, h * dh:h * dh + half]
    t2 = ref[:, h * dh + half:(h + 1) * dh]
    return t1 * cos - t2 * sin, t2 * cos + t1 * sin


def _state_update(k1, k2, kdec, v_bf):
    kd = jnp.concatenate([k1 * kdec, k2 * kdec], axis=1).astype(BF16)
    return lax.dot_general(kd, v_bf, (((0,), (0,)), ((), ())), preferred_element_type=F32)


def _meta_kernel(k_ref, v_ref, cc_ref, ch_ref, cos_ref, sin_ref, kdec_ref, s0_ref, u0_ref, *, dh):
    cos = cos_ref[...]
    sin = sin_ref[...]
    scale = dh ** -0.5
    for h in range(RET_HEADS):
        k1, k2 = _rotary_halves(k_ref, h, dh, cos, sin)
        v_bf = v_ref[:, h * dh:(h + 1) * dh].astype(BF16)
        s0_ref[h] = _state_update(k1 * scale, k2 * scale, kdec_ref[h], v_bf)
    u0_ref[...] = cc_ref[CHUNK - SUBLANES:CHUNK, :] * ch_ref[CHUNK - SUBLANES:CHUNK, :]


def _meta_state(pm_pad, cosm, sinm, kdec, r, dh):
    col = lambda cb: pl.BlockSpec((CHUNK, r), lambda i: (0, cb))
    full2 = pl.BlockSpec((CHUNK, LANES), lambda i: (0, 0))
    return pl.pallas_call(
        functools.partial(_meta_kernel, dh=dh),
        grid=(1,),
        in_specs=[col(1), col(2), col(5), col(6), full2, full2,
                  pl.BlockSpec((RET_HEADS, CHUNK, LANES), lambda i: (0, 0, 0))],
        out_specs=[pl.BlockSpec((RET_HEADS, dh, dh), lambda i: (0, 0, 0)),
                   pl.BlockSpec((SUBLANES, r), lambda i: (0, 0))],
        out_shape=[jax.ShapeDtypeStruct((RET_HEADS, dh, dh), F32),
                   jax.ShapeDtypeStruct((SUBLANES, r), F32)],
        compiler_params=_params(("arbitrary",)),
        name="meta_state",
    )(pm_pad, pm_pad, pm_pad, pm_pad, cosm, sinm, kdec)


def _mixer_kernel(cdec_ref, q_ref, k_ref, v_ref, g_ref, cb_ref, cc_ref, ch_ref,
                  cos_ref, sin_ref, dmask_ref, qdec_ref, kdec_ref, cw_ref, s0_ref, u0_ref,
                  mix_ref, state_ref, uext_ref, *, dh, r):
    @pl.when(pl.program_id(1) == 0)
    def _():
        state_ref[...] = s0_ref[...]
        uext_ref[0:SUBLANES, :] = u0_ref[...]

    cos = cos_ref[...]
    sin = sin_ref[...]
    scale = dh ** -0.5
    for h in range(RET_HEADS):
        q1, q2 = _rotary_halves(q_ref, h, dh, cos, sin)
        k1, k2 = _rotary_halves(k_ref, h, dh, cos, sin)
        k1 = k1 * scale
        k2 = k2 * scale
        qb = jnp.concatenate([q1, q2], axis=1).astype(BF16)
        kb = jnp.concatenate([k1, k2], axis=1).astype(BF16)
        v_bf = v_ref[:, h * dh:(h + 1) * dh].astype(BF16)
        scores = lax.dot_general(qb, kb, (((1,), (1,)), ((), ())), preferred_element_type=F32)
        sm = (scores * dmask_ref[h]).astype(BF16)
        inner = jnp.dot(sm, v_bf, preferred_element_type=F32)
        qdec = qdec_ref[h]
        qd = jnp.concatenate([q1 * qdec, q2 * qdec], axis=1).astype(BF16)
        st = state_ref[h]
        cross = jnp.dot(qd, st.astype(BF16), preferred_element_type=F32)
        state_ref[h] = st * cdec_ref[h] + _state_update(k1, k2, kdec_ref[h], v_bf)
        y = inner + cross
        mu = jnp.mean(y, axis=-1, keepdims=True)
        yc = y - mu
        var = jnp.mean(yc * yc, axis=-1, keepdims=True)
        yn = yc * lax.rsqrt(var + GN_EPS)
        gh = g_ref[:, h * dh:(h + 1) * dh]
        mix_ref[:, h * dh:(h + 1) * dh] = (gh * jax.nn.sigmoid(gh) * yn).astype(mix_ref.dtype)

    cw = 512
    for cs in range(0, r, cw):
        sl = slice(cs, cs + cw)
        u = cc_ref[:, sl] * ch_ref[:, sl]
        uext_ref[SUBLANES:SUBLANES + CHUNK, sl] = u
        u1 = uext_ref[SUBLANES - 1:SUBLANES - 1 + CHUNK, sl]
        u2 = uext_ref[SUBLANES - 2:SUBLANES - 2 + CHUNK, sl]
        conv = cw_ref[0:1, sl] * u2 + cw_ref[1:2, sl] * u1 + cw_ref[2:3, sl] * u
        mix_ref[:, r + cs:r + cs + cw] = (cb_ref[:, sl] * conv).astype(mix_ref.dtype)
        uext_ref[0:SUBLANES, sl] = uext_ref[CHUNK:CHUNK + SUBLANES, sl]


def _mixer(p, cos, sin, dmask, qdec, kdec, cdec, conv_w, s0, u0, bsz, seq, r, dh):
    nc = seq // CHUNK
    d = 2 * r
    col = lambda cb: pl.BlockSpec((CHUNK, r), lambda b, c: (b * nc + c, cb))
    tab = pl.BlockSpec((CHUNK, LANES), lambda b, c: (c, 0))
    hconst = pl.BlockSpec((RET_HEADS, CHUNK, LANES), lambda b, c: (0, 0, 0))
    return pl.pallas_call(
        functools.partial(_mixer_kernel, dh=dh, r=r),
        grid=(bsz, nc),
        in_specs=[pl.BlockSpec(memory_space=pltpu.SMEM),
                  col(0), col(1), col(2), col(3), col(4), col(5), col(6),
                  tab, tab, hconst, hconst, hconst,
                  pl.BlockSpec((CONV_K, r), lambda b, c: (0, 0)),
                  pl.BlockSpec((RET_HEADS, dh, dh), lambda b, c: (0, 0, 0)),
                  pl.BlockSpec((SUBLANES, r), lambda b, c: (0, 0))],
        out_specs=pl.BlockSpec((CHUNK, d), lambda b, c: (b * nc + c, 0)),
        out_shape=jax.ShapeDtypeStruct((bsz * seq, d), BF16),
        scratch_shapes=[pltpu.VMEM((RET_HEADS, dh, dh), F32),
                        pltpu.VMEM((CHUNK + 2 * SUBLANES, r), F32)],
        compiler_params=_params(("arbitrary", "arbitrary")),
        name="mixer",
    )(cdec, p, p, p, p, p, p, p, cos, sin, dmask, qdec, kdec, conv_w, s0, u0)


def _outproj_kernel(a_ref, w_ref, x_ref, o_ref, wb_ref):
    @pl.when(pl.program_id(1) == 0)
    def _():
        wb_ref[...] = w_ref[...].astype(BF16)

    o_ref[...] = x_ref[...] + jnp.dot(a_ref[...], wb_ref[...], preferred_element_type=F32)


def _outproj(mix, w, x2d, tm, tn):
    n, d = mix.shape
    c = w.shape[1]
    return pl.pallas_call(
        _outproj_kernel,
        grid=(c // tn, n // tm),
        in_specs=[pl.BlockSpec((tm, d), lambda j, i: (i, 0)),
                  pl.BlockSpec((d, tn), lambda j, i: (0, j)),
                  pl.BlockSpec((tm, tn), lambda j, i: (i, j))],
        out_specs=pl.BlockSpec((tm, tn), lambda j, i: (i, j)),
        out_shape=jax.ShapeDtypeStruct((n, c), F32),
        scratch_shapes=[pltpu.VMEM((d, tn), BF16)],
        compiler_params=_params(("arbitrary", "arbitrary")),
        name="outproj",
    )(mix, w, x2d)


def _router_kernel(h_ref, g_ref, wh_ref, wl_ref, b_ref, m_ref, route_ref, cnt_ref, carry_ref, *, tr):
    @pl.when(pl.program_id(0) == 0)
    def _():
        carry_ref[...] = jnp.zeros_like(carry_ref)

    x = h_ref[...]
    ms = jnp.mean(x * x, axis=-1, keepdims=True)
    m = x * lax.rsqrt(ms + NORM_EPS) * g_ref[...]
    m_ref[...] = m

    mh = m.astype(BF16)
    ml = (m - mh.astype(F32)).astype(BF16)
    wh = wh_ref[...]
    logits = (jnp.dot(mh, wh, preferred_element_type=F32)
              + jnp.dot(ml, wh, preferred_element_type=F32)
              + jnp.dot(mh, wl_ref[...], preferred_element_type=F32)
              + b_ref[...])

    lane = lax.broadcasted_iota(jnp.int32, (tr, ROUTE_LANES), 1)
    lane_f = lane.astype(F32)
    work = logits
    vals, onehots = [], []
    for _ in range(TOP_K):
        mx = jnp.max(work, axis=-1, keepdims=True)
        idx = jnp.min(jnp.where(work == mx, lane_f, float(ROUTE_LANES)), axis=-1, keepdims=True)
        oh = lane_f == idx
        vals.append(mx)
        onehots.append(oh)
        work = jnp.where(oh, -jnp.inf, work)

    exps = [jnp.exp(v - vals[0]) for v in vals]
    denom = exps[0] + exps[1] + exps[2] + exps[3]
    gates = [e / denom for e in exps]

    chosen = onehots[0] | onehots[1] | onehots[2] | onehots[3]
    cmat = jnp.where(chosen, 1.0, 0.0).astype(BF16)
    row = lax.broadcasted_iota(jnp.int32, (tr, tr), 0)
    colm = lax.broadcasted_iota(jnp.int32, (tr, tr), 1)
    lower = jnp.where(colm < row, 1.0, 0.0).astype(BF16)
    carry = carry_ref[0:1, :]
    rank_e = jnp.dot(lower, cmat, preferred_element_type=F32) + carry
    new_carry = carry + jnp.sum(cmat.astype(F32), axis=0, keepdims=True)
    carry_ref[...] = jnp.broadcast_to(new_carry, carry_ref.shape)
    cnt_ref[...] = jnp.broadcast_to(new_carry, cnt_ref.shape)

    out = jnp.zeros((tr, ROUTE_LANES), F32)
    for k in range(TOP_K):
        idx_k = jnp.sum(jnp.where(onehots[k], lane_f, 0.0), axis=-1, keepdims=True)
        rank_k = jnp.sum(jnp.where(onehots[k], rank_e, 0.0), axis=-1, keepdims=True)
        out = jnp.where(lane == k, gates[k], out)
        out = jnp.where(lane == TOP_K + k, idx_k, out)
        out = jnp.where(lane == 2 * TOP_K + k, rank_k, out)
    route_ref[...] = out


def _router(h2, g, wh, wl, b, tr):
    n, d = h2.shape
    return pl.pallas_call(
        functools.partial(_router_kernel, tr=tr),
        grid=(n // tr,),
        in_specs=[pl.BlockSpec((tr, d), lambda i: (i, 0)),
                  pl.BlockSpec((1, d), lambda i: (0, 0)),
                  pl.BlockSpec((d, ROUTE_LANES), lambda i: (0, 0)),
                  pl.BlockSpec((d, ROUTE_LANES), lambda i: (0, 0)),
                  pl.BlockSpec((1, ROUTE_LANES), lambda i: (0, 0))],
        out_specs=[pl.BlockSpec((tr, d), lambda i: (i, 0)),
                   pl.BlockSpec((tr, ROUTE_LANES), lambda i: (i, 0)),
                   pl.BlockSpec((SUBLANES, ROUTE_LANES), lambda i: (0, 0))],
        out_shape=[jax.ShapeDtypeStruct((n, d), F32),
                   jax.ShapeDtypeStruct((n, ROUTE_LANES), F32),
                   jax.ShapeDtypeStruct((SUBLANES, ROUTE_LANES), F32)],
        scratch_shapes=[pltpu.VMEM((SUBLANES, ROUTE_LANES), F32)],
        compiler_params=_params(("arbitrary",)),
        name="router",
    )(h2, g, wh, wl, b)


def _row_copy(src_ref, src_row, dst_ref, dst_row, sem):
    return pltpu.make_async_copy(src_ref.at[pl.ds(src_row, 1), :], dst_ref.at[pl.ds(dst_row, 1), :], sem)


def _dispatch_kernel(dest_ref, zlo_ref, zhi_ref, m_ref, xg_ref, zero_ref, sem, zsem, *, tt):
    @pl.when(pl.program_id(0) == 0)
    def _():
        zero_ref[...] = jnp.zeros_like(zero_ref)

        def range_body(e, carry):
            def start_body(rr, c):
                _row_copy(zero_ref, 0, xg_ref, rr, zsem).start()
                return c

            def wait_body(rr, c):
                _row_copy(zero_ref, 0, xg_ref, rr, zsem).wait()
                return c

            lax.fori_loop(zlo_ref[e], zhi_ref[e], start_body, 0)
            lax.fori_loop(zlo_ref[e], zhi_ref[e], wait_body, 0)
            return carry

        lax.fori_loop(0, N_EXPERTS + 1, range_body, 0)

    def start_body(t, c):
        for k in range(TOP_K):
            _row_copy(m_ref, t, xg_ref, dest_ref[t * TOP_K + k], sem).start()
        return c

    def wait_body(t, c):
        for k in range(TOP_K):
            _row_copy(m_ref, t, xg_ref, dest_ref[t * TOP_K + k], sem).wait()
        return c

    lax.fori_loop(0, tt, start_body, 0)
    lax.fori_loop(0, tt, wait_body, 0)


def _dispatch(dest_flat, zlo, zhi, m, rows, tt):
    n, d = m.shape
    return pl.pallas_call(
        functools.partial(_dispatch_kernel, tt=tt),
        grid=(n // tt,),
        in_specs=[pl.BlockSpec((tt * TOP_K,), lambda i: (i,), memory_space=pltpu.SMEM),
                  pl.BlockSpec(memory_space=pltpu.SMEM),
                  pl.BlockSpec(memory_space=pltpu.SMEM),
                  pl.BlockSpec((tt, d), lambda i: (i, 0))],
        out_specs=pl.BlockSpec(memory_space=pl.ANY),
        out_shape=jax.ShapeDtypeStruct((rows, d), F32),
        scratch_shapes=[pltpu.VMEM((SUBLANES, d), F32),
                        pltpu.SemaphoreType.DMA(()),
                        pltpu.SemaphoreType.DMA(())],
        compiler_params=_params(("arbitrary",)),
        name="dispatch",
    )(dest_flat, zlo, zhi, m)


def _for_row_blocks(nsub, block_fn):
    n4 = nsub // 4
    rem = nsub - n4 * 4

    def body(i, c):
        block_fn(pl.multiple_of(i * (4 * SUB_ROWS), 4 * SUB_ROWS), 4 * SUB_ROWS)
        return c

    lax.fori_loop(0, n4, body, 0)
    base = n4 * (4 * SUB_ROWS)

    @pl.when(rem >= 2)
    def _():
        block_fn(pl.multiple_of(base, SUB_ROWS), 2 * SUB_ROWS)

    @pl.when(rem % 2 == 1)
    def _():
        block_fn(pl.multiple_of(base + (rem // 2) * (2 * SUB_ROWS), SUB_ROWS), SUB_ROWS)


def _expert_kernel(gexp_ref, gsub_ref, gns_ref, tail_ref,
                   xg_ref, wg_ref, wu_ref, bg_ref, bu_ref, wd_ref, bd_ref, yg_ref,
                   xb_ref, hid_ref, stage_ref, ybuf_ref, pend_ref,
                   xsem, ysem, *, nj, nn):
    g = pl.program_id(0)
    s = pl.program_id(1)
    tf = GATE_UP_COLS
    tn = DOWN_COLS
    ns = gns_ref[g]
    row0 = gsub_ref[g] * SUB_ROWS

    def aligned(v, m):
        return v if isinstance(v, int) else pl.multiple_of(v, m)

    def y_copy(sub, slot, col, first_row):
        return pltpu.make_async_copy(
            ybuf_ref.at[slot, pl.ds(aligned(sub * SUB_ROWS, SUB_ROWS), SUB_ROWS), :],
            yg_ref.at[pl.ds(aligned(first_row + sub * SUB_ROWS, SUB_ROWS), SUB_ROWS),
                      pl.ds(aligned(col * tn, tn), tn)],
            ysem.at[slot])

    def x_copy(first_row, sub, slot):
        return pltpu.make_async_copy(
            xg_ref.at[pl.ds(aligned(first_row + sub * SUB_ROWS, SUB_ROWS), SUB_ROWS), :],
            stage_ref.at[slot], xsem.at[slot])

    def stage_to_rows(sub, slot):
        rows = pl.ds(aligned(sub * SUB_ROWS, SUB_ROWS), SUB_ROWS)
        xb_ref[rows, :] = stage_ref[slot].astype(BF16)

    def wait_pending(slot):
        def body(i, c):
            y_copy(0, slot, 0, 0).wait()
            return c

        lax.fori_loop(0, pend_ref[slot], body, 0)
        pend_ref[slot] = 0

    @pl.when((g == 0) & (s == 0))
    def _():
        pend_ref[0] = 0
        pend_ref[1] = 0
        ybuf_ref[0, 0:SUB_ROWS, :] = jnp.zeros((SUB_ROWS, tn), F32)
        lo = tail_ref[0]
        hi = tail_ref[1]

        def start_body(i, c):
            for col in range(nn):
                y_copy(0, 0, col, i * SUB_ROWS).start()
            return c

        def wait_body(i, c):
            for col in range(nn):
                y_copy(0, 0, col, i * SUB_ROWS).wait()
            return c

        lax.fori_loop(lo, hi, start_body, 0)
        lax.fori_loop(lo, hi, wait_body, 0)

        x_copy(row0, 0, 0).start()

        def first_body(sub, c):
            slot = lax.rem(sub, 2)

            @pl.when(sub + 1 < ns)
            def _():
                x_copy(row0, sub + 1, 1 - slot).start()

            x_copy(row0, sub, slot).wait()
            stage_to_rows(sub, slot)
            return c

        lax.fori_loop(0, ns, first_body, 0)

    @pl.when(ns > 0)
    def _():
        @pl.when(s < nj)
        def _():
            bg = bg_ref[0]
            bu = bu_ref[0]

            def block(start, rows_n):
                rows = pl.ds(start, rows_n)
                x = xb_ref[rows, :]
                gt = jnp.dot(x, wg_ref[0].astype(BF16), preferred_element_type=F32) + bg
                up = jnp.dot(x, wu_ref[0].astype(BF16), preferred_element_type=F32) + bu
                gt = jnp.minimum(gt, SWIGLU_LIMIT)
                up = jnp.clip(up, -SWIGLU_LIMIT, SWIGLU_LIMIT)
                hid = (up + 1.0) * gt * jax.nn.sigmoid(SWIGLU_ALPHA * gt)
                hid_ref[s, rows, :] = hid.astype(hid_ref.dtype)

            _for_row_blocks(ns, block)

        @pl.when(s >= nj)
        def _():
            col = s - nj
            slot = lax.rem(col, 2)
            bd = bd_ref[0]
            wait_pending(slot)

            last_g = pl.num_programs(0) - 1
            g_next = jnp.minimum(g + 1, last_g)
            ns_next = jnp.where(g < last_g, gns_ref[g_next], 0)
            row0_next = gsub_ref[g_next] * SUB_ROWS
            for k in range(2):
                @pl.when(2 * col + k < ns_next)
                def _():
                    x_copy(row0_next, 2 * col + k, k).start()

            def block(start, rows_n):
                rows = pl.ds(start, rows_n)
                acc = jnp.dot(hid_ref[0, rows, :], wd_ref[0, 0:tf, :].astype(BF16),
                              preferred_element_type=F32)
                for c in range(1, nj):
                    acc = acc + jnp.dot(hid_ref[c, rows, :], wd_ref[0, c * tf:(c + 1) * tf, :].astype(BF16),
                                        preferred_element_type=F32)
                ybuf_ref[slot, rows, :] = acc + bd

            _for_row_blocks(ns, block)

            def start_body(sub, c):
                y_copy(sub, slot, col, row0).start()
                return c

            lax.fori_loop(0, ns, start_body, 0)
            pend_ref[slot] = ns

            for k in range(2):
                @pl.when(2 * col + k < ns_next)
                def _():
                    x_copy(row0_next, 2 * col + k, k).wait()
                    stage_to_rows(2 * col + k, k)

    @pl.when((g == pl.num_programs(0) - 1) & (s == nj + nn - 1))
    def _():
        wait_pending(0)
        wait_pending(1)


def _experts(gexp, gsub, gns, tail, xg, w_gate, w_up, b_gate, b_up, w_down, b_down, n_groups_max):
    rows, d = xg.shape
    f = w_gate.shape[2]
    tf, tn = GATE_UP_COLS, DOWN_COLS
    nj, nn = f // tf, d // tn
    assert GROUP_SUBS <= 2 * nn, "next group's rows are fetched two sub-blocks per down step"

    def up_idx(g, s, ge, gs, gn, tl):
        act = jnp.minimum(gn[g], 1)
        return (ge[g], 0, jnp.minimum(s, nj - 1) * act + (nj - 1) * (1 - act))

    def down_idx(g, s, ge, gs, gn, tl):
        act = jnp.minimum(gn[g], 1)
        return (ge[g], 0, jnp.maximum(s - nj, 0) * act + (nn - 1) * (1 - act))

    grid_spec = pltpu.PrefetchScalarGridSpec(
        num_scalar_prefetch=4,
        grid=(n_groups_max, nj + nn),
        in_specs=[
            pl.BlockSpec(memory_space=pl.ANY),
            pl.BlockSpec((1, d, tf), up_idx),
            pl.BlockSpec((1, d, tf), up_idx),
            pl.BlockSpec((1, 1, tf), up_idx),
            pl.BlockSpec((1, 1, tf), up_idx),
            pl.BlockSpec((1, f, tn), down_idx),
            pl.BlockSpec((1, 1, tn), down_idx),
        ],
        out_specs=pl.BlockSpec(memory_space=pl.ANY),
        scratch_shapes=[pltpu.VMEM((GROUP_ROWS, d), BF16),
                        pltpu.VMEM((nj, GROUP_ROWS, tf), BF16),
                        pltpu.VMEM((2, SUB_ROWS, d), F32),
                        pltpu.VMEM((2, GROUP_ROWS, tn), F32),
                        pltpu.SMEM((2,), jnp.int32),
                        pltpu.SemaphoreType.DMA((2,)),
                        pltpu.SemaphoreType.DMA((2,))],
    )
    return pl.pallas_call(
        functools.partial(_expert_kernel, nj=nj, nn=nn),
        grid_spec=grid_spec,
        out_shape=jax.ShapeDtypeStruct((rows, d), F32),
        compiler_params=_params(("arbitrary", "arbitrary")),
        name="experts",
    )(gexp, gsub, gns, tail, xg, w_gate, w_up, b_gate, b_up, w_down, b_down)


def _combine_kernel(dcur_ref, dnext_ref, route_ref, h_ref, g_ref, yg_ref, o_ref, buf_ref, sem, *, tt):
    i = pl.program_id(0)
    n_steps = pl.num_programs(0)
    slot = lax.rem(i, 2)

    def start_tile(dref, sl):
        def body(t, c):
            for k in range(TOP_K):
                pltpu.make_async_copy(yg_ref.at[pl.ds(dref[t * TOP_K + k], 1), :],
                                      buf_ref.at[sl, k, pl.ds(t, 1), :], sem.at[sl]).start()
            return c

        lax.fori_loop(0, tt, body, 0)

    @pl.when(i == 0)
    def _():
        start_tile(dcur_ref, 0)

    @pl.when(i + 1 < n_steps)
    def _():
        start_tile(dnext_ref, 1 - slot)

    def wait_body(t, c):
        for k in range(TOP_K):
            pltpu.make_async_copy(yg_ref.at[pl.ds(0, 1), :],
                                  buf_ref.at[slot, k, pl.ds(t, 1), :], sem.at[slot]).wait()
        return c

    lax.fori_loop(0, tt, wait_body, 0)

    route = route_ref[...]
    acc = h_ref[...]
    for k in range(TOP_K):
        acc = acc + buf_ref[slot, k] * route[:, k:k + 1]
    ms = jnp.mean(acc * acc, axis=-1, keepdims=True)
    o_ref[...] = acc * lax.rsqrt(ms + NORM_EPS) * g_ref[...]


def _combine(dest_flat, route, h2, g, yg, tt):
    n, d = h2.shape
    steps = n // tt
    return pl.pallas_call(
        functools.partial(_combine_kernel, tt=tt),
        grid=(steps,),
        in_specs=[pl.BlockSpec((tt * TOP_K,), lambda i: (i,), memory_space=pltpu.SMEM),
                  pl.BlockSpec((tt * TOP_K,), lambda i: (jnp.minimum(i + 1, steps - 1),),
                               memory_space=pltpu.SMEM),
                  pl.BlockSpec((tt, ROUTE_LANES), lambda i: (i, 0)),
                  pl.BlockSpec((tt, d), lambda i: (i, 0)),
                  pl.BlockSpec((1, d), lambda i: (0, 0)),
                  pl.BlockSpec(memory_space=pl.ANY)],
        out_specs=pl.BlockSpec((tt, d), lambda i: (i, 0)),
        out_shape=jax.ShapeDtypeStruct((n, d), F32),
        scratch_shapes=[pltpu.VMEM((2, TOP_K, tt, d), F32),
                        pltpu.SemaphoreType.DMA((2,))],
        compiler_params=_params(("arbitrary",)),
        name="combine",
    )(dest_flat, dest_flat, route, h2, g, yg)


def _retention_tables(dh):
    lg = jnp.log1p(-jnp.exp2(-5.0 - jnp.arange(RET_HEADS, dtype=F32)))
    i = jnp.arange(CHUNK, dtype=F32)
    rel = i[:, None] - i[None, :]
    causal = rel >= 0
    dmask = jnp.where(causal[None], jnp.exp(jnp.where(causal, rel, 0.0)[None] * lg[:, None, None]), 0.0)
    q_dec = jnp.exp((i + 1.0)[None, :] * lg[:, None])
    k_dec = jnp.exp((CHUNK - 1.0 - i)[None, :] * lg[:, None])
    c_dec = jnp.exp(CHUNK * lg)
    bcast = lambda t: jnp.broadcast_to(t[:, :, None], (RET_HEADS, CHUNK, LANES))
    return dmask, bcast(q_dec), bcast(k_dec), c_dec


def _rotary_tables(length, dh):
    half = dh // 2
    inv = ROPE_BASE ** (-jnp.arange(half, dtype=F32) / half)
    ang = jnp.arange(length, dtype=jnp.int32).astype(F32)[:, None] * inv[None, :]
    return jnp.cos(ang), jnp.sin(ang)


def kernel(x, meta_tokens, norm_mix, w_in, conv_w, w_out, norm_ffn, router_w, router_b,
           w_gate, b_gate, w_up, b_up, w_down, b_down, norm_final):
    bsz, seq, d = x.shape
    assert w_in.shape[0] == 1, "single-layer stack expected"
    assert meta_tokens.shape[0] == N_META and seq % CHUNK == 0
    r = d // 2
    dh = r // RET_HEADS
    assert dh // 2 == LANES
    n = bsz * seq
    x2d = x.reshape(n, d)

    a = _prenorm(x2d, norm_mix[0][None, :], 256)
    a_meta = _prenorm(meta_tokens.astype(x.dtype), norm_mix[0][None, :], N_META)
    p, pm = _inproj(a, a_meta, w_in[0], min(1024, n), 512)

    cos, sin = _rotary_tables(N_META + seq, dh)
    pad_meta = lambda t: jnp.pad(t, ((CHUNK - N_META, 0), (0, 0)))
    dmask, qdec, kdec, cdec = _retention_tables(dh)
    s0, u0 = _meta_state(pad_meta(pm), pad_meta(cos[:N_META]), pad_meta(sin[:N_META]), kdec, r, dh)
    mix = _mixer(p, cos[N_META:], sin[N_META:], dmask, qdec, kdec, cdec, conv_w[0], s0, u0,
                 bsz, seq, r, dh)
    h2 = _outproj(mix, w_out[0], x2d, min(1024, n), 512)

    rw = jnp.pad(router_w[0], ((0, 0), (0, ROUTE_LANES - N_EXPERTS)))
    rwh = rw.astype(BF16)
    rwl = (rw - rwh.astype(F32)).astype(BF16)
    rb = jnp.pad(router_b[0].astype(F32), (0, ROUTE_LANES - N_EXPERTS), constant_values=-1e30)[None, :]
    m, route, cnt = _router(h2, norm_ffn[0][None, :], rwh, rwl, rb, 256)

    ids = route[:, TOP_K:2 * TOP_K].astype(jnp.int32)
    rank = route[:, 2 * TOP_K:3 * TOP_K].astype(jnp.int32)
    counts = cnt[0, :N_EXPERTS].astype(jnp.int32)

    total_sub = (n * TOP_K + N_EXPERTS * (SUB_ROWS - 1) + SUB_ROWS - 1) // SUB_ROWS
    psub = (counts + SUB_ROWS - 1) // SUB_ROWS
    pend = jnp.cumsum(psub) * SUB_ROWS
    pstart = pend - psub * SUB_ROWS
    dest_flat = (pstart[ids] + rank).reshape(-1).astype(jnp.int32)
    zlo = jnp.concatenate([pstart + counts, pend[-1:]]).astype(jnp.int32)
    zhi = jnp.concatenate([pend, jnp.full((1,), total_sub * SUB_ROWS)]).astype(jnp.int32)

    n_groups_max = N_EXPERTS + total_sub // GROUP_SUBS
    ngrp = (psub + GROUP_SUBS - 1) // GROUP_SUBS
    gend = jnp.cumsum(ngrp)
    gstart = gend - ngrp
    n_groups = gend[-1]
    gidx = jnp.arange(n_groups_max, dtype=jnp.int32)
    glast = jnp.minimum(gidx, n_groups - 1)
    gexp = jnp.minimum(jnp.searchsorted(gend, glast, side="right"), N_EXPERTS - 1).astype(jnp.int32)
    gq = glast - gstart[gexp]
    gsub = (pstart[gexp] // SUB_ROWS + gq * GROUP_SUBS).astype(jnp.int32)
    gns = jnp.where(gidx < n_groups, jnp.minimum(psub[gexp] - gq * GROUP_SUBS, GROUP_SUBS), 0).astype(jnp.int32)
    tail = jnp.stack([pend[-1] // SUB_ROWS, jnp.asarray(total_sub, pend.dtype)]).astype(jnp.int32)

    xg = _dispatch(dest_flat, zlo, zhi, m, total_sub * SUB_ROWS, 128)
    yg = _experts(gexp, gsub, gns, tail, xg, w_gate[0], w_up[0], b_gate[0][:, None, :],
                  b_up[0][:, None, :], w_down[0], b_down[0][:, None, :], n_groups_max)
    out = _combine(dest_flat, route, h2, norm_final[None, :], yg, 128)
    return out.reshape(bsz, seq, d)
```

```python
import functools

import jax
import jax.numpy as jnp
from jax import lax
from jax.experimental import pallas as pl
from jax.experimental.pallas import tpu as pltpu

F32 = jnp.float32
BF16 = jnp.bfloat16

N_META = 16
RET_HEADS = 8
CHUNK = 128
CONV_K = 3
ROPE_BASE = 10000.0
N_EXPERTS = 32
TOP_K = 4
SWIGLU_LIMIT = 7.0
SWIGLU_ALPHA = 1.702
NORM_EPS = 1e-5
GN_EPS = 1e-6

LANES = 128
SUBLANES = 8
VMEM_LIMIT = 58 * 1024 * 1024

SUB_ROWS = 128
GROUP_SUBS = 9
GROUP_ROWS = GROUP_SUBS * SUB_ROWS
GATE_UP_COLS = 256
DOWN_COLS = 512
ROUTE_LANES = 128


def _params(sem, vmem=VMEM_LIMIT):
    return pltpu.CompilerParams(dimension_semantics=sem, vmem_limit_bytes=vmem)


def _prenorm_kernel(x_ref, g_ref, o_ref):
    x = x_ref[...]
    ms = jnp.mean(x * x, axis=-1, keepdims=True)
    o_ref[...] = (x * lax.rsqrt(ms + NORM_EPS) * g_ref[...]).astype(o_ref.dtype)


def _prenorm(x2d, g, tm):
    n, d = x2d.shape
    return pl.pallas_call(
        _prenorm_kernel,
        grid=(n // tm,),
        in_specs=[pl.BlockSpec((tm, d), lambda i: (i, 0)),
                  pl.BlockSpec((1, d), lambda i: (0, 0))],
        out_specs=pl.BlockSpec((tm, d), lambda i: (i, 0)),
        out_shape=jax.ShapeDtypeStruct((n, d), BF16),
        compiler_params=_params(("arbitrary",)),
        name="prenorm",
    )(x2d, g)


def _inproj_kernel(a_ref, am_ref, w_ref, o_ref, om_ref):
    @pl.when(pl.program_id(1) == 0)
    def _():
        om_ref[...] = jnp.dot(am_ref[...], w_ref[...].astype(BF16), preferred_element_type=F32)

    o_ref[...] = jnp.dot(a_ref[...], w_ref[...].astype(BF16), preferred_element_type=F32)


def _inproj(a, a_meta, w, tm, tn):
    n, d = a.shape
    c = w.shape[1]
    nm = a_meta.shape[0]
    return pl.pallas_call(
        _inproj_kernel,
        grid=(c // tn, n // tm),
        in_specs=[pl.BlockSpec((tm, d), lambda j, i: (i, 0)),
                  pl.BlockSpec((nm, d), lambda j, i: (0, 0)),
                  pl.BlockSpec((d, tn), lambda j, i: (0, j))],
        out_specs=[pl.BlockSpec((tm, tn), lambda j, i: (i, j)),
                   pl.BlockSpec((nm, tn), lambda j, i: (0, j))],
        out_shape=[jax.ShapeDtypeStruct((n, c), F32),
                   jax.ShapeDtypeStruct((nm, c), F32)],
        compiler_params=_params(("arbitrary", "arbitrary")),
        name="inproj",
    )(a, a_meta, w)


def _rotary_halves(ref, h, dh, cos, sin):
    half = dh // 2
    t1 = ref[:, h * dh:h * dh + half]
    t2 = ref[:, h * dh + half:(h + 1) * dh]
    return t1 * cos - t2 * sin, t2 * cos + t1 * sin


def _state_update(k1, k2, kdec, v_bf):
    kd = jnp.concatenate([k1 * kdec, k2 * kdec], axis=1).astype(BF16)
    return lax.dot_general(kd, v_bf, (((0,), (0,)), ((), ())), preferred_element_type=F32)


def _meta_kernel(k_ref, v_ref, cc_ref, ch_ref, cos_ref, sin_ref, kdec_ref, s0_ref, u0_ref, *, dh):
    cos = cos_ref[...]
    sin = sin_ref[...]
    scale = dh ** -0.5
    for h in range(RET_HEADS):
        k1, k2 = _rotary_halves(k_ref, h, dh, cos, sin)
        v_bf = v_ref[:, h * dh:(h + 1) * dh].astype(BF16)
        s0_ref[h] = _state_update(k1 * scale, k2 * scale, kdec_ref[h], v_bf)
    u0_ref[...] = cc_ref[CHUNK - SUBLANES:CHUNK, :] * ch_ref[CHUNK - SUBLANES:CHUNK, :]


def _meta_state(pm_pad, cosm, sinm, kdec, r, dh):
    col = lambda cb: pl.BlockSpec((CHUNK, r), lambda i: (0, cb))
    full2 = pl.BlockSpec((CHUNK, LANES), lambda i: (0, 0))
    return pl.pallas_call(
        functools.partial(_meta_kernel, dh=dh),
        grid=(1,),
        in_specs=[col(1), col(2), col(5), col(6), full2, full2,
                  pl.BlockSpec((RET_HEADS, CHUNK, LANES), lambda i: (0, 0, 0))],
        out_specs=[pl.BlockSpec((RET_HEADS, dh, dh), lambda i: (0, 0, 0)),
                   pl.BlockSpec((SUBLANES, r), lambda i: (0, 0))],
        out_shape=[jax.ShapeDtypeStruct((RET_HEADS, dh, dh), F32),
                   jax.ShapeDtypeStruct((SUBLANES, r), F32)],
        compiler_params=_params(("arbitrary",)),
        name="meta_state",
    )(pm_pad, pm_pad, pm_pad, pm_pad, cosm, sinm, kdec)


def _mixer_kernel(cdec_ref, q_ref, k_ref, v_ref, g_ref, cb_ref, cc_ref, ch_ref,
                  cos_ref, sin_ref, dmask_ref, qdec_ref, kdec_ref, cw_ref, s0_ref, u0_ref,
                  mix_ref, state_ref, uext_ref, *, dh, r):
    @pl.when(pl.program_id(1) == 0)
    def _():
        state_ref[...] = s0_ref[...]
        uext_ref[0:SUBLANES, :] = u0_ref[...]

    cos = cos_ref[...]
    sin = sin_ref[...]
    scale = dh ** -0.5
    for h in range(RET_HEADS):
        q1, q2 = _rotary_halves(q_ref, h, dh, cos, sin)
        k1, k2 = _rotary_halves(k_ref, h, dh, cos, sin)
        k1 = k1 * scale
        k2 = k2 * scale
        qb = jnp.concatenate([q1, q2], axis=1).astype(BF16)
        kb = jnp.concatenate([k1, k2], axis=1).astype(BF16)
        v_bf = v_ref[:, h * dh:(h + 1) * dh].astype(BF16)
        scores = lax.dot_general(qb, kb, (((1,), (1,)), ((), ())), preferred_element_type=F32)
        sm = (scores * dmask_ref[h]).astype(BF16)
        inner = jnp.dot(sm, v_bf, preferred_element_type=F32)
        qdec = qdec_ref[h]
        qd = jnp.concatenate([q1 * qdec, q2 * qdec], axis=1).astype(BF16)
        st = state_ref[h]
        cross = jnp.dot(qd, st.astype(BF16), preferred_element_type=F32)
        state_ref[h] = st * cdec_ref[h] + _state_update(k1, k2, kdec_ref[h], v_bf)
        y = inner + cross
        mu = jnp.mean(y, axis=-1, keepdims=True)
        yc = y - mu
        var = jnp.mean(yc * yc, axis=-1, keepdims=True)
        yn = yc * lax.rsqrt(var + GN_EPS)
        gh = g_ref[:, h * dh:(h + 1) * dh]
        mix_ref[:, h * dh:(h + 1) * dh] = (gh * jax.nn.sigmoid(gh) * yn).astype(mix_ref.dtype)

    cw = 512
    for cs in range(0, r, cw):
        sl = slice(cs, cs + cw)
        u = cc_ref[:, sl] * ch_ref[:, sl]
        uext_ref[SUBLANES:SUBLANES + CHUNK, sl] = u
        u1 = uext_ref[SUBLANES - 1:SUBLANES - 1 + CHUNK, sl]
        u2 = uext_ref[SUBLANES - 2:SUBLANES - 2 + CHUNK, sl]
        conv = cw_ref[0:1, sl] * u2 + cw_ref[1:2, sl] * u1 + cw_ref[2:3, sl] * u
        mix_ref[:, r + cs:r + cs + cw] = (cb_ref[:, sl] * conv).astype(mix_ref.dtype)
        uext_ref[0:SUBLANES, sl] = uext_ref[CHUNK:CHUNK + SUBLANES, sl]


def _mixer(p, cos, sin, dmask, qdec, kdec, cdec, conv_w, s0, u0, bsz, seq, r, dh):
    nc = seq // CHUNK
    d = 2 * r
    col = lambda cb: pl.BlockSpec((CHUNK, r), lambda b, c: (b * nc + c, cb))
    tab = pl.BlockSpec((CHUNK, LANES), lambda b, c: (c, 0))
    hconst = pl.BlockSpec((RET_HEADS, CHUNK, LANES), lambda b, c: (0, 0, 0))
    return pl.pallas_call(
        functools.partial(_mixer_kernel, dh=dh, r=r),
        grid=(bsz, nc),
        in_specs=[pl.BlockSpec(memory_space=pltpu.SMEM),
                  col(0), col(1), col(2), col(3), col(4), col(5), col(6),
                  tab, tab, hconst, hconst, hconst,
                  pl.BlockSpec((CONV_K, r), lambda b, c: (0, 0)),
                  pl.BlockSpec((RET_HEADS, dh, dh), lambda b, c: (0, 0, 0)),
                  pl.BlockSpec((SUBLANES, r), lambda b, c: (0, 0))],
        out_specs=pl.BlockSpec((CHUNK, d), lambda b, c: (b * nc + c, 0)),
        out_shape=jax.ShapeDtypeStruct((bsz * seq, d), BF16),
        scratch_shapes=[pltpu.VMEM((RET_HEADS, dh, dh), F32),
                        pltpu.VMEM((CHUNK + 2 * SUBLANES, r), F32)],
        compiler_params=_params(("arbitrary", "arbitrary")),
        name="mixer",
    )(cdec, p, p, p, p, p, p, p, cos, sin, dmask, qdec, kdec, conv_w, s0, u0)


def _outproj_kernel(a_ref, w_ref, x_ref, o_ref):
    o_ref[...] = x_ref[...] + jnp.dot(a_ref[...], w_ref[...].astype(BF16), preferred_element_type=F32)


def _outproj(mix, w, x2d, tm, tn):
    n, d = mix.shape
    c = w.shape[1]
    return pl.pallas_call(
        _outproj_kernel,
        grid=(c // tn, n // tm),
        in_specs=[pl.BlockSpec((tm, d), lambda j, i: (i, 0)),
                  pl.BlockSpec((d, tn), lambda j, i: (0, j)),
                  pl.BlockSpec((tm, tn), lambda j, i: (i, j))],
        out_specs=pl.BlockSpec((tm, tn), lambda j, i: (i, j)),
        out_shape=jax.ShapeDtypeStruct((n, c), F32),
        compiler_params=_params(("arbitrary", "arbitrary")),
        name="outproj",
    )(mix, w, x2d)


def _pack_bf16_pairs(x):
    half = x.shape[1] // 2
    bits = lax.bitcast_convert_type(x.astype(BF16).astype(F32), jnp.uint32)
    return bits[:, half:] | (bits[:, :half] >> 16)


def _unpack_bf16_pairs(w):
    lo = lax.bitcast_convert_type(w << 16, F32)
    hi = lax.bitcast_convert_type(w & jnp.uint32(0xFFFF0000), F32)
    return lo.astype(BF16), hi.astype(BF16)


def _router_kernel(h_ref, g_ref, wh_ref, wl_ref, b_ref, m_ref, route_ref, cnt_ref, carry_ref, *, tr):
    @pl.when(pl.program_id(0) == 0)
    def _():
        carry_ref[...] = jnp.zeros_like(carry_ref)

    x = h_ref[...]
    ms = jnp.mean(x * x, axis=-1, keepdims=True)
    m = x * lax.rsqrt(ms + NORM_EPS) * g_ref[...]
    m_ref[...] = _pack_bf16_pairs(m)

    mh = m.astype(BF16)
    ml = (m - mh.astype(F32)).astype(BF16)
    wh = wh_ref[...]
    logits = (jnp.dot(mh, wh, preferred_element_type=F32)
              + jnp.dot(ml, wh, preferred_element_type=F32)
              + jnp.dot(mh, wl_ref[...], preferred_element_type=F32)
              + b_ref[...])

    lane = lax.broadcasted_iota(jnp.int32, (tr, ROUTE_LANES), 1)
    lane_f = lane.astype(F32)
    work = logits
    vals, onehots = [], []
    for _ in range(TOP_K):
        mx = jnp.max(work, axis=-1, keepdims=True)
        idx = jnp.min(jnp.where(work == mx, lane_f, float(ROUTE_LANES)), axis=-1, keepdims=True)
        oh = lane_f == idx
        vals.append(mx)
        onehots.append(oh)
        work = jnp.where(oh, -jnp.inf, work)

    exps = [jnp.exp(v - vals[0]) for v in vals]
    denom = exps[0] + exps[1] + exps[2] + exps[3]
    gates = [e / denom for e in exps]

    chosen = onehots[0] | onehots[1] | onehots[2] | onehots[3]
    cmat = jnp.where(chosen, 1.0, 0.0).astype(BF16)
    row = lax.broadcasted_iota(jnp.int32, (tr, tr), 0)
    colm = lax.broadcasted_iota(jnp.int32, (tr, tr), 1)
    lower = jnp.where(colm < row, 1.0, 0.0).astype(BF16)
    carry = carry_ref[0:1, :]
    rank_e = jnp.dot(lower, cmat, preferred_element_type=F32) + carry
    new_carry = carry + jnp.sum(cmat.astype(F32), axis=0, keepdims=True)
    carry_ref[...] = jnp.broadcast_to(new_carry, carry_ref.shape)
    cnt_ref[...] = jnp.broadcast_to(new_carry, cnt_ref.shape)

    out = jnp.zeros((tr, ROUTE_LANES), F32)
    for k in range(TOP_K):
        idx_k = jnp.sum(jnp.where(onehots[k], lane_f, 0.0), axis=-1, keepdims=True)
        rank_k = jnp.sum(jnp.where(onehots[k], rank_e, 0.0), axis=-1, keepdims=True)
        out = jnp.where(lane == k, gates[k], out)
        out = jnp.where(lane == TOP_K + k, idx_k, out)
        out = jnp.where(lane == 2 * TOP_K + k, rank_k, out)
    route_ref[...] = out


def _router(h2, g, wh, wl, b, tr):
    n, d = h2.shape
    return pl.pallas_call(
        functools.partial(_router_kernel, tr=tr),
        grid=(n // tr,),
        in_specs=[pl.BlockSpec((tr, d), lambda i: (i, 0)),
                  pl.BlockSpec((1, d), lambda i: (0, 0)),
                  pl.BlockSpec((d, ROUTE_LANES), lambda i: (0, 0)),
                  pl.BlockSpec((d, ROUTE_LANES), lambda i: (0, 0)),
                  pl.BlockSpec((1, ROUTE_LANES), lambda i: (0, 0))],
        out_specs=[pl.BlockSpec((tr, d // 2), lambda i: (i, 0)),
                   pl.BlockSpec((tr, ROUTE_LANES), lambda i: (i, 0)),
                   pl.BlockSpec((SUBLANES, ROUTE_LANES), lambda i: (0, 0))],
        out_shape=[jax.ShapeDtypeStruct((n, d // 2), jnp.uint32),
                   jax.ShapeDtypeStruct((n, ROUTE_LANES), F32),
                   jax.ShapeDtypeStruct((SUBLANES, ROUTE_LANES), F32)],
        scratch_shapes=[pltpu.VMEM((SUBLANES, ROUTE_LANES), F32)],
        compiler_params=_params(("arbitrary",)),
        name="router",
    )(h2, g, wh, wl, b)


def _row_copy(src_ref, src_row, dst_ref, dst_row, sem):
    return pltpu.make_async_copy(src_ref.at[pl.ds(src_row, 1), :], dst_ref.at[pl.ds(dst_row, 1), :], sem)


def _dispatch_kernel(dest_ref, zlo_ref, zhi_ref, m_ref, xg_ref, zero_ref, sem, zsem, *, tt):
    @pl.when(pl.program_id(0) == 0)
    def _():
        zero_ref[...] = jnp.zeros_like(zero_ref)

        def range_body(e, carry):
            def start_body(rr, c):
                _row_copy(zero_ref, 0, xg_ref, rr, zsem).start()
                return c

            def wait_body(rr, c):
                _row_copy(zero_ref, 0, xg_ref, rr, zsem).wait()
                return c

            lax.fori_loop(zlo_ref[e], zhi_ref[e], start_body, 0)
            lax.fori_loop(zlo_ref[e], zhi_ref[e], wait_body, 0)
            return carry

        lax.fori_loop(0, N_EXPERTS + 1, range_body, 0)

    def start_body(t, c):
        for k in range(TOP_K):
            _row_copy(m_ref, t, xg_ref, dest_ref[t * TOP_K + k], sem).start()
        return c

    def wait_body(t, c):
        for k in range(TOP_K):
            _row_copy(m_ref, t, xg_ref, dest_ref[t * TOP_K + k], sem).wait()
        return c

    lax.fori_loop(0, tt, start_body, 0)
    lax.fori_loop(0, tt, wait_body, 0)


def _dispatch(dest_flat, zlo, zhi, m, rows, tt):
    n, d = m.shape
    return pl.pallas_call(
        functools.partial(_dispatch_kernel, tt=tt),
        grid=(n // tt,),
        in_specs=[pl.BlockSpec((tt * TOP_K,), lambda i: (i,), memory_space=pltpu.SMEM),
                  pl.BlockSpec(memory_space=pltpu.SMEM),
                  pl.BlockSpec(memory_space=pltpu.SMEM),
                  pl.BlockSpec((tt, d), lambda i: (i, 0))],
        out_specs=pl.BlockSpec(memory_space=pl.ANY),
        out_shape=jax.ShapeDtypeStruct((rows, d), m.dtype),
        scratch_shapes=[pltpu.VMEM((SUBLANES, d), m.dtype),
                        pltpu.SemaphoreType.DMA(()),
                        pltpu.SemaphoreType.DMA(())],
        compiler_params=_params(("arbitrary",)),
        name="dispatch",
    )(dest_flat, zlo, zhi, m)


def _for_row_blocks(nsub, block_fn):
    n4 = nsub // 4
    rem = nsub - n4 * 4

    def body(i, c):
        block_fn(pl.multiple_of(i * (4 * SUB_ROWS), 4 * SUB_ROWS), 4 * SUB_ROWS)
        return c

    lax.fori_loop(0, n4, body, 0)
    base = n4 * (4 * SUB_ROWS)

    @pl.when(rem >= 2)
    def _():
        block_fn(pl.multiple_of(base, SUB_ROWS), 2 * SUB_ROWS)

    @pl.when(rem % 2 == 1)
    def _():
        block_fn(pl.multiple_of(base + (rem // 2) * (2 * SUB_ROWS), SUB_ROWS), SUB_ROWS)


def _expert_kernel(gexp_ref, gsub_ref, gns_ref, tail_ref,
                   xg_ref, wg_ref, wu_ref, bg_ref, bu_ref, wd_ref, bd_ref, yg_ref,
                   xb_ref, hid_ref, stage_ref, ybuf_ref, pend_ref,
                   xsem, ysem, *, nj, nn):
    g = pl.program_id(0)
    s = pl.program_id(1)
    tf = GATE_UP_COLS
    tn = DOWN_COLS
    ns = gns_ref[g]
    row0 = gsub_ref[g] * SUB_ROWS

    def aligned(v, m):
        return v if isinstance(v, int) else pl.multiple_of(v, m)

    def y_copy(sub, slot, col, first_row):
        return pltpu.make_async_copy(
            ybuf_ref.at[slot, pl.ds(aligned(sub * SUB_ROWS, SUB_ROWS), SUB_ROWS), :],
            yg_ref.at[pl.ds(aligned(first_row + sub * SUB_ROWS, SUB_ROWS), SUB_ROWS),
                      pl.ds(aligned(col * tn, tn), tn)],
            ysem.at[slot])

    def x_copy(first_row, sub, slot):
        return pltpu.make_async_copy(
            xg_ref.at[pl.ds(aligned(first_row + sub * SUB_ROWS, SUB_ROWS), SUB_ROWS), :],
            stage_ref.at[slot], xsem.at[slot])

    def stage_to_rows(sub, slot):
        rows = pl.ds(aligned(sub * SUB_ROWS, SUB_ROWS), SUB_ROWS)
        lo, hi = _unpack_bf16_pairs(stage_ref[slot])
        half = lo.shape[1]
        xb_ref[rows, 0:half] = lo
        xb_ref[rows, half:2 * half] = hi

    def wait_pending(slot):
        def body(i, c):
            y_copy(0, slot, 0, 0).wait()
            return c

        lax.fori_loop(0, pend_ref[slot], body, 0)
        pend_ref[slot] = 0

    @pl.when((g == 0) & (s == 0))
    def _():
        pend_ref[0] = 0
        pend_ref[1] = 0
        ybuf_ref[0, 0:SUB_ROWS, :] = jnp.zeros((SUB_ROWS, tn), F32)
        lo = tail_ref[0]
        hi = tail_ref[1]

        def start_body(i, c):
            for col in range(nn):
                y_copy(0, 0, col, i * SUB_ROWS).start()
            return c

        def wait_body(i, c):
            for col in range(nn):
                y_copy(0, 0, col, i * SUB_ROWS).wait()
            return c

        lax.fori_loop(lo, hi, start_body, 0)
        lax.fori_loop(lo, hi, wait_body, 0)

        x_copy(row0, 0, 0).start()

        def first_body(sub, c):
            slot = lax.rem(sub, 2)

            @pl.when(sub + 1 < ns)
            def _():
                x_copy(row0, sub + 1, 1 - slot).start()

            x_copy(row0, sub, slot).wait()
            stage_to_rows(sub, slot)
            return c

        lax.fori_loop(0, ns, first_body, 0)

    @pl.when(ns > 0)
    def _():
        @pl.when(s < nj)
        def _():
            bg = bg_ref[0]
            bu = bu_ref[0]

            def block(start, rows_n):
                rows = pl.ds(start, rows_n)
                x = xb_ref[rows, :]
                gt = jnp.dot(x, wg_ref[0].astype(BF16), preferred_element_type=F32) + bg
                up = jnp.dot(x, wu_ref[0].astype(BF16), preferred_element_type=F32) + bu
                gt = jnp.minimum(gt, SWIGLU_LIMIT)
                up = jnp.clip(up, -SWIGLU_LIMIT, SWIGLU_LIMIT)
                hid = (up + 1.0) * gt * jax.nn.sigmoid(SWIGLU_ALPHA * gt)
                hid_ref[s, rows, :] = hid.astype(hid_ref.dtype)

            _for_row_blocks(ns, block)

        @pl.when(s >= nj)
        def _():
            col = s - nj
            slot = lax.rem(col, 2)
            bd = bd_ref[0]
            wait_pending(slot)

            last_g = pl.num_programs(0) - 1
            g_next = jnp.minimum(g + 1, last_g)
            ns_next = jnp.where(g < last_g, gns_ref[g_next], 0)
            row0_next = gsub_ref[g_next] * SUB_ROWS
            for k in range(2):
                @pl.when(2 * col + k < ns_next)
                def _():
                    x_copy(row0_next, 2 * col + k, k).start()

            def block(start, rows_n):
                rows = pl.ds(start, rows_n)
                acc = jnp.dot(hid_ref[0, rows, :], wd_ref[0, 0:tf, :].astype(BF16),
                              preferred_element_type=F32)
                for c in range(1, nj):
                    acc = acc + jnp.dot(hid_ref[c, rows, :], wd_ref[0, c * tf:(c + 1) * tf, :].astype(BF16),
                                        preferred_element_type=F32)
                ybuf_ref[slot, rows, :] = acc + bd

            _for_row_blocks(ns, block)

            def start_body(sub, c):
                y_copy(sub, slot, col, row0).start()
                return c

            lax.fori_loop(0, ns, start_body, 0)
            pend_ref[slot] = ns

            for k in range(2):
                @pl.when(2 * col + k < ns_next)
                def _():
                    x_copy(row0_next, 2 * col + k, k).wait()
                    stage_to_rows(2 * col + k, k)

    @pl.when((g == pl.num_programs(0) - 1) & (s == nj + nn - 1))
    def _():
        wait_pending(0)
        wait_pending(1)


def _experts(gexp, gsub, gns, tail, xg, w_gate, w_up, b_gate, b_up, w_down, b_down, n_groups):
    rows = xg.shape[0]
    d = w_gate.shape[1]
    f = w_gate.shape[2]
    tf, tn = GATE_UP_COLS, DOWN_COLS
    nj, nn = f // tf, d // tn
    assert GROUP_SUBS <= 2 * nn, "next group's rows are fetched two sub-blocks per down step"

    def up_idx(g, s, ge, gs, gn, tl):
        act = jnp.minimum(gn[g], 1)
        return (ge[g], 0, jnp.minimum(s, nj - 1) * act + (nj - 1) * (1 - act))

    def down_idx(g, s, ge, gs, gn, tl):
        act = jnp.minimum(gn[g], 1)
        return (ge[g], 0, jnp.maximum(s - nj, 0) * act + (nn - 1) * (1 - act))

    grid_spec = pltpu.PrefetchScalarGridSpec(
        num_scalar_prefetch=4,
        grid=(n_groups, nj + nn),
        in_specs=[
            pl.BlockSpec(memory_space=pl.ANY),
            pl.BlockSpec((1, d, tf), up_idx),
            pl.BlockSpec((1, d, tf), up_idx),
            pl.BlockSpec((1, 1, tf), up_idx),
            pl.BlockSpec((1, 1, tf), up_idx),
            pl.BlockSpec((1, f, tn), down_idx),
            pl.BlockSpec((1, 1, tn), down_idx),
        ],
        out_specs=pl.BlockSpec(memory_space=pl.ANY),
        scratch_shapes=[pltpu.VMEM((GROUP_ROWS, d), BF16),
                        pltpu.VMEM((nj, GROUP_ROWS, tf), BF16),
                        pltpu.VMEM((2, SUB_ROWS, d // 2), xg.dtype),
                        pltpu.VMEM((2, GROUP_ROWS, tn), F32),
                        pltpu.SMEM((2,), jnp.int32),
                        pltpu.SemaphoreType.DMA((2,)),
                        pltpu.SemaphoreType.DMA((2,))],
    )
    return pl.pallas_call(
        functools.partial(_expert_kernel, nj=nj, nn=nn),
        grid_spec=grid_spec,
        out_shape=jax.ShapeDtypeStruct((rows, d), F32),
        compiler_params=_params(("arbitrary", "arbitrary")),
        name="experts",
    )(gexp, gsub, gns, tail, xg, w_gate, w_up, b_gate, b_up, w_down, b_down)


def _combine_kernel(dcur_ref, dnext_ref, route_ref, h_ref, g_ref, yg_ref, o_ref, buf_ref, sem, *, tt):
    i = pl.program_id(0)
    n_steps = pl.num_programs(0)
    slot = lax.rem(i, 2)

    def start_tile(dref, sl):
        def body(t, c):
            for k in range(TOP_K):
                pltpu.make_async_copy(yg_ref.at[pl.ds(dref[t * TOP_K + k], 1), :],
                                      buf_ref.at[sl, k, pl.ds(t, 1), :], sem.at[sl]).start()
            return c

        lax.fori_loop(0, tt, body, 0)

    @pl.when(i == 0)
    def _():
        start_tile(dcur_ref, 0)

    @pl.when(i + 1 < n_steps)
    def _():
        start_tile(dnext_ref, 1 - slot)

    def wait_body(t, c):
        for k in range(TOP_K):
            pltpu.make_async_copy(yg_ref.at[pl.ds(0, 1), :],
                                  buf_ref.at[slot, k, pl.ds(t, 1), :], sem.at[slot]).wait()
        return c

    lax.fori_loop(0, tt, wait_body, 0)

    route = route_ref[...]
    acc = h_ref[...]
    for k in range(TOP_K):
        acc = acc + buf_ref[slot, k] * route[:, k:k + 1]
    ms = jnp.mean(acc * acc, axis=-1, keepdims=True)
    o_ref[...] = acc * lax.rsqrt(ms + NORM_EPS) * g_ref[...]


def _combine(dest_flat, route, h2, g, yg, tt):
    n, d = h2.shape
    steps = n // tt
    return pl.pallas_call(
        functools.partial(_combine_kernel, tt=tt),
        grid=(steps,),
        in_specs=[pl.BlockSpec((tt * TOP_K,), lambda i: (i,), memory_space=pltpu.SMEM),
                  pl.BlockSpec((tt * TOP_K,), lambda i: (jnp.minimum(i + 1, steps - 1),),
                               memory_space=pltpu.SMEM),
                  pl.BlockSpec((tt, ROUTE_LANES), lambda i: (i, 0)),
                  pl.BlockSpec((tt, d), lambda i: (i, 0)),
                  pl.BlockSpec((1, d), lambda i: (0, 0)),
                  pl.BlockSpec(memory_space=pl.ANY)],
        out_specs=pl.BlockSpec((tt, d), lambda i: (i, 0)),
        out_shape=jax.ShapeDtypeStruct((n, d), F32),
        scratch_shapes=[pltpu.VMEM((2, TOP_K, tt, d), F32),
                        pltpu.SemaphoreType.DMA((2,))],
        compiler_params=_params(("arbitrary",)),
        name="combine",
    )(dest_flat, dest_flat, route, h2, g, yg)


def _retention_tables(dh):
    lg = jnp.log1p(-jnp.exp2(-5.0 - jnp.arange(RET_HEADS, dtype=F32)))
    i = jnp.arange(CHUNK, dtype=F32)
    rel = i[:, None] - i[None, :]
    causal = rel >= 0
    dmask = jnp.where(causal[None], jnp.exp(jnp.where(causal, rel, 0.0)[None] * lg[:, None, None]), 0.0)
    q_dec = jnp.exp((i + 1.0)[None, :] * lg[:, None])
    k_dec = jnp.exp((CHUNK - 1.0 - i)[None, :] * lg[:, None])
    c_dec = jnp.exp(CHUNK * lg)
    bcast = lambda t: jnp.broadcast_to(t[:, :, None], (RET_HEADS, CHUNK, LANES))
    return dmask, bcast(q_dec), bcast(k_dec), c_dec


def _rotary_tables(length, dh):
    half = dh // 2
    inv = ROPE_BASE ** (-jnp.arange(half, dtype=F32) / half)
    ang = jnp.arange(length, dtype=jnp.int32).astype(F32)[:, None] * inv[None, :]
    return jnp.cos(ang), jnp.sin(ang)


def kernel(x, meta_tokens, norm_mix, w_in, conv_w, w_out, norm_ffn, router_w, router_b,
           w_gate, b_gate, w_up, b_up, w_down, b_down, norm_final):
    bsz, seq, d = x.shape
    assert w_in.shape[0] == 1, "single-layer stack expected"
    assert meta_tokens.shape[0] == N_META and seq % CHUNK == 0
    r = d // 2
    dh = r // RET_HEADS
    assert dh // 2 == LANES
    n = bsz * seq
    x2d = x.reshape(n, d)

    a = _prenorm(x2d, norm_mix[0][None, :], 256)
    a_meta = _prenorm(meta_tokens.astype(x.dtype), norm_mix[0][None, :], N_META)
    p, pm = _inproj(a, a_meta, w_in[0], 512, 1024)

    cos, sin = _rotary_tables(N_META + seq, dh)
    pad_meta = lambda t: jnp.pad(t, ((CHUNK - N_META, 0), (0, 0)))
    dmask, qdec, kdec, cdec = _retention_tables(dh)
    s0, u0 = _meta_state(pad_meta(pm), pad_meta(cos[:N_META]), pad_meta(sin[:N_META]), kdec, r, dh)
    mix = _mixer(p, cos[N_META:], sin[N_META:], dmask, qdec, kdec, cdec, conv_w[0], s0, u0,
                 bsz, seq, r, dh)
    h2 = _outproj(mix, w_out[0], x2d, 512, 1024)

    rw = jnp.pad(router_w[0], ((0, 0), (0, ROUTE_LANES - N_EXPERTS)))
    rwh = rw.astype(BF16)
    rwl = (rw - rwh.astype(F32)).astype(BF16)
    rb = jnp.pad(router_b[0].astype(F32), (0, ROUTE_LANES - N_EXPERTS), constant_values=-1e30)[None, :]
    m, route, cnt = _router(h2, norm_ffn[0][None, :], rwh, rwl, rb, 256)

    ids = route[:, TOP_K:2 * TOP_K].astype(jnp.int32)
    rank = route[:, 2 * TOP_K:3 * TOP_K].astype(jnp.int32)
    counts = cnt[0, :N_EXPERTS].astype(jnp.int32)

    total_sub = (n * TOP_K + N_EXPERTS * (SUB_ROWS - 1) + SUB_ROWS - 1) // SUB_ROWS
    psub = (counts + SUB_ROWS - 1) // SUB_ROWS
    pend = jnp.cumsum(psub) * SUB_ROWS
    pstart = pend - psub * SUB_ROWS
    dest_flat = (pstart[ids] + rank).reshape(-1).astype(jnp.int32)
    zlo = jnp.concatenate([pstart + counts, pend[-1:]]).astype(jnp.int32)
    zhi = jnp.concatenate([pend, jnp.full((1,), total_sub * SUB_ROWS)]).astype(jnp.int32)

    n_groups_max = N_EXPERTS + total_sub // GROUP_SUBS
    ngrp = (psub + GROUP_SUBS - 1) // GROUP_SUBS
    gend = jnp.cumsum(ngrp)
    gstart = gend - ngrp
    n_groups = gend[-1]
    gidx = jnp.arange(n_groups_max, dtype=jnp.int32)
    glast = jnp.minimum(gidx, n_groups - 1)
    gexp = jnp.minimum(jnp.searchsorted(gend, glast, side="right"), N_EXPERTS - 1).astype(jnp.int32)
    gq = glast - gstart[gexp]
    gsub = (pstart[gexp] // SUB_ROWS + gq * GROUP_SUBS).astype(jnp.int32)
    gns = jnp.where(gidx < n_groups, jnp.minimum(psub[gexp] - gq * GROUP_SUBS, GROUP_SUBS), 0).astype(jnp.int32)
    tail = jnp.stack([pend[-1] // SUB_ROWS, jnp.asarray(total_sub, pend.dtype)]).astype(jnp.int32)

    xg = _dispatch(dest_flat, zlo, zhi, m, total_sub * SUB_ROWS, 128)
    yg = _experts(gexp, gsub, gns, tail, xg, w_gate[0], w_up[0], b_gate[0][:, None, :],
                  b_up[0][:, None, :], w_down[0], b_down[0][:, None, :], n_groups.astype(jnp.int32))
    out = _combine(dest_flat, route, h2, norm_final[None, :], yg, 128)
    return out.reshape(bsz, seq, d)
```

```python
import functools

import jax
import jax.numpy as jnp
from jax import lax
from jax.experimental import pallas as pl
from jax.experimental.pallas import tpu as pltpu

F32 = jnp.float32
BF16 = jnp.bfloat16

N_META = 16
RET_HEADS = 8
CHUNK = 128
CONV_K = 3
ROPE_BASE = 10000.0
N_EXPERTS = 32
TOP_K = 4
SWIGLU_LIMIT = 7.0
SWIGLU_ALPHA = 1.702
NORM_EPS = 1e-5
GN_EPS = 1e-6

LANES = 128
SUBLANES = 8
VMEM_LIMIT = 58 * 1024 * 1024

SUB_ROWS = 128
GROUP_SUBS = 9
GROUP_ROWS = GROUP_SUBS * SUB_ROWS
GATE_UP_COLS = 256
DOWN_COLS = 512
ROUTE_LANES = 128


def _params(sem, vmem=VMEM_LIMIT):
    return pltpu.CompilerParams(dimension_semantics=sem, vmem_limit_bytes=vmem)


def _prenorm_kernel(x_ref, g_ref, o_ref):
    x = x_ref[...]
    ms = jnp.mean(x * x, axis=-1, keepdims=True)
    o_ref[...] = (x * lax.rsqrt(ms + NORM_EPS) * g_ref[...]).astype(o_ref.dtype)


def _prenorm(x2d, g, tm):
    n, d = x2d.shape
    return pl.pallas_call(
        _prenorm_kernel,
        grid=(n // tm,),
        in_specs=[pl.BlockSpec((tm, d), lambda i: (i, 0)),
                  pl.BlockSpec((1, d), lambda i: (0, 0))],
        out_specs=pl.BlockSpec((tm, d), lambda i: (i, 0)),
        out_shape=jax.ShapeDtypeStruct((n, d), BF16),
        compiler_params=_params(("arbitrary",)),
        name="prenorm",
    )(x2d, g)


def _inproj_kernel(a_ref, am_ref, w_ref, o_ref, om_ref):
    @pl.when(pl.program_id(1) == 0)
    def _():
        om_ref[...] = jnp.dot(am_ref[...], w_ref[...].astype(BF16), preferred_element_type=F32)

    o_ref[...] = jnp.dot(a_ref[...], w_ref[...].astype(BF16), preferred_element_type=F32)


def _inproj(a, a_meta, w, tm, tn):
    n, d = a.shape
    c = w.shape[1]
    nm = a_meta.shape[0]
    return pl.pallas_call(
        _inproj_kernel,
        grid=(c // tn, n // tm),
        in_specs=[pl.BlockSpec((tm, d), lambda j, i: (i, 0)),
                  pl.BlockSpec((nm, d), lambda j, i: (0, 0)),
                  pl.BlockSpec((d, tn), lambda j, i: (0, j))],
        out_specs=[pl.BlockSpec((tm, tn), lambda j, i: (i, j)),
                   pl.BlockSpec((nm, tn), lambda j, i: (0, j))],
        out_shape=[jax.ShapeDtypeStruct((n, c), F32),
                   jax.ShapeDtypeStruct((nm, c), F32)],
        compiler_params=_params(("arbitrary", "arbitrary")),
        name="inproj",
    )(a, a_meta, w)


def _rotary_halves(ref, h, dh, cos, sin):
    half = dh // 2
    t1 = ref[:, h * dh:h * dh + half]
    t2 = ref[:, h * dh + half:(h + 1) * dh]
    return t1 * cos - t2 * sin, t2 * cos + t1 * sin


def _state_update(k1, k2, kdec, v_bf):
    kd = jnp.concatenate([k1 * kdec, k2 * kdec], axis=1).astype(BF16)
    return lax.dot_general(kd, v_bf, (((0,), (0,)), ((), ())), preferred_element_type=F32)


def _meta_kernel(k_ref, v_ref, cc_ref, ch_ref, cos_ref, sin_ref, kdec_ref, s0_ref, u0_ref, *, dh):
    cos = cos_ref[...]
    sin = sin_ref[...]
    scale = dh ** -0.5
    for h in range(RET_HEADS):
        k1, k2 = _rotary_halves(k_ref, h, dh, cos, sin)
        v_bf = v_ref[:, h * dh:(h + 1) * dh].astype(BF16)
        s0_ref[h] = _state_update(k1 * scale, k2 * scale, kdec_ref[h], v_bf)
    u0_ref[...] = cc_ref[CHUNK - SUBLANES:CHUNK, :] * ch_ref[CHUNK - SUBLANES:CHUNK, :]


def _meta_state(pm_pad, cosm, sinm, kdec, r, dh):
    col = lambda cb: pl.BlockSpec((CHUNK, r), lambda i: (0, cb))
    full2 = pl.BlockSpec((CHUNK, LANES), lambda i: (0, 0))
    return pl.pallas_call(
        functools.partial(_meta_kernel, dh=dh),
        grid=(1,),
        in_specs=[col(1), col(2), col(5), col(6), full2, full2,
                  pl.BlockSpec((RET_HEADS, CHUNK, LANES), lambda i: (0, 0, 0))],
        out_specs=[pl.BlockSpec((RET_HEADS, dh, dh), lambda i: (0, 0, 0)),
                   pl.BlockSpec((SUBLANES, r), lambda i: (0, 0))],
        out_shape=[jax.ShapeDtypeStruct((RET_HEADS, dh, dh), F32),
                   jax.ShapeDtypeStruct((SUBLANES, r), F32)],
        compiler_params=_params(("arbitrary",)),
        name="meta_state",
    )(pm_pad, pm_pad, pm_pad, pm_pad, cosm, sinm, kdec)


def _mixer_kernel(cdec_ref, q_ref, k_ref, v_ref, g_ref, cb_ref, cc_ref, ch_ref,
                  cos_ref, sin_ref, dmask_ref, qdec_ref, kdec_ref, cw_ref, s0_ref, u0_ref,
                  mix_ref, state_ref, uext_ref, *, dh, r):
    @pl.when(pl.program_id(1) == 0)
    def _():
        state_ref[...] = s0_ref[...]
        uext_ref[0:SUBLANES, :] = u0_ref[...]

    cos = cos_ref[...]
    sin = sin_ref[...]
    scale = dh ** -0.5
    for h in range(RET_HEADS):
        q1, q2 = _rotary_halves(q_ref, h, dh, cos, sin)
        k1, k2 = _rotary_halves(k_ref, h, dh, cos, sin)
        k1 = k1 * scale
        k2 = k2 * scale
        qb = jnp.concatenate([q1, q2], axis=1).astype(BF16)
        kb = jnp.concatenate([k1, k2], axis=1).astype(BF16)
        v_bf = v_ref[:, h * dh:(h + 1) * dh].astype(BF16)
        scores = lax.dot_general(qb, kb, (((1,), (1,)), ((), ())), preferred_element_type=F32)
        sm = (scores * dmask_ref[h]).astype(BF16)
        inner = jnp.dot(sm, v_bf, preferred_element_type=F32)
        qdec = qdec_ref[h]
        qd = jnp.concatenate([q1 * qdec, q2 * qdec], axis=1).astype(BF16)
        st = state_ref[h]
        cross = jnp.dot(qd, st.astype(BF16), preferred_element_type=F32)
        state_ref[h] = st * cdec_ref[h] + _state_update(k1, k2, kdec_ref[h], v_bf)
        y = inner + cross
        mu = jnp.mean(y, axis=-1, keepdims=True)
        yc = y - mu
        var = jnp.mean(yc * yc, axis=-1, keepdims=True)
        yn = yc * lax.rsqrt(var + GN_EPS)
        gh = g_ref[:, h * dh:(h + 1) * dh]
        mix_ref[:, h * dh:(h + 1) * dh] = (gh * jax.nn.sigmoid(gh) * yn).astype(mix_ref.dtype)

    cw = 512
    for cs in range(0, r, cw):
        sl = slice(cs, cs + cw)
        u = cc_ref[:, sl] * ch_ref[:, sl]
        uext_ref[SUBLANES:SUBLANES + CHUNK, sl] = u
        u1 = uext_ref[SUBLANES - 1:SUBLANES - 1 + CHUNK, sl]
        u2 = uext_ref[SUBLANES - 2:SUBLANES - 2 + CHUNK, sl]
        conv = cw_ref[0:1, sl] * u2 + cw_ref[1:2, sl] * u1 + cw_ref[2:3, sl] * u
        mix_ref[:, r + cs:r + cs + cw] = (cb_ref[:, sl] * conv).astype(mix_ref.dtype)
        uext_ref[0:SUBLANES, sl] = uext_ref[CHUNK:CHUNK + SUBLANES, sl]


def _mixer(p, cos, sin, dmask, qdec, kdec, cdec, conv_w, s0, u0, bsz, seq, r, dh):
    nc = seq // CHUNK
    d = 2 * r
    col = lambda cb: pl.BlockSpec((CHUNK, r), lambda b, c: (b * nc + c, cb))
    tab = pl.BlockSpec((CHUNK, LANES), lambda b, c: (c, 0))
    hconst = pl.BlockSpec((RET_HEADS, CHUNK, LANES), lambda b, c: (0, 0, 0))
    return pl.pallas_call(
        functools.partial(_mixer_kernel, dh=dh, r=r),
        grid=(bsz, nc),
        in_specs=[pl.BlockSpec(memory_space=pltpu.SMEM),
                  col(0), col(1), col(2), col(3), col(4), col(5), col(6),
                  tab, tab, hconst, hconst, hconst,
                  pl.BlockSpec((CONV_K, r), lambda b, c: (0, 0)),
                  pl.BlockSpec((RET_HEADS, dh, dh), lambda b, c: (0, 0, 0)),
                  pl.BlockSpec((SUBLANES, r), lambda b, c: (0, 0))],
        out_specs=pl.BlockSpec((CHUNK, d), lambda b, c: (b * nc + c, 0)),
        out_shape=jax.ShapeDtypeStruct((bsz * seq, d), BF16),
        scratch_shapes=[pltpu.VMEM((RET_HEADS, dh, dh), F32),
                        pltpu.VMEM((CHUNK + 2 * SUBLANES, r), F32)],
        compiler_params=_params(("arbitrary", "arbitrary")),
        name="mixer",
    )(cdec, p, p, p, p, p, p, p, cos, sin, dmask, qdec, kdec, conv_w, s0, u0)


def _outproj_kernel(a_ref, w_ref, x_ref, o_ref):
    o_ref[...] = x_ref[...] + jnp.dot(a_ref[...], w_ref[...].astype(BF16), preferred_element_type=F32)


def _outproj(mix, w, x2d, tm, tn):
    n, d = mix.shape
    c = w.shape[1]
    return pl.pallas_call(
        _outproj_kernel,
        grid=(c // tn, n // tm),
        in_specs=[pl.BlockSpec((tm, d), lambda j, i: (i, 0)),
                  pl.BlockSpec((d, tn), lambda j, i: (0, j)),
                  pl.BlockSpec((tm, tn), lambda j, i: (i, j))],
        out_specs=pl.BlockSpec((tm, tn), lambda j, i: (i, j)),
        out_shape=jax.ShapeDtypeStruct((n, c), F32),
        compiler_params=_params(("arbitrary", "arbitrary")),
        name="outproj",
    )(mix, w, x2d)


def _pack_bf16_pairs(x):
    half = x.shape[1] // 2
    bits = lax.bitcast_convert_type(x.astype(BF16).astype(F32), jnp.uint32)
    return bits[:, half:] | (bits[:, :half] >> 16)


def _unpack_bf16_pairs(w):
    lo = lax.bitcast_convert_type(w << 16, F32)
    hi = lax.bitcast_convert_type(w & jnp.uint32(0xFFFF0000), F32)
    return lo.astype(BF16), hi.astype(BF16)


def _router_kernel(h_ref, g_ref, wh_ref, wl_ref, b_ref, m_ref, route_ref, cnt_ref, carry_ref, *, tr):
    @pl.when(pl.program_id(0) == 0)
    def _():
        carry_ref[...] = jnp.zeros_like(carry_ref)

    x = h_ref[...]
    ms = jnp.mean(x * x, axis=-1, keepdims=True)
    m = x * lax.rsqrt(ms + NORM_EPS) * g_ref[...]
    m_ref[...] = _pack_bf16_pairs(m)

    mh = m.astype(BF16)
    ml = (m - mh.astype(F32)).astype(BF16)
    wh = wh_ref[...]
    logits = (jnp.dot(mh, wh, preferred_element_type=F32)
              + jnp.dot(ml, wh, preferred_element_type=F32)
              + jnp.dot(mh, wl_ref[...], preferred_element_type=F32)
              + b_ref[...])

    lane = lax.broadcasted_iota(jnp.int32, (tr, ROUTE_LANES), 1)
    lane_f = lane.astype(F32)
    work = logits
    vals, onehots = [], []
    for _ in range(TOP_K):
        mx = jnp.max(work, axis=-1, keepdims=True)
        idx = jnp.min(jnp.where(work == mx, lane_f, float(ROUTE_LANES)), axis=-1, keepdims=True)
        oh = lane_f == idx
        vals.append(mx)
        onehots.append(oh)
        work = jnp.where(oh, -jnp.inf, work)

    exps = [jnp.exp(v - vals[0]) for v in vals]
    denom = exps[0] + exps[1] + exps[2] + exps[3]
    gates = [e / denom for e in exps]

    chosen = onehots[0] | onehots[1] | onehots[2] | onehots[3]
    cmat = jnp.where(chosen, 1.0, 0.0).astype(BF16)
    row = lax.broadcasted_iota(jnp.int32, (tr, tr), 0)
    colm = lax.broadcasted_iota(jnp.int32, (tr, tr), 1)
    lower = jnp.where(colm < row, 1.0, 0.0).astype(BF16)
    carry = carry_ref[0:1, :]
    rank_e = jnp.dot(lower, cmat, preferred_element_type=F32) + carry
    new_carry = carry + jnp.sum(cmat.astype(F32), axis=0, keepdims=True)
    carry_ref[...] = jnp.broadcast_to(new_carry, carry_ref.shape)
    cnt_ref[...] = jnp.broadcast_to(new_carry, cnt_ref.shape)

    out = jnp.zeros((tr, ROUTE_LANES), F32)
    for k in range(TOP_K):
        idx_k = jnp.sum(jnp.where(onehots[k], lane_f, 0.0), axis=-1, keepdims=True)
        rank_k = jnp.sum(jnp.where(onehots[k], rank_e, 0.0), axis=-1, keepdims=True)
        out = jnp.where(lane == k, gates[k], out)
        out = jnp.where(lane == TOP_K + k, idx_k, out)
        out = jnp.where(lane == 2 * TOP_K + k, rank_k, out)
    route_ref[...] = out


def _router(h2, g, wh, wl, b, tr):
    n, d = h2.shape
    return pl.pallas_call(
        functools.partial(_router_kernel, tr=tr),
        grid=(n // tr,),
        in_specs=[pl.BlockSpec((tr, d), lambda i: (i, 0)),
                  pl.BlockSpec((1, d), lambda i: (0, 0)),
                  pl.BlockSpec((d, ROUTE_LANES), lambda i: (0, 0)),
                  pl.BlockSpec((d, ROUTE_LANES), lambda i: (0, 0)),
                  pl.BlockSpec((1, ROUTE_LANES), lambda i: (0, 0))],
        out_specs=[pl.BlockSpec((tr, d // 2), lambda i: (i, 0)),
                   pl.BlockSpec((tr, ROUTE_LANES), lambda i: (i, 0)),
                   pl.BlockSpec((SUBLANES, ROUTE_LANES), lambda i: (0, 0))],
        out_shape=[jax.ShapeDtypeStruct((n, d // 2), jnp.uint32),
                   jax.ShapeDtypeStruct((n, ROUTE_LANES), F32),
                   jax.ShapeDtypeStruct((SUBLANES, ROUTE_LANES), F32)],
        scratch_shapes=[pltpu.VMEM((SUBLANES, ROUTE_LANES), F32)],
        compiler_params=_params(("arbitrary",)),
        name="router",
    )(h2, g, wh, wl, b)


def _row_copy(src_ref, src_row, dst_ref, dst_row, sem):
    return pltpu.make_async_copy(src_ref.at[pl.ds(src_row, 1), :], dst_ref.at[pl.ds(dst_row, 1), :], sem)


def _dispatch_kernel(dest_ref, zlo_ref, zhi_ref, m_ref, xg_ref, zero_ref, sem, zsem, *, tt):
    @pl.when(pl.program_id(0) == 0)
    def _():
        zero_ref[...] = jnp.zeros_like(zero_ref)

        def range_body(e, carry):
            def start_body(rr, c):
                _row_copy(zero_ref, 0, xg_ref, rr, zsem).start()
                return c

            def wait_body(rr, c):
                _row_copy(zero_ref, 0, xg_ref, rr, zsem).wait()
                return c

            lax.fori_loop(zlo_ref[e], zhi_ref[e], start_body, 0)
            lax.fori_loop(zlo_ref[e], zhi_ref[e], wait_body, 0)
            return carry

        lax.fori_loop(0, N_EXPERTS + 1, range_body, 0)

    def start_body(t, c):
        for k in range(TOP_K):
            _row_copy(m_ref, t, xg_ref, dest_ref[t * TOP_K + k], sem).start()
        return c

    def wait_body(t, c):
        for k in range(TOP_K):
            _row_copy(m_ref, t, xg_ref, dest_ref[t * TOP_K + k], sem).wait()
        return c

    lax.fori_loop(0, tt, start_body, 0)
    lax.fori_loop(0, tt, wait_body, 0)


def _dispatch(dest_flat, zlo, zhi, m, rows, tt):
    n, d = m.shape
    return pl.pallas_call(
        functools.partial(_dispatch_kernel, tt=tt),
        grid=(n // tt,),
        in_specs=[pl.BlockSpec((tt * TOP_K,), lambda i: (i,), memory_space=pltpu.SMEM),
                  pl.BlockSpec(memory_space=pltpu.SMEM),
                  pl.BlockSpec(memory_space=pltpu.SMEM),
                  pl.BlockSpec((tt, d), lambda i: (i, 0))],
        out_specs=pl.BlockSpec(memory_space=pl.ANY),
        out_shape=jax.ShapeDtypeStruct((rows, d), m.dtype),
        scratch_shapes=[pltpu.VMEM((SUBLANES, d), m.dtype),
                        pltpu.SemaphoreType.DMA(()),
                        pltpu.SemaphoreType.DMA(())],
        compiler_params=_params(("arbitrary",)),
        name="dispatch",
    )(dest_flat, zlo, zhi, m)


def _for_row_blocks(nsub, block_fn):
    n4 = nsub // 4
    rem = nsub - n4 * 4

    def body(i, c):
        block_fn(pl.multiple_of(i * (4 * SUB_ROWS), 4 * SUB_ROWS), 4 * SUB_ROWS)
        return c

    lax.fori_loop(0, n4, body, 0)
    base = n4 * (4 * SUB_ROWS)

    @pl.when(rem >= 2)
    def _():
        block_fn(pl.multiple_of(base, SUB_ROWS), 2 * SUB_ROWS)

    @pl.when(rem % 2 == 1)
    def _():
        block_fn(pl.multiple_of(base + (rem // 2) * (2 * SUB_ROWS), SUB_ROWS), SUB_ROWS)


def _expert_kernel(gexp_ref, gsub_ref, gns_ref, tail_ref,
                   xg_ref, wg_ref, wu_ref, bg_ref, bu_ref, wd_ref, bd_ref, yg_ref,
                   xb_ref, hid_ref, stage_ref, ybuf_ref, pend_ref,
                   xsem, ysem, *, nj, nn):
    g = pl.program_id(0)
    s = pl.program_id(1)
    tf = GATE_UP_COLS
    tn = DOWN_COLS
    ns = gns_ref[g]
    row0 = gsub_ref[g] * SUB_ROWS

    def aligned(v, m):
        return v if isinstance(v, int) else pl.multiple_of(v, m)

    def y_copy(sub, slot, col, first_row):
        return pltpu.make_async_copy(
            ybuf_ref.at[slot, pl.ds(aligned(sub * SUB_ROWS, SUB_ROWS), SUB_ROWS), :],
            yg_ref.at[pl.ds(aligned(first_row + sub * SUB_ROWS, SUB_ROWS), SUB_ROWS),
                      pl.ds(aligned(col * tn, tn), tn)],
            ysem.at[slot])

    def x_copy(first_row, sub, slot):
        return pltpu.make_async_copy(
            xg_ref.at[pl.ds(aligned(first_row + sub * SUB_ROWS, SUB_ROWS), SUB_ROWS), :],
            stage_ref.at[slot], xsem.at[slot])

    def stage_to_rows(buf, sub, slot):
        rows = pl.ds(aligned(sub * SUB_ROWS, SUB_ROWS), SUB_ROWS)
        lo, hi = _unpack_bf16_pairs(stage_ref[slot])
        half = lo.shape[1]
        xb_ref[buf, rows, 0:half] = lo
        xb_ref[buf, rows, half:2 * half] = hi

    def wait_pending(slot):
        def body(i, c):
            y_copy(0, slot, 0, 0).wait()
            return c

        lax.fori_loop(0, pend_ref[slot], body, 0)
        pend_ref[slot] = 0

    @pl.when((g == 0) & (s == 0))
    def _():
        pend_ref[0] = 0
        pend_ref[1] = 0
        ybuf_ref[0, 0:SUB_ROWS, :] = jnp.zeros((SUB_ROWS, tn), F32)
        lo = tail_ref[0]
        hi = tail_ref[1]

        def start_body(i, c):
            for col in range(nn):
                y_copy(0, 0, col, i * SUB_ROWS).start()
            return c

        def wait_body(i, c):
            for col in range(nn):
                y_copy(0, 0, col, i * SUB_ROWS).wait()
            return c

        lax.fori_loop(lo, hi, start_body, 0)
        lax.fori_loop(lo, hi, wait_body, 0)

        x_copy(row0, 0, 0).start()

        def first_body(sub, c):
            slot = lax.rem(sub, 2)

            @pl.when(sub + 1 < ns)
            def _():
                x_copy(row0, sub + 1, 1 - slot).start()

            x_copy(row0, sub, slot).wait()
            stage_to_rows(0, sub, slot)
            return c

        lax.fori_loop(0, ns, first_body, 0)

    @pl.when(ns > 0)
    def _():
        @pl.when(s < nj)
        def _():
            bg = bg_ref[0]
            bu = bu_ref[0]
            cur = lax.rem(g, 2)

            last_g = pl.num_programs(0) - 1
            g_next = jnp.minimum(g + 1, last_g)
            ns_next = jnp.where(g < last_g, gns_ref[g_next], 0)
            row0_next = gsub_ref[g_next] * SUB_ROWS
            for k in range(2):
                @pl.when(2 * s + k < ns_next)
                def _():
                    x_copy(row0_next, 2 * s + k, k).start()

            def block(start, rows_n):
                rows = pl.ds(start, rows_n)
                x = xb_ref[cur, rows, :]
                gt = jnp.dot(x, wg_ref[0].astype(BF16), preferred_element_type=F32) + bg
                up = jnp.dot(x, wu_ref[0].astype(BF16), preferred_element_type=F32) + bu
                gt = jnp.minimum(gt, SWIGLU_LIMIT)
                up = jnp.clip(up, -SWIGLU_LIMIT, SWIGLU_LIMIT)
                hid = (up + 1.0) * gt * jax.nn.sigmoid(SWIGLU_ALPHA * gt)
                hid_ref[s, rows, :] = hid.astype(hid_ref.dtype)

            _for_row_blocks(ns, block)

            for k in range(2):
                @pl.when(2 * s + k < ns_next)
                def _():
                    x_copy(row0_next, 2 * s + k, k).wait()
                    stage_to_rows(1 - cur, 2 * s + k, k)

        @pl.when(s >= nj)
        def _():
            col = s - nj
            slot = lax.rem(col, 2)
            bd = bd_ref[0]
            wait_pending(slot)

            def block(start, rows_n):
                rows = pl.ds(start, rows_n)
                acc = jnp.dot(hid_ref[0, rows, :], wd_ref[0, 0:tf, :].astype(BF16),
                              preferred_element_type=F32)
                for c in range(1, nj):
                    acc = acc + jnp.dot(hid_ref[c, rows, :], wd_ref[0, c * tf:(c + 1) * tf, :].astype(BF16),
                                        preferred_element_type=F32)
                ybuf_ref[slot, rows, :] = acc + bd

            _for_row_blocks(ns, block)

            def start_body(sub, c):
                y_copy(sub, slot, col, row0).start()
                return c

            lax.fori_loop(0, ns, start_body, 0)
            pend_ref[slot] = ns

    @pl.when((g == pl.num_programs(0) - 1) & (s == nj + nn - 1))
    def _():
        wait_pending(0)
        wait_pending(1)


def _experts(gexp, gsub, gns, tail, xg, w_gate, w_up, b_gate, b_up, w_down, b_down, n_groups):
    rows = xg.shape[0]
    d = w_gate.shape[1]
    f = w_gate.shape[2]
    tf, tn = GATE_UP_COLS, DOWN_COLS
    nj, nn = f // tf, d // tn
    assert GROUP_SUBS <= 2 * nj, "next group's rows are fetched two sub-blocks per gate/up step"

    def up_idx(g, s, ge, gs, gn, tl):
        act = jnp.minimum(gn[g], 1)
        return (ge[g], 0, jnp.minimum(s, nj - 1) * act + (nj - 1) * (1 - act))

    def down_idx(g, s, ge, gs, gn, tl):
        act = jnp.minimum(gn[g], 1)
        return (ge[g], 0, jnp.maximum(s - nj, 0) * act + (nn - 1) * (1 - act))

    grid_spec = pltpu.PrefetchScalarGridSpec(
        num_scalar_prefetch=4,
        grid=(n_groups, nj + nn),
        in_specs=[
            pl.BlockSpec(memory_space=pl.ANY),
            pl.BlockSpec((1, d, tf), up_idx),
            pl.BlockSpec((1, d, tf), up_idx),
            pl.BlockSpec((1, 1, tf), up_idx),
            pl.BlockSpec((1, 1, tf), up_idx),
            pl.BlockSpec((1, f, tn), down_idx),
            pl.BlockSpec((1, 1, tn), down_idx),
        ],
        out_specs=pl.BlockSpec(memory_space=pl.ANY),
        scratch_shapes=[pltpu.VMEM((2, GROUP_ROWS, d), BF16),
                        pltpu.VMEM((nj, GROUP_ROWS, tf), BF16),
                        pltpu.VMEM((2, SUB_ROWS, d // 2), xg.dtype),
                        pltpu.VMEM((2, GROUP_ROWS, tn), F32),
                        pltpu.SMEM((2,), jnp.int32),
                        pltpu.SemaphoreType.DMA((2,)),
                        pltpu.SemaphoreType.DMA((2,))],
    )
    return pl.pallas_call(
        functools.partial(_expert_kernel, nj=nj, nn=nn),
        grid_spec=grid_spec,
        out_shape=jax.ShapeDtypeStruct((rows, d), F32),
        compiler_params=_params(("arbitrary", "arbitrary")),
        name="experts",
    )(gexp, gsub, gns, tail, xg, w_gate, w_up, b_gate, b_up, w_down, b_down)


def _combine_kernel(dcur_ref, dnext_ref, route_ref, h_ref, g_ref, yg_ref, o_ref, buf_ref, sem, *, tt):
    i = pl.program_id(0)
    n_steps = pl.num_programs(0)
    slot = lax.rem(i, 2)

    def start_tile(dref, sl):
        def body(t, c):
            for k in range(TOP_K):
                pltpu.make_async_copy(yg_ref.at[pl.ds(dref[t * TOP_K + k], 1), :],
                                      buf_ref.at[sl, k, pl.ds(t, 1), :], sem.at[sl]).start()
            return c

        lax.fori_loop(0, tt, body, 0)

    @pl.when(i == 0)
    def _():
        start_tile(dcur_ref, 0)

    @pl.when(i + 1 < n_steps)
    def _():
        start_tile(dnext_ref, 1 - slot)

    def wait_body(t, c):
        for k in range(TOP_K):
            pltpu.make_async_copy(yg_ref.at[pl.ds(0, 1), :],
                                  buf_ref.at[slot, k, pl.ds(t, 1), :], sem.at[slot]).wait()
        return c

    lax.fori_loop(0, tt, wait_body, 0)

    d = o_ref.shape[1]
    cw = 2 * LANES
    gates = [route_ref[:, k:k + 1] for k in range(TOP_K)]
    ssq = jnp.zeros((tt, 1), F32)
    for cs in range(0, d, cw):
        acc = h_ref[:, cs:cs + cw]
        for k in range(TOP_K):
            acc = acc + buf_ref[slot, k, :, cs:cs + cw] * gates[k]
        o_ref[:, cs:cs + cw] = acc
        ssq = ssq + jnp.sum(acc * acc, axis=-1, keepdims=True)
    scale = lax.rsqrt(ssq * (1.0 / d) + NORM_EPS)
    for cs in range(0, d, cw):
        o_ref[:, cs:cs + cw] = o_ref[:, cs:cs + cw] * scale * g_ref[:, cs:cs + cw]


def _combine(dest_flat, route, h2, g, yg, tt):
    n, d = h2.shape
    steps = n // tt
    return pl.pallas_call(
        functools.partial(_combine_kernel, tt=tt),
        grid=(steps,),
        in_specs=[pl.BlockSpec((tt * TOP_K,), lambda i: (i,), memory_space=pltpu.SMEM),
                  pl.BlockSpec((tt * TOP_K,), lambda i: (jnp.minimum(i + 1, steps - 1),),
                               memory_space=pltpu.SMEM),
                  pl.BlockSpec((tt, ROUTE_LANES), lambda i: (i, 0)),
                  pl.BlockSpec((tt, d), lambda i: (i, 0)),
                  pl.BlockSpec((1, d), lambda i: (0, 0)),
                  pl.BlockSpec(memory_space=pl.ANY)],
        out_specs=pl.BlockSpec((tt, d), lambda i: (i, 0)),
        out_shape=jax.ShapeDtypeStruct((n, d), F32),
        scratch_shapes=[pltpu.VMEM((2, TOP_K, tt, d), F32),
                        pltpu.SemaphoreType.DMA((2,))],
        compiler_params=_params(("arbitrary",)),
        name="combine",
    )(dest_flat, dest_flat, route, h2, g, yg)


def _retention_tables(dh):
    lg = jnp.log1p(-jnp.exp2(-5.0 - jnp.arange(RET_HEADS, dtype=F32)))
    i = jnp.arange(CHUNK, dtype=F32)
    rel = i[:, None] - i[None, :]
    causal = rel >= 0
    dmask = jnp.where(causal[None], jnp.exp(jnp.where(causal, rel, 0.0)[None] * lg[:, None, None]), 0.0)
    q_dec = jnp.exp((i + 1.0)[None, :] * lg[:, None])
    k_dec = jnp.exp((CHUNK - 1.0 - i)[None, :] * lg[:, None])
    c_dec = jnp.exp(CHUNK * lg)
    bcast = lambda t: jnp.broadcast_to(t[:, :, None], (RET_HEADS, CHUNK, LANES))
    return dmask, bcast(q_dec), bcast(k_dec), c_dec


def _rotary_tables(length, dh):
    half = dh // 2
    inv = ROPE_BASE ** (-jnp.arange(half, dtype=F32) / half)
    ang = jnp.arange(length, dtype=jnp.int32).astype(F32)[:, None] * inv[None, :]
    return jnp.cos(ang), jnp.sin(ang)


def kernel(x, meta_tokens, norm_mix, w_in, conv_w, w_out, norm_ffn, router_w, router_b,
           w_gate, b_gate, w_up, b_up, w_down, b_down, norm_final):
    bsz, seq, d = x.shape
    assert w_in.shape[0] == 1, "single-layer stack expected"
    assert meta_tokens.shape[0] == N_META and seq % CHUNK == 0
    r = d // 2
    dh = r // RET_HEADS
    assert dh // 2 == LANES
    n = bsz * seq
    x2d = x.reshape(n, d)

    a = _prenorm(x2d, norm_mix[0][None, :], 256)
    a_meta = _prenorm(meta_tokens.astype(x.dtype), norm_mix[0][None, :], N_META)
    p, pm = _inproj(a, a_meta, w_in[0], 512, 1024)

    cos, sin = _rotary_tables(N_META + seq, dh)
    pad_meta = lambda t: jnp.pad(t, ((CHUNK - N_META, 0), (0, 0)))
    dmask, qdec, kdec, cdec = _retention_tables(dh)
    s0, u0 = _meta_state(pad_meta(pm), pad_meta(cos[:N_META]), pad_meta(sin[:N_META]), kdec, r, dh)
    mix = _mixer(p, cos[N_META:], sin[N_META:], dmask, qdec, kdec, cdec, conv_w[0], s0, u0,
                 bsz, seq, r, dh)
    h2 = _outproj(mix, w_out[0], x2d, 512, 1024)

    rw = jnp.pad(router_w[0], ((0, 0), (0, ROUTE_LANES - N_EXPERTS)))
    rwh = rw.astype(BF16)
    rwl = (rw - rwh.astype(F32)).astype(BF16)
    rb = jnp.pad(router_b[0].astype(F32), (0, ROUTE_LANES - N_EXPERTS), constant_values=-1e30)[None, :]
    m, route, cnt = _router(h2, norm_ffn[0][None, :], rwh, rwl, rb, 256)

    ids = route[:, TOP_K:2 * TOP_K].astype(jnp.int32)
    rank = route[:, 2 * TOP_K:3 * TOP_K].astype(jnp.int32)
    counts = cnt[0, :N_EXPERTS].astype(jnp.int32)

    total_sub = (n * TOP_K + N_EXPERTS * (SUB_ROWS - 1) + SUB_ROWS - 1) // SUB_ROWS
    psub = (counts + SUB_ROWS - 1) // SUB_ROWS
    pend = jnp.cumsum(psub) * SUB_ROWS
    pstart = pend - psub * SUB_ROWS
    dest_flat = (pstart[ids] + rank).reshape(-1).astype(jnp.int32)
    zlo = jnp.concatenate([pstart + counts, pend[-1:]]).astype(jnp.int32)
    zhi = jnp.concatenate([pend, jnp.full((1,), total_sub * SUB_ROWS)]).astype(jnp.int32)

    n_groups_max = N_EXPERTS + total_sub // GROUP_SUBS
    ngrp = (psub + GROUP_SUBS - 1) // GROUP_SUBS
    gend = jnp.cumsum(ngrp)
    gstart = gend - ngrp
    n_groups = gend[-1]
    gidx = jnp.arange(n_groups_max, dtype=jnp.int32)
    glast = jnp.minimum(gidx, n_groups - 1)
    gexp = jnp.minimum(jnp.searchsorted(gend, glast, side="right"), N_EXPERTS - 1).astype(jnp.int32)
    gq = glast - gstart[gexp]
    gsub = (pstart[gexp] // SUB_ROWS + gq * GROUP_SUBS).astype(jnp.int32)
    gns = jnp.where(gidx < n_groups, jnp.minimum(psub[gexp] - gq * GROUP_SUBS, GROUP_SUBS), 0).astype(jnp.int32)
    tail = jnp.stack([pend[-1] // SUB_ROWS, jnp.asarray(total_sub, pend.dtype)]).astype(jnp.int32)

    xg = _dispatch(dest_flat, zlo, zhi, m, total_sub * SUB_ROWS, 128)
    yg = _experts(gexp, gsub, gns, tail, xg, w_gate[0], w_up[0], b_gate[0][:, None, :],
                  b_up[0][:, None, :], w_down[0], b_down[0][:, None, :], n_groups.astype(jnp.int32))
    out = _combine(dest_flat, route, h2, norm_final[None, :], yg, 128)
    return out.reshape(bsz, seq, d)
```

```python
import functools

import jax
import jax.numpy as jnp
from jax import lax
from jax.experimental import pallas as pl
from jax.experimental.pallas import tpu as pltpu

F32 = jnp.float32
BF16 = jnp.bfloat16

N_META = 16
RET_HEADS = 8
CHUNK = 128
CONV_K = 3
ROPE_BASE = 10000.0
N_EXPERTS = 32
TOP_K = 4
SWIGLU_LIMIT = 7.0
SWIGLU_ALPHA = 1.702
NORM_EPS = 1e-5
GN_EPS = 1e-6

LANES = 128
SUBLANES = 8
VMEM_LIMIT = 58 * 1024 * 1024

SUB_ROWS = 128
GROUP_SUBS = 9
GROUP_ROWS = GROUP_SUBS * SUB_ROWS
GATE_UP_COLS = 256
DOWN_COLS = 512
ROUTE_LANES = 128


def _params(sem, vmem=VMEM_LIMIT):
    return pltpu.CompilerParams(dimension_semantics=sem, vmem_limit_bytes=vmem)


def _prenorm_kernel(x_ref, g_ref, o_ref):
    x = x_ref[...]
    ms = jnp.mean(x * x, axis=-1, keepdims=True)
    o_ref[...] = (x * lax.rsqrt(ms + NORM_EPS) * g_ref[...]).astype(o_ref.dtype)


def _prenorm(x2d, g, tm):
    n, d = x2d.shape
    return pl.pallas_call(
        _prenorm_kernel,
        grid=(n // tm,),
        in_specs=[pl.BlockSpec((tm, d), lambda i: (i, 0)),
                  pl.BlockSpec((1, d), lambda i: (0, 0))],
        out_specs=pl.BlockSpec((tm, d), lambda i: (i, 0)),
        out_shape=jax.ShapeDtypeStruct((n, d), BF16),
        compiler_params=_params(("arbitrary",)),
        name="prenorm",
    )(x2d, g)


def _inproj_kernel(a_ref, am_ref, w_ref, o_ref, om_ref):
    @pl.when(pl.program_id(1) == 0)
    def _():
        om_ref[...] = jnp.dot(am_ref[...], w_ref[...].astype(BF16), preferred_element_type=F32)

    o_ref[...] = jnp.dot(a_ref[...], w_ref[...].astype(BF16), preferred_element_type=F32)


def _inproj(a, a_meta, w, tm, tn):
    n, d = a.shape
    c = w.shape[1]
    nm = a_meta.shape[0]
    return pl.pallas_call(
        _inproj_kernel,
        grid=(c // tn, n // tm),
        in_specs=[pl.BlockSpec((tm, d), lambda j, i: (i, 0)),
                  pl.BlockSpec((nm, d), lambda j, i: (0, 0)),
                  pl.BlockSpec((d, tn), lambda j, i: (0, j))],
        out_specs=[pl.BlockSpec((tm, tn), lambda j, i: (i, j)),
                   pl.BlockSpec((nm, tn), lambda j, i: (0, j))],
        out_shape=[jax.ShapeDtypeStruct((n, c), F32),
                   jax.ShapeDtypeStruct((nm, c), F32)],
        compiler_params=_params(("arbitrary", "arbitrary")),
        name="inproj",
    )(a, a_meta, w)


def _rotary_halves(ref, h, dh, cos, sin):
    half = dh // 2
    t1 = ref[:, h * dh:h * dh + half]
    t2 = ref[:, h * dh + half:(h + 1) * dh]
    return t1 * cos - t2 * sin, t2 * cos + t1 * sin


def _state_update(k1, k2, kdec, v_bf):
    kd = jnp.concatenate([k1 * kdec, k2 * kdec], axis=1).astype(BF16)
    return lax.dot_general(kd, v_bf, (((0,), (0,)), ((), ())), preferred_element_type=F32)


def _meta_kernel(k_ref, v_ref, cc_ref, ch_ref, cos_ref, sin_ref, kdec_ref, s0_ref, u0_ref, *, dh):
    cos = cos_ref[...]
    sin = sin_ref[...]
    scale = dh ** -0.5
    for h in range(RET_HEADS):
        k1, k2 = _rotary_halves(k_ref, h, dh, cos, sin)
        v_bf = v_ref[:, h * dh:(h + 1) * dh].astype(BF16)
        s0_ref[h] = _state_update(k1 * scale, k2 * scale, kdec_ref[h], v_bf)
    u0_ref[...] = cc_ref[CHUNK - SUBLANES:CHUNK, :] * ch_ref[CHUNK - SUBLANES:CHUNK, :]


def _meta_state(pm_pad, cosm, sinm, kdec, r, dh):
    col = lambda cb: pl.BlockSpec((CHUNK, r), lambda i: (0, cb))
    full2 = pl.BlockSpec((CHUNK, LANES), lambda i: (0, 0))
    return pl.pallas_call(
        functools.partial(_meta_kernel, dh=dh),
        grid=(1,),
        in_specs=[col(1), col(2), col(5), col(6), full2, full2,
                  pl.BlockSpec((RET_HEADS, CHUNK, LANES), lambda i: (0, 0, 0))],
        out_specs=[pl.BlockSpec((RET_HEADS, dh, dh), lambda i: (0, 0, 0)),
                   pl.BlockSpec((SUBLANES, r), lambda i: (0, 0))],
        out_shape=[jax.ShapeDtypeStruct((RET_HEADS, dh, dh), F32),
                   jax.ShapeDtypeStruct((SUBLANES, r), F32)],
        compiler_params=_params(("arbitrary",)),
        name="meta_state",
    )(pm_pad, pm_pad, pm_pad, pm_pad, cosm, sinm, kdec)


def _mixer_kernel(cdec_ref, q_ref, k_ref, v_ref, g_ref, cb_ref, cc_ref, ch_ref,
                  cos_ref, sin_ref, dmask_ref, qdec_ref, kdec_ref, cw_ref, s0_ref, u0_ref,
                  mix_ref, state_ref, uext_ref, *, dh, r):
    @pl.when(pl.program_id(1) == 0)
    def _():
        state_ref[...] = s0_ref[...]
        uext_ref[0:SUBLANES, :] = u0_ref[...]

    cos = cos_ref[...]
    sin = sin_ref[...]
    scale = dh ** -0.5
    for h in range(RET_HEADS):
        q1, q2 = _rotary_halves(q_ref, h, dh, cos, sin)
        k1, k2 = _rotary_halves(k_ref, h, dh, cos, sin)
        k1 = k1 * scale
        k2 = k2 * scale
        qb = jnp.concatenate([q1, q2], axis=1).astype(BF16)
        kb = jnp.concatenate([k1, k2], axis=1).astype(BF16)
        v_bf = v_ref[:, h * dh:(h + 1) * dh].astype(BF16)
        scores = lax.dot_general(qb, kb, (((1,), (1,)), ((), ())), preferred_element_type=F32)
        sm = (scores * dmask_ref[h]).astype(BF16)
        inner = jnp.dot(sm, v_bf, preferred_element_type=F32)
        qdec = qdec_ref[h]
        qd = jnp.concatenate([q1 * qdec, q2 * qdec], axis=1).astype(BF16)
        st = state_ref[h]
        cross = jnp.dot(qd, st.astype(BF16), preferred_element_type=F32)
        state_ref[h] = st * cdec_ref[h] + _state_update(k1, k2, kdec_ref[h], v_bf)
        y = inner + cross
        mu = jnp.mean(y, axis=-1, keepdims=True)
        yc = y - mu
        var = jnp.mean(yc * yc, axis=-1, keepdims=True)
        yn = yc * lax.rsqrt(var + GN_EPS)
        gh = g_ref[:, h * dh:(h + 1) * dh]
        mix_ref[:, h * dh:(h + 1) * dh] = (gh * jax.nn.sigmoid(gh) * yn).astype(mix_ref.dtype)

    cw = 512
    for cs in range(0, r, cw):
        sl = slice(cs, cs + cw)
        u = cc_ref[:, sl] * ch_ref[:, sl]
        uext_ref[SUBLANES:SUBLANES + CHUNK, sl] = u
        u1 = uext_ref[SUBLANES - 1:SUBLANES - 1 + CHUNK, sl]
        u2 = uext_ref[SUBLANES - 2:SUBLANES - 2 + CHUNK, sl]
        conv = cw_ref[0:1, sl] * u2 + cw_ref[1:2, sl] * u1 + cw_ref[2:3, sl] * u
        mix_ref[:, r + cs:r + cs + cw] = (cb_ref[:, sl] * conv).astype(mix_ref.dtype)
        uext_ref[0:SUBLANES, sl] = uext_ref[CHUNK:CHUNK + SUBLANES, sl]


def _mixer(p, cos, sin, dmask, qdec, kdec, cdec, conv_w, s0, u0, bsz, seq, r, dh):
    nc = seq // CHUNK
    d = 2 * r
    col = lambda cb: pl.BlockSpec((CHUNK, r), lambda b, c: (b * nc + c, cb))
    tab = pl.BlockSpec((CHUNK, LANES), lambda b, c: (c, 0))
    hconst = pl.BlockSpec((RET_HEADS, CHUNK, LANES), lambda b, c: (0, 0, 0))
    return pl.pallas_call(
        functools.partial(_mixer_kernel, dh=dh, r=r),
        grid=(bsz, nc),
        in_specs=[pl.BlockSpec(memory_space=pltpu.SMEM),
                  col(0), col(1), col(2), col(3), col(4), col(5), col(6),
                  tab, tab, hconst, hconst, hconst,
                  pl.BlockSpec((CONV_K, r), lambda b, c: (0, 0)),
                  pl.BlockSpec((RET_HEADS, dh, dh), lambda b, c: (0, 0, 0)),
                  pl.BlockSpec((SUBLANES, r), lambda b, c: (0, 0))],
        out_specs=pl.BlockSpec((CHUNK, d), lambda b, c: (b * nc + c, 0)),
        out_shape=jax.ShapeDtypeStruct((bsz * seq, d), BF16),
        scratch_shapes=[pltpu.VMEM((RET_HEADS, dh, dh), F32),
                        pltpu.VMEM((CHUNK + 2 * SUBLANES, r), F32)],
        compiler_params=_params(("arbitrary", "arbitrary")),
        name="mixer",
    )(cdec, p, p, p, p, p, p, p, cos, sin, dmask, qdec, kdec, conv_w, s0, u0)


def _outproj_kernel(a_ref, w_ref, x_ref, o_ref):
    o_ref[...] = x_ref[...] + jnp.dot(a_ref[...], w_ref[...].astype(BF16), preferred_element_type=F32)


def _outproj(mix, w, x2d, tm, tn):
    n, d = mix.shape
    c = w.shape[1]
    return pl.pallas_call(
        _outproj_kernel,
        grid=(c // tn, n // tm),
        in_specs=[pl.BlockSpec((tm, d), lambda j, i: (i, 0)),
                  pl.BlockSpec((d, tn), lambda j, i: (0, j)),
                  pl.BlockSpec((tm, tn), lambda j, i: (i, j))],
        out_specs=pl.BlockSpec((tm, tn), lambda j, i: (i, j)),
        out_shape=jax.ShapeDtypeStruct((n, c), F32),
        compiler_params=_params(("arbitrary", "arbitrary")),
        name="outproj",
    )(mix, w, x2d)


def _pack_bf16_pairs(x):
    half = x.shape[1] // 2
    bits = lax.bitcast_convert_type(x.astype(BF16).astype(F32), jnp.uint32)
    return bits[:, half:] | (bits[:, :half] >> 16)


def _unpack_bf16_pairs(w):
    lo = lax.bitcast_convert_type(w << 16, F32)
    hi = lax.bitcast_convert_type(w & jnp.uint32(0xFFFF0000), F32)
    return lo.astype(BF16), hi.astype(BF16)


def _router_kernel(h_ref, g_ref, wh_ref, wl_ref, b_ref, m_ref, route_ref, cnt_ref, carry_ref, *, tr):
    @pl.when(pl.program_id(0) == 0)
    def _():
        carry_ref[...] = jnp.zeros_like(carry_ref)

    x = h_ref[...]
    ms = jnp.mean(x * x, axis=-1, keepdims=True)
    m = x * lax.rsqrt(ms + NORM_EPS) * g_ref[...]
    m_ref[...] = _pack_bf16_pairs(m)

    mh = m.astype(BF16)
    ml = (m - mh.astype(F32)).astype(BF16)
    wh = wh_ref[...]
    logits = (jnp.dot(mh, wh, preferred_element_type=F32)
              + jnp.dot(ml, wh, preferred_element_type=F32)
              + jnp.dot(mh, wl_ref[...], preferred_element_type=F32)
              + b_ref[...])

    lane = lax.broadcasted_iota(jnp.int32, (tr, ROUTE_LANES), 1)
    lane_f = lane.astype(F32)
    work = logits
    vals, onehots = [], []
    for _ in range(TOP_K):
        mx = jnp.max(work, axis=-1, keepdims=True)
        idx = jnp.min(jnp.where(work == mx, lane_f, float(ROUTE_LANES)), axis=-1, keepdims=True)
        oh = lane_f == idx
        vals.append(mx)
        onehots.append(oh)
        work = jnp.where(oh, -jnp.inf, work)

    exps = [jnp.exp(v - vals[0]) for v in vals]
    denom = exps[0] + exps[1] + exps[2] + exps[3]
    gates = [e / denom for e in exps]

    chosen = onehots[0] | onehots[1] | onehots[2] | onehots[3]
    cmat = jnp.where(chosen, 1.0, 0.0).astype(BF16)
    row = lax.broadcasted_iota(jnp.int32, (tr, tr), 0)
    colm = lax.broadcasted_iota(jnp.int32, (tr, tr), 1)
    lower = jnp.where(colm < row, 1.0, 0.0).astype(BF16)
    carry = carry_ref[0:1, :]
    rank_e = jnp.dot(lower, cmat, preferred_element_type=F32) + carry
    new_carry = carry + jnp.sum(cmat.astype(F32), axis=0, keepdims=True)
    carry_ref[...] = jnp.broadcast_to(new_carry, carry_ref.shape)
    cnt_ref[...] = jnp.broadcast_to(new_carry, cnt_ref.shape)

    out = jnp.zeros((tr, ROUTE_LANES), F32)
    for k in range(TOP_K):
        idx_k = jnp.sum(jnp.where(onehots[k], lane_f, 0.0), axis=-1, keepdims=True)
        rank_k = jnp.sum(jnp.where(onehots[k], rank_e, 0.0), axis=-1, keepdims=True)
        out = jnp.where(lane == k, gates[k], out)
        out = jnp.where(lane == TOP_K + k, idx_k, out)
        out = jnp.where(lane == 2 * TOP_K + k, rank_k, out)
    route_ref[...] = out


def _router(h2, g, wh, wl, b, tr):
    n, d = h2.shape
    return pl.pallas_call(
        functools.partial(_router_kernel, tr=tr),
        grid=(n // tr,),
        in_specs=[pl.BlockSpec((tr, d), lambda i: (i, 0)),
                  pl.BlockSpec((1, d), lambda i: (0, 0)),
                  pl.BlockSpec((d, ROUTE_LANES), lambda i: (0, 0)),
                  pl.BlockSpec((d, ROUTE_LANES), lambda i: (0, 0)),
                  pl.BlockSpec((1, ROUTE_LANES), lambda i: (0, 0))],
        out_specs=[pl.BlockSpec((tr, d // 2), lambda i: (i, 0)),
                   pl.BlockSpec((tr, ROUTE_LANES), lambda i: (i, 0)),
                   pl.BlockSpec((SUBLANES, ROUTE_LANES), lambda i: (0, 0))],
        out_shape=[jax.ShapeDtypeStruct((n, d // 2), jnp.uint32),
                   jax.ShapeDtypeStruct((n, ROUTE_LANES), F32),
                   jax.ShapeDtypeStruct((SUBLANES, ROUTE_LANES), F32)],
        scratch_shapes=[pltpu.VMEM((SUBLANES, ROUTE_LANES), F32)],
        compiler_params=_params(("arbitrary",)),
        name="router",
    )(h2, g, wh, wl, b)


def _row_copy(src_ref, src_row, dst_ref, dst_row, sem):
    return pltpu.make_async_copy(src_ref.at[pl.ds(src_row, 1), :], dst_ref.at[pl.ds(dst_row, 1), :], sem)


def _dispatch_kernel(dest_ref, zlo_ref, zhi_ref, m_ref, xg_ref, zero_ref, sem, zsem, *, tt):
    @pl.when(pl.program_id(0) == 0)
    def _():
        zero_ref[...] = jnp.zeros_like(zero_ref)

        def range_body(e, carry):
            def start_body(rr, c):
                _row_copy(zero_ref, 0, xg_ref, rr, zsem).start()
                return c

            def wait_body(rr, c):
                _row_copy(zero_ref, 0, xg_ref, rr, zsem).wait()
                return c

            lax.fori_loop(zlo_ref[e], zhi_ref[e], start_body, 0)
            lax.fori_loop(zlo_ref[e], zhi_ref[e], wait_body, 0)
            return carry

        lax.fori_loop(0, N_EXPERTS + 1, range_body, 0)

    def start_body(t, c):
        for k in range(TOP_K):
            _row_copy(m_ref, t, xg_ref, dest_ref[t * TOP_K + k], sem).start()
        return c

    def wait_body(t, c):
        for k in range(TOP_K):
            _row_copy(m_ref, t, xg_ref, dest_ref[t * TOP_K + k], sem).wait()
        return c

    lax.fori_loop(0, tt, start_body, 0)
    lax.fori_loop(0, tt, wait_body, 0)


def _dispatch(dest_flat, zlo, zhi, m, rows, tt):
    n, d = m.shape
    return pl.pallas_call(
        functools.partial(_dispatch_kernel, tt=tt),
        grid=(n // tt,),
        in_specs=[pl.BlockSpec((tt * TOP_K,), lambda i: (i,), memory_space=pltpu.SMEM),
                  pl.BlockSpec(memory_space=pltpu.SMEM),
                  pl.BlockSpec(memory_space=pltpu.SMEM),
                  pl.BlockSpec((tt, d), lambda i: (i, 0))],
        out_specs=pl.BlockSpec(memory_space=pl.ANY),
        out_shape=jax.ShapeDtypeStruct((rows, d), m.dtype),
        scratch_shapes=[pltpu.VMEM((SUBLANES, d), m.dtype),
                        pltpu.SemaphoreType.DMA(()),
                        pltpu.SemaphoreType.DMA(())],
        compiler_params=_params(("arbitrary",)),
        name="dispatch",
    )(dest_flat, zlo, zhi, m)


def _for_row_blocks(nsub, block_fn):
    n8 = nsub // 8
    rem8 = nsub - n8 * 8

    def body(i, c):
        start = pl.multiple_of(i * (8 * SUB_ROWS), 8 * SUB_ROWS)
        block_fn(start, 4 * SUB_ROWS)
        block_fn(start + 4 * SUB_ROWS, 4 * SUB_ROWS)
        return c

    lax.fori_loop(0, n8, body, 0)
    base8 = n8 * (8 * SUB_ROWS)

    @pl.when(rem8 >= 4)
    def _():
        block_fn(pl.multiple_of(base8, 4 * SUB_ROWS), 4 * SUB_ROWS)

    rem = rem8 % 4
    base = base8 + (rem8 // 4) * (4 * SUB_ROWS)

    @pl.when(rem >= 2)
    def _():
        block_fn(pl.multiple_of(base, SUB_ROWS), 2 * SUB_ROWS)

    @pl.when(rem % 2 == 1)
    def _():
        block_fn(pl.multiple_of(base + (rem // 2) * (2 * SUB_ROWS), SUB_ROWS), SUB_ROWS)


def _expert_kernel(gexp_ref, gsub_ref, gns_ref, tail_ref,
                   xg_ref, wg_ref, wu_ref, bg_ref, bu_ref, wd_ref, bd_ref, yg_ref,
                   xb_ref, hid_ref, stage_ref, ybuf_ref, pend_ref,
                   xsem, ysem, *, nj, nn):
    g = pl.program_id(0)
    s = pl.program_id(1)
    tf = GATE_UP_COLS
    tn = DOWN_COLS
    ns = gns_ref[g]
    row0 = gsub_ref[g] * SUB_ROWS

    def aligned(v, m):
        return v if isinstance(v, int) else pl.multiple_of(v, m)

    def y_copy(sub, slot, col, first_row):
        return pltpu.make_async_copy(
            ybuf_ref.at[slot, pl.ds(aligned(sub * SUB_ROWS, SUB_ROWS), SUB_ROWS), :],
            yg_ref.at[pl.ds(aligned(first_row + sub * SUB_ROWS, SUB_ROWS), SUB_ROWS),
                      pl.ds(aligned(col * tn, tn), tn)],
            ysem.at[slot])

    def x_copy(first_row, sub, slot):
        return pltpu.make_async_copy(
            xg_ref.at[pl.ds(aligned(first_row + sub * SUB_ROWS, SUB_ROWS), SUB_ROWS), :],
            stage_ref.at[slot], xsem.at[slot])

    def stage_to_rows(buf, sub, slot):
        rows = pl.ds(aligned(sub * SUB_ROWS, SUB_ROWS), SUB_ROWS)
        lo, hi = _unpack_bf16_pairs(stage_ref[slot])
        half = lo.shape[1]
        xb_ref[buf, rows, 0:half] = lo
        xb_ref[buf, rows, half:2 * half] = hi

    def wait_pending(slot):
        def body(i, c):
            y_copy(0, slot, 0, 0).wait()
            return c

        lax.fori_loop(0, pend_ref[slot], body, 0)
        pend_ref[slot] = 0

    @pl.when((g == 0) & (s == 0))
    def _():
        pend_ref[0] = 0
        pend_ref[1] = 0
        ybuf_ref[0, 0:SUB_ROWS, :] = jnp.zeros((SUB_ROWS, tn), F32)
        lo = tail_ref[0]
        hi = tail_ref[1]

        def start_body(i, c):
            for col in range(nn):
                y_copy(0, 0, col, i * SUB_ROWS).start()
            return c

        def wait_body(i, c):
            for col in range(nn):
                y_copy(0, 0, col, i * SUB_ROWS).wait()
            return c

        lax.fori_loop(lo, hi, start_body, 0)
        lax.fori_loop(lo, hi, wait_body, 0)

        x_copy(row0, 0, 0).start()

        def first_body(sub, c):
            slot = lax.rem(sub, 2)

            @pl.when(sub + 1 < ns)
            def _():
                x_copy(row0, sub + 1, 1 - slot).start()

            x_copy(row0, sub, slot).wait()
            stage_to_rows(0, sub, slot)
            return c

        lax.fori_loop(0, ns, first_body, 0)

    @pl.when(ns > 0)
    def _():
        @pl.when(s < nj)
        def _():
            bg = bg_ref[gexp_ref[g], pl.ds(s, 1), :]
            bu = bu_ref[gexp_ref[g], pl.ds(s, 1), :]
            cur = lax.rem(g, 2)

            last_g = pl.num_programs(0) - 1
            g_next = jnp.minimum(g + 1, last_g)
            ns_next = jnp.where(g < last_g, gns_ref[g_next], 0)
            row0_next = gsub_ref[g_next] * SUB_ROWS
            for k in range(2):
                @pl.when(2 * s + k < ns_next)
                def _():
                    x_copy(row0_next, 2 * s + k, k).start()

            def block(start, rows_n):
                rows = pl.ds(start, rows_n)
                x = xb_ref[cur, rows, :]
                gt = jnp.dot(x, wg_ref[0].astype(BF16), preferred_element_type=F32) + bg
                up = jnp.dot(x, wu_ref[0].astype(BF16), preferred_element_type=F32) + bu
                gt = jnp.minimum(gt, SWIGLU_LIMIT)
                up = jnp.clip(up, -SWIGLU_LIMIT, SWIGLU_LIMIT)
                hid = (up + 1.0) * gt * jax.nn.sigmoid(SWIGLU_ALPHA * gt)
                hid_ref[s, rows, :] = hid.astype(hid_ref.dtype)

            _for_row_blocks(ns, block)

            for k in range(2):
                @pl.when(2 * s + k < ns_next)
                def _():
                    x_copy(row0_next, 2 * s + k, k).wait()
                    stage_to_rows(1 - cur, 2 * s + k, k)

        @pl.when(s >= nj)
        def _():
            col = s - nj
            slot = lax.rem(col, 2)
            bd = bd_ref[gexp_ref[g], pl.ds(col, 1), :]
            wait_pending(slot)

            def block(start, rows_n):
                rows = pl.ds(start, rows_n)
                acc = jnp.dot(hid_ref[0, rows, :], wd_ref[0, 0:tf, :].astype(BF16),
                              preferred_element_type=F32)
                for c in range(1, nj):
                    acc = acc + jnp.dot(hid_ref[c, rows, :], wd_ref[0, c * tf:(c + 1) * tf, :].astype(BF16),
                                        preferred_element_type=F32)
                ybuf_ref[slot, rows, :] = acc + bd

            _for_row_blocks(ns, block)

            def start_body(sub, c):
                y_copy(sub, slot, col, row0).start()
                return c

            lax.fori_loop(0, ns, start_body, 0)
            pend_ref[slot] = ns

    @pl.when((g == pl.num_programs(0) - 1) & (s == nj + nn - 1))
    def _():
        wait_pending(0)
        wait_pending(1)


def _experts(gexp, gsub, gns, tail, xg, w_gate, w_up, b_gate, b_up, w_down, b_down, n_groups):
    rows = xg.shape[0]
    d = w_gate.shape[1]
    f = w_gate.shape[2]
    tf, tn = GATE_UP_COLS, DOWN_COLS
    nj, nn = f // tf, d // tn
    assert GROUP_SUBS <= 2 * nj, "next group's rows are fetched two sub-blocks per gate/up step"

    def up_idx(g, s, ge, gs, gn, tl):
        act = jnp.minimum(gn[g], 1)
        return (ge[g], 0, jnp.minimum(s, nj - 1) * act + (nj - 1) * (1 - act))

    def down_idx(g, s, ge, gs, gn, tl):
        act = jnp.minimum(gn[g], 1)
        return (ge[g], 0, jnp.maximum(s - nj, 0) * act + (nn - 1) * (1 - act))

    def whole(g, s, ge, gs, gn, tl):
        return (0, 0, 0)

    grid_spec = pltpu.PrefetchScalarGridSpec(
        num_scalar_prefetch=4,
        grid=(n_groups, nj + nn),
        in_specs=[
            pl.BlockSpec(memory_space=pl.ANY),
            pl.BlockSpec((1, d, tf), up_idx),
            pl.BlockSpec((1, d, tf), up_idx),
            pl.BlockSpec((N_EXPERTS, nj, tf), whole),
            pl.BlockSpec((N_EXPERTS, nj, tf), whole),
            pl.BlockSpec((1, f, tn), down_idx),
            pl.BlockSpec((N_EXPERTS, nn, tn), whole),
        ],
        out_specs=pl.BlockSpec(memory_space=pl.ANY),
        scratch_shapes=[pltpu.VMEM((2, GROUP_ROWS, d), BF16),
                        pltpu.VMEM((nj, GROUP_ROWS, tf), BF16),
                        pltpu.VMEM((2, SUB_ROWS, d // 2), xg.dtype),
                        pltpu.VMEM((2, GROUP_ROWS, tn), F32),
                        pltpu.SMEM((2,), jnp.int32),
                        pltpu.SemaphoreType.DMA((2,)),
                        pltpu.SemaphoreType.DMA((2,))],
    )
    return pl.pallas_call(
        functools.partial(_expert_kernel, nj=nj, nn=nn),
        grid_spec=grid_spec,
        out_shape=jax.ShapeDtypeStruct((rows, d), F32),
        compiler_params=_params(("arbitrary", "arbitrary")),
        name="experts",
    )(gexp, gsub, gns, tail, xg, w_gate, w_up, b_gate.reshape(N_EXPERTS, nj, tf),
      b_up.reshape(N_EXPERTS, nj, tf), w_down, b_down.reshape(N_EXPERTS, nn, tn))


def _combine_kernel(dcur_ref, dnext_ref, route_ref, h_ref, g_ref, yg_ref, o_ref, buf_ref, sem, *, tt):
    i = pl.program_id(0)
    n_steps = pl.num_programs(0)
    slot = lax.rem(i, 2)

    def start_tile(dref, sl):
        def body(t, c):
            for k in range(TOP_K):
                pltpu.make_async_copy(yg_ref.at[pl.ds(dref[t * TOP_K + k], 1), :],
                                      buf_ref.at[sl, k, pl.ds(t, 1), :], sem.at[sl]).start()
            return c

        lax.fori_loop(0, tt, body, 0)

    @pl.when(i == 0)
    def _():
        start_tile(dcur_ref, 0)

    @pl.when(i + 1 < n_steps)
    def _():
        start_tile(dnext_ref, 1 - slot)

    def wait_body(t, c):
        for k in range(TOP_K):
            pltpu.make_async_copy(yg_ref.at[pl.ds(0, 1), :],
                                  buf_ref.at[slot, k, pl.ds(t, 1), :], sem.at[slot]).wait()
        return c

    lax.fori_loop(0, tt, wait_body, 0)

    d = o_ref.shape[1]
    cw = 2 * LANES
    gates = [route_ref[:, k:k + 1] for k in range(TOP_K)]
    ssq = jnp.zeros((tt, 1), F32)
    for cs in range(0, d, cw):
        acc = h_ref[:, cs:cs + cw]
        for k in range(TOP_K):
            acc = acc + buf_ref[slot, k, :, cs:cs + cw] * gates[k]
        o_ref[:, cs:cs + cw] = acc
        ssq = ssq + jnp.sum(acc * acc, axis=-1, keepdims=True)
    scale = lax.rsqrt(ssq * (1.0 / d) + NORM_EPS)
    for cs in range(0, d, cw):
        o_ref[:, cs:cs + cw] = o_ref[:, cs:cs + cw] * scale * g_ref[:, cs:cs + cw]


def _combine(dest_flat, route, h2, g, yg, tt):
    n, d = h2.shape
    steps = n // tt
    return pl.pallas_call(
        functools.partial(_combine_kernel, tt=tt),
        grid=(steps,),
        in_specs=[pl.BlockSpec((tt * TOP_K,), lambda i: (i,), memory_space=pltpu.SMEM),
                  pl.BlockSpec((tt * TOP_K,), lambda i: (jnp.minimum(i + 1, steps - 1),),
                               memory_space=pltpu.SMEM),
                  pl.BlockSpec((tt, ROUTE_LANES), lambda i: (i, 0)),
                  pl.BlockSpec((tt, d), lambda i: (i, 0)),
                  pl.BlockSpec((1, d), lambda i: (0, 0)),
                  pl.BlockSpec(memory_space=pl.ANY)],
        out_specs=pl.BlockSpec((tt, d), lambda i: (i, 0)),
        out_shape=jax.ShapeDtypeStruct((n, d), F32),
        scratch_shapes=[pltpu.VMEM((2, TOP_K, tt, d), F32),
                        pltpu.SemaphoreType.DMA((2,))],
        compiler_params=_params(("arbitrary",)),
        name="combine",
    )(dest_flat, dest_flat, route, h2, g, yg)


def _retention_tables(dh):
    lg = jnp.log1p(-jnp.exp2(-5.0 - jnp.arange(RET_HEADS, dtype=F32)))
    i = jnp.arange(CHUNK, dtype=F32)
    rel = i[:, None] - i[None, :]
    causal = rel >= 0
    dmask = jnp.where(causal[None], jnp.exp(jnp.where(causal, rel, 0.0)[None] * lg[:, None, None]), 0.0)
    q_dec = jnp.exp((i + 1.0)[None, :] * lg[:, None])
    k_dec = jnp.exp((CHUNK - 1.0 - i)[None, :] * lg[:, None])
    c_dec = jnp.exp(CHUNK * lg)
    bcast = lambda t: jnp.broadcast_to(t[:, :, None], (RET_HEADS, CHUNK, LANES))
    return dmask, bcast(q_dec), bcast(k_dec), c_dec


def _rotary_tables(length, dh):
    half = dh // 2
    inv = ROPE_BASE ** (-jnp.arange(half, dtype=F32) / half)
    ang = jnp.arange(length, dtype=jnp.int32).astype(F32)[:, None] * inv[None, :]
    return jnp.cos(ang), jnp.sin(ang)


def kernel(x, meta_tokens, norm_mix, w_in, conv_w, w_out, norm_ffn, router_w, router_b,
           w_gate, b_gate, w_up, b_up, w_down, b_down, norm_final):
    bsz, seq, d = x.shape
    assert w_in.shape[0] == 1, "single-layer stack expected"
    assert meta_tokens.shape[0] == N_META and seq % CHUNK == 0
    r = d // 2
    dh = r // RET_HEADS
    assert dh // 2 == LANES
    n = bsz * seq
    x2d = x.reshape(n, d)

    a = _prenorm(x2d, norm_mix[0][None, :], 256)
    a_meta = _prenorm(meta_tokens.astype(x.dtype), norm_mix[0][None, :], N_META)
    p, pm = _inproj(a, a_meta, w_in[0], 512, 1024)

    cos, sin = _rotary_tables(N_META + seq, dh)
    pad_meta = lambda t: jnp.pad(t, ((CHUNK - N_META, 0), (0, 0)))
    dmask, qdec, kdec, cdec = _retention_tables(dh)
    s0, u0 = _meta_state(pad_meta(pm), pad_meta(cos[:N_META]), pad_meta(sin[:N_META]), kdec, r, dh)
    mix = _mixer(p, cos[N_META:], sin[N_META:], dmask, qdec, kdec, cdec, conv_w[0], s0, u0,
                 bsz, seq, r, dh)
    h2 = _outproj(mix, w_out[0], x2d, 512, 1024)

    rw = jnp.pad(router_w[0], ((0, 0), (0, ROUTE_LANES - N_EXPERTS)))
    rwh = rw.astype(BF16)
    rwl = (rw - rwh.astype(F32)).astype(BF16)
    rb = jnp.pad(router_b[0].astype(F32), (0, ROUTE_LANES - N_EXPERTS), constant_values=-1e30)[None, :]
    m, route, cnt = _router(h2, norm_ffn[0][None, :], rwh, rwl, rb, 256)

    ids = route[:, TOP_K:2 * TOP_K].astype(jnp.int32)
    rank = route[:, 2 * TOP_K:3 * TOP_K].astype(jnp.int32)
    counts = cnt[0, :N_EXPERTS].astype(jnp.int32)

    total_sub = (n * TOP_K + N_EXPERTS * (SUB_ROWS - 1) + SUB_ROWS - 1) // SUB_ROWS
    psub = (counts + SUB_ROWS - 1) // SUB_ROWS
    pend = jnp.cumsum(psub) * SUB_ROWS
    pstart = pend - psub * SUB_ROWS
    dest_flat = (pstart[ids] + rank).reshape(-1).astype(jnp.int32)
    zlo = jnp.concatenate([pstart + counts, pend[-1:]]).astype(jnp.int32)
    zhi = jnp.concatenate([pend, jnp.full((1,), total_sub * SUB_ROWS)]).astype(jnp.int32)

    n_groups_max = N_EXPERTS + total_sub // GROUP_SUBS
    ngrp = (psub + GROUP_SUBS - 1) // GROUP_SUBS
    gend = jnp.cumsum(ngrp)
    gstart = gend - ngrp
    n_groups = gend[-1]
    gidx = jnp.arange(n_groups_max, dtype=jnp.int32)
    glast = jnp.minimum(gidx, n_groups - 1)
    gexp = jnp.minimum(jnp.searchsorted(gend, glast, side="right"), N_EXPERTS - 1).astype(jnp.int32)
    gq = glast - gstart[gexp]
    gsub = (pstart[gexp] // SUB_ROWS + gq * GROUP_SUBS).astype(jnp.int32)
    gns = jnp.where(gidx < n_groups, jnp.minimum(psub[gexp] - gq * GROUP_SUBS, GROUP_SUBS), 0).astype(jnp.int32)
    tail = jnp.stack([pend[-1] // SUB_ROWS, jnp.asarray(total_sub, pend.dtype)]).astype(jnp.int32)

    xg = _dispatch(dest_flat, zlo, zhi, m, total_sub * SUB_ROWS, 256)
    yg = _experts(gexp, gsub, gns, tail, xg, w_gate[0], w_up[0], b_gate[0], b_up[0],
                  w_down[0], b_down[0], n_groups.astype(jnp.int32))
    out = _combine(dest_flat, route, h2, norm_final[None, :], yg, 128)
    return out.reshape(bsz, seq, d)
```

```python
import functools

import jax
import jax.numpy as jnp
from jax import lax
from jax.experimental import pallas as pl
from jax.experimental.pallas import tpu as pltpu

F32 = jnp.float32
BF16 = jnp.bfloat16

N_META = 16
RET_HEADS = 8
CHUNK = 128
CONV_K = 3
ROPE_BASE = 10000.0
N_EXPERTS = 32
TOP_K = 4
SWIGLU_LIMIT = 7.0
SWIGLU_ALPHA = 1.702
NORM_EPS = 1e-5
GN_EPS = 1e-6

LANES = 128
SUBLANES = 8
VMEM_LIMIT = 58 * 1024 * 1024

SUB_ROWS = 128
GROUP_SUBS = 9
GROUP_ROWS = GROUP_SUBS * SUB_ROWS
GATE_UP_COLS = 256
DOWN_COLS = 512
ROUTE_LANES = 128


def _params(sem, vmem=VMEM_LIMIT):
    return pltpu.CompilerParams(dimension_semantics=sem, vmem_limit_bytes=vmem)


def _prenorm_kernel(x_ref, g_ref, o_ref):
    x = x_ref[...]
    ms = jnp.mean(x * x, axis=-1, keepdims=True)
    o_ref[...] = (x * lax.rsqrt(ms + NORM_EPS) * g_ref[...]).astype(o_ref.dtype)


def _prenorm(x2d, g, tm):
    n, d = x2d.shape
    return pl.pallas_call(
        _prenorm_kernel,
        grid=(n // tm,),
        in_specs=[pl.BlockSpec((tm, d), lambda i: (i, 0)),
                  pl.BlockSpec((1, d), lambda i: (0, 0))],
        out_specs=pl.BlockSpec((tm, d), lambda i: (i, 0)),
        out_shape=jax.ShapeDtypeStruct((n, d), BF16),
        compiler_params=_params(("arbitrary",)),
        name="prenorm",
    )(x2d, g)


def _inproj_kernel(a_ref, am_ref, w_ref, o_ref, om_ref):
    @pl.when(pl.program_id(1) == 0)
    def _():
        om_ref[...] = jnp.dot(am_ref[...], w_ref[...].astype(BF16), preferred_element_type=F32)

    o_ref[...] = jnp.dot(a_ref[...], w_ref[...].astype(BF16), preferred_element_type=F32)


def _inproj(a, a_meta, w, tm, tn):
    n, d = a.shape
    c = w.shape[1]
    nm = a_meta.shape[0]
    return pl.pallas_call(
        _inproj_kernel,
        grid=(c // tn, n // tm),
        in_specs=[pl.BlockSpec((tm, d), lambda j, i: (i, 0)),
                  pl.BlockSpec((nm, d), lambda j, i: (0, 0)),
                  pl.BlockSpec((d, tn), lambda j, i: (0, j))],
        out_specs=[pl.BlockSpec((tm, tn), lambda j, i: (i, j)),
                   pl.BlockSpec((nm, tn), lambda j, i: (0, j))],
        out_shape=[jax.ShapeDtypeStruct((n, c), F32),
                   jax.ShapeDtypeStruct((nm, c), F32)],
        compiler_params=_params(("arbitrary", "arbitrary")),
        name="inproj",
    )(a, a_meta, w)


def _rotary_halves(ref, h, dh, cos, sin):
    half = dh // 2
    t1 = ref[:, h * dh:h * dh + half]
    t2 = ref[:, h * dh + half:(h + 1) * dh]
    return t1 * cos - t2 * sin, t2 * cos + t1 * sin


def _state_update(k1, k2, kdec, v_bf):
    kd = jnp.concatenate([k1 * kdec, k2 * kdec], axis=1).astype(BF16)
    return lax.dot_general(kd, v_bf, (((0,), (0,)), ((), ())), preferred_element_type=F32)


def _meta_kernel(k_ref, v_ref, cc_ref, ch_ref, cos_ref, sin_ref, kdec_ref, s0_ref, u0_ref, *, dh):
    cos = cos_ref[...]
    sin = sin_ref[...]
    scale = dh ** -0.5
    for h in range(RET_HEADS):
        k1, k2 = _rotary_halves(k_ref, h, dh, cos, sin)
        v_bf = v_ref[:, h * dh:(h + 1) * dh].astype(BF16)
        s0_ref[h] = _state_update(k1 * scale, k2 * scale, kdec_ref[h], v_bf)
    u0_ref[...] = cc_ref[CHUNK - SUBLANES:CHUNK, :] * ch_ref[CHUNK - SUBLANES:CHUNK, :]


def _meta_state(pm_pad, cosm, sinm, kdec, r, dh):
    col = lambda cb: pl.BlockSpec((CHUNK, r), lambda i: (0, cb))
    full2 = pl.BlockSpec((CHUNK, LANES), lambda i: (0, 0))
    return pl.pallas_call(
        functools.partial(_meta_kernel, dh=dh),
        grid=(1,),
        in_specs=[col(1), col(2), col(5), col(6), full2, full2,
                  pl.BlockSpec((RET_HEADS, CHUNK, LANES), lambda i: (0, 0, 0))],
        out_specs=[pl.BlockSpec((RET_HEADS, dh, dh), lambda i: (0, 0, 0)),
                   pl.BlockSpec((SUBLANES, r), lambda i: (0, 0))],
        out_shape=[jax.ShapeDtypeStruct((RET_HEADS, dh, dh), F32),
                   jax.ShapeDtypeStruct((SUBLANES, r), F32)],
        compiler_params=_params(("arbitrary",)),
        name="meta_state",
    )(pm_pad, pm_pad, pm_pad, pm_pad, cosm, sinm, kdec)


def _mixer_kernel(cdec_ref, q_ref, k_ref, v_ref, g_ref, cb_ref, cc_ref, ch_ref,
                  cos_ref, sin_ref, dmask_ref, qdec_ref, kdec_ref, cw_ref, s0_ref, u0_ref,
                  mix_ref, state_ref, uext_ref, *, dh, r):
    @pl.when(pl.program_id(1) == 0)
    def _():
        state_ref[...] = s0_ref[...]
        uext_ref[0:SUBLANES, :] = u0_ref[...]

    cos = cos_ref[...]
    sin = sin_ref[...]
    scale = dh ** -0.5
    for h in range(RET_HEADS):
        q1, q2 = _rotary_halves(q_ref, h, dh, cos, sin)
        k1, k2 = _rotary_halves(k_ref, h, dh, cos, sin)
        k1 = k1 * scale
        k2 = k2 * scale
        qb = jnp.concatenate([q1, q2], axis=1).astype(BF16)
        kb = jnp.concatenate([k1, k2], axis=1).astype(BF16)
        v_bf = v_ref[:, h * dh:(h + 1) * dh].astype(BF16)
        scores = lax.dot_general(qb, kb, (((1,), (1,)), ((), ())), preferred_element_type=F32)
        sm = (scores * dmask_ref[h]).astype(BF16)
        inner = jnp.dot(sm, v_bf, preferred_element_type=F32)
        qdec = qdec_ref[h]
        qd = jnp.concatenate([q1 * qdec, q2 * qdec], axis=1).astype(BF16)
        st = state_ref[h]
        cross = jnp.dot(qd, st.astype(BF16), preferred_element_type=F32)
        state_ref[h] = st * cdec_ref[h] + _state_update(k1, k2, kdec_ref[h], v_bf)
        y = inner + cross
        mu = jnp.mean(y, axis=-1, keepdims=True)
        yc = y - mu
        var = jnp.mean(yc * yc, axis=-1, keepdims=True)
        yn = yc * lax.rsqrt(var + GN_EPS)
        gh = g_ref[:, h * dh:(h + 1) * dh]
        mix_ref[:, h * dh:(h + 1) * dh] = (gh * jax.nn.sigmoid(gh) * yn).astype(mix_ref.dtype)

    cw = 512
    for cs in range(0, r, cw):
        sl = slice(cs, cs + cw)
        u = cc_ref[:, sl] * ch_ref[:, sl]
        uext_ref[SUBLANES:SUBLANES + CHUNK, sl] = u
        u1 = uext_ref[SUBLANES - 1:SUBLANES - 1 + CHUNK, sl]
        u2 = uext_ref[SUBLANES - 2:SUBLANES - 2 + CHUNK, sl]
        conv = cw_ref[0:1, sl] * u2 + cw_ref[1:2, sl] * u1 + cw_ref[2:3, sl] * u
        mix_ref[:, r + cs:r + cs + cw] = (cb_ref[:, sl] * conv).astype(mix_ref.dtype)
        uext_ref[0:SUBLANES, sl] = uext_ref[CHUNK:CHUNK + SUBLANES, sl]


def _mixer(p, cos, sin, dmask, qdec, kdec, cdec, conv_w, s0, u0, bsz, seq, r, dh):
    nc = seq // CHUNK
    d = 2 * r
    col = lambda cb: pl.BlockSpec((CHUNK, r), lambda b, c: (b * nc + c, cb))
    tab = pl.BlockSpec((CHUNK, LANES), lambda b, c: (c, 0))
    hconst = pl.BlockSpec((RET_HEADS, CHUNK, LANES), lambda b, c: (0, 0, 0))
    return pl.pallas_call(
        functools.partial(_mixer_kernel, dh=dh, r=r),
        grid=(bsz, nc),
        in_specs=[pl.BlockSpec(memory_space=pltpu.SMEM),
                  col(0), col(1), col(2), col(3), col(4), col(5), col(6),
                  tab, tab, hconst, hconst, hconst,
                  pl.BlockSpec((CONV_K, r), lambda b, c: (0, 0)),
                  pl.BlockSpec((RET_HEADS, dh, dh), lambda b, c: (0, 0, 0)),
                  pl.BlockSpec((SUBLANES, r), lambda b, c: (0, 0))],
        out_specs=pl.BlockSpec((CHUNK, d), lambda b, c: (b * nc + c, 0)),
        out_shape=jax.ShapeDtypeStruct((bsz * seq, d), BF16),
        scratch_shapes=[pltpu.VMEM((RET_HEADS, dh, dh), F32),
                        pltpu.VMEM((CHUNK + 2 * SUBLANES, r), F32)],
        compiler_params=_params(("arbitrary", "arbitrary")),
        name="mixer",
    )(cdec, p, p, p, p, p, p, p, cos, sin, dmask, qdec, kdec, conv_w, s0, u0)


def _outproj_kernel(a_ref, w_ref, x_ref, o_ref):
    o_ref[...] = x_ref[...] + jnp.dot(a_ref[...], w_ref[...].astype(BF16), preferred_element_type=F32)


def _outproj(mix, w, x2d, tm, tn):
    n, d = mix.shape
    c = w.shape[1]
    return pl.pallas_call(
        _outproj_kernel,
        grid=(c // tn, n // tm),
        in_specs=[pl.BlockSpec((tm, d), lambda j, i: (i, 0)),
                  pl.BlockSpec((d, tn), lambda j, i: (0, j)),
                  pl.BlockSpec((tm, tn), lambda j, i: (i, j))],
        out_specs=pl.BlockSpec((tm, tn), lambda j, i: (i, j)),
        out_shape=jax.ShapeDtypeStruct((n, c), F32),
        compiler_params=_params(("arbitrary", "arbitrary")),
        name="outproj",
    )(mix, w, x2d)


def _pack_bf16_pairs(x):
    half = x.shape[1] // 2
    bits = lax.bitcast_convert_type(x.astype(BF16).astype(F32), jnp.uint32)
    return bits[:, half:] | (bits[:, :half] >> 16)


def _unpack_bf16_pairs(w):
    lo = lax.bitcast_convert_type(w << 16, F32)
    hi = lax.bitcast_convert_type(w & jnp.uint32(0xFFFF0000), F32)
    return lo.astype(BF16), hi.astype(BF16)


def _router_kernel(h_ref, g_ref, w_ref, b_ref, m_ref, route_ref, cnt_ref, carry_ref, *, tr):
    @pl.when(pl.program_id(0) == 0)
    def _():
        carry_ref[...] = jnp.zeros_like(carry_ref)

    x = h_ref[...]
    ms = jnp.mean(x * x, axis=-1, keepdims=True)
    m = x * lax.rsqrt(ms + NORM_EPS) * g_ref[...]
    m_ref[...] = _pack_bf16_pairs(m)

    logits = jnp.dot(m.astype(BF16), w_ref[...], preferred_element_type=F32) + b_ref[...]

    lane = lax.broadcasted_iota(jnp.int32, (tr, ROUTE_LANES), 1)
    lane_f = lane.astype(F32)
    work = logits
    vals, onehots = [], []
    for _ in range(TOP_K):
        mx = jnp.max(work, axis=-1, keepdims=True)
        idx = jnp.min(jnp.where(work == mx, lane_f, float(ROUTE_LANES)), axis=-1, keepdims=True)
        oh = lane_f == idx
        vals.append(mx)
        onehots.append(oh)
        work = jnp.where(oh, -jnp.inf, work)

    exps = [jnp.exp(v - vals[0]) for v in vals]
    denom = exps[0] + exps[1] + exps[2] + exps[3]
    gates = [e / denom for e in exps]

    chosen = onehots[0] | onehots[1] | onehots[2] | onehots[3]
    cmat = jnp.where(chosen, 1.0, 0.0).astype(BF16)
    row = lax.broadcasted_iota(jnp.int32, (tr, tr), 0)
    colm = lax.broadcasted_iota(jnp.int32, (tr, tr), 1)
    lower = jnp.where(colm < row, 1.0, 0.0).astype(BF16)
    carry = carry_ref[0:1, :]
    rank_e = jnp.dot(lower, cmat, preferred_element_type=F32) + carry
    new_carry = carry + jnp.sum(cmat.astype(F32), axis=0, keepdims=True)
    carry_ref[...] = jnp.broadcast_to(new_carry, carry_ref.shape)
    cnt_ref[...] = jnp.broadcast_to(new_carry, cnt_ref.shape)

    out = jnp.zeros((tr, ROUTE_LANES), F32)
    for k in range(TOP_K):
        idx_k = jnp.sum(jnp.where(onehots[k], lane_f, 0.0), axis=-1, keepdims=True)
        rank_k = jnp.sum(jnp.where(onehots[k], rank_e, 0.0), axis=-1, keepdims=True)
        out = jnp.where(lane == k, gates[k], out)
        out = jnp.where(lane == TOP_K + k, idx_k, out)
        out = jnp.where(lane == 2 * TOP_K + k, rank_k, out)
    route_ref[...] = out


def _router(h2, g, w, b, tr):
    n, d = h2.shape
    return pl.pallas_call(
        functools.partial(_router_kernel, tr=tr),
        grid=(n // tr,),
        in_specs=[pl.BlockSpec((tr, d), lambda i: (i, 0)),
                  pl.BlockSpec((1, d), lambda i: (0, 0)),
                  pl.BlockSpec((d, ROUTE_LANES), lambda i: (0, 0)),
                  pl.BlockSpec((1, ROUTE_LANES), lambda i: (0, 0))],
        out_specs=[pl.BlockSpec((tr, d // 2), lambda i: (i, 0)),
                   pl.BlockSpec((tr, ROUTE_LANES), lambda i: (i, 0)),
                   pl.BlockSpec((SUBLANES, ROUTE_LANES), lambda i: (0, 0))],
        out_shape=[jax.ShapeDtypeStruct((n, d // 2), jnp.uint32),
                   jax.ShapeDtypeStruct((n, ROUTE_LANES), F32),
                   jax.ShapeDtypeStruct((SUBLANES, ROUTE_LANES), F32)],
        scratch_shapes=[pltpu.VMEM((SUBLANES, ROUTE_LANES), F32)],
        compiler_params=_params(("arbitrary",)),
        name="router",
    )(h2, g, w, b)


def _row_copy(src_ref, src_row, dst_ref, dst_row, sem):
    return pltpu.make_async_copy(src_ref.at[pl.ds(src_row, 1), :], dst_ref.at[pl.ds(dst_row, 1), :], sem)


def _dispatch_kernel(dest_ref, zlo_ref, zhi_ref, m_ref, xg_ref, zero_ref, sem, zsem, *, tt):
    @pl.when(pl.program_id(0) == 0)
    def _():
        zero_ref[...] = jnp.zeros_like(zero_ref)

        def range_body(e, carry):
            def start_body(rr, c):
                _row_copy(zero_ref, 0, xg_ref, rr, zsem).start()
                return c

            def wait_body(rr, c):
                _row_copy(zero_ref, 0, xg_ref, rr, zsem).wait()
                return c

            lax.fori_loop(zlo_ref[e], zhi_ref[e], start_body, 0)
            lax.fori_loop(zlo_ref[e], zhi_ref[e], wait_body, 0)
            return carry

        lax.fori_loop(0, N_EXPERTS + 1, range_body, 0)

    def start_body(t, c):
        for k in range(TOP_K):
            _row_copy(m_ref, t, xg_ref, dest_ref[t * TOP_K + k], sem).start()
        return c

    def wait_body(t, c):
        for k in range(TOP_K):
            _row_copy(m_ref, t, xg_ref, dest_ref[t * TOP_K + k], sem).wait()
        return c

    lax.fori_loop(0, tt, start_body, 0)
    lax.fori_loop(0, tt, wait_body, 0)


def _dispatch(dest_flat, zlo, zhi, m, rows, tt):
    n, d = m.shape
    return pl.pallas_call(
        functools.partial(_dispatch_kernel, tt=tt),
        grid=(n // tt,),
        in_specs=[pl.BlockSpec((tt * TOP_K,), lambda i: (i,), memory_space=pltpu.SMEM),
                  pl.BlockSpec(memory_space=pltpu.SMEM),
                  pl.BlockSpec(memory_space=pltpu.SMEM),
                  pl.BlockSpec((tt, d), lambda i: (i, 0))],
        out_specs=pl.BlockSpec(memory_space=pl.ANY),
        out_shape=jax.ShapeDtypeStruct((rows, d), m.dtype),
        scratch_shapes=[pltpu.VMEM((SUBLANES, d), m.dtype),
                        pltpu.SemaphoreType.DMA(()),
                        pltpu.SemaphoreType.DMA(())],
        compiler_params=_params(("arbitrary",)),
        name="dispatch",
    )(dest_flat, zlo, zhi, m)


def _for_row_blocks(nsub, block_fn):
    n8 = nsub // 8
    rem8 = nsub - n8 * 8

    def body(i, c):
        start = pl.multiple_of(i * (8 * SUB_ROWS), 8 * SUB_ROWS)
        block_fn(start, 4 * SUB_ROWS)
        block_fn(start + 4 * SUB_ROWS, 4 * SUB_ROWS)
        return c

    lax.fori_loop(0, n8, body, 0)
    base8 = n8 * (8 * SUB_ROWS)

    @pl.when(rem8 >= 4)
    def _():
        block_fn(pl.multiple_of(base8, 4 * SUB_ROWS), 4 * SUB_ROWS)

    rem = rem8 % 4
    base = base8 + (rem8 // 4) * (4 * SUB_ROWS)

    @pl.when(rem >= 2)
    def _():
        block_fn(pl.multiple_of(base, SUB_ROWS), 2 * SUB_ROWS)

    @pl.when(rem % 2 == 1)
    def _():
        block_fn(pl.multiple_of(base + (rem // 2) * (2 * SUB_ROWS), SUB_ROWS), SUB_ROWS)


def _expert_kernel(gexp_ref, gsub_ref, gns_ref, tail_ref,
                   xg_ref, wg_ref, wu_ref, bg_ref, bu_ref, wd_ref, bd_ref, yg_ref,
                   xb_ref, hid_ref, stage_ref, ybuf_ref, pend_ref,
                   xsem, ysem, *, nj, nn):
    g = pl.program_id(0)
    s = pl.program_id(1)
    tf = GATE_UP_COLS
    tn = DOWN_COLS
    ns = gns_ref[g]
    row0 = gsub_ref[g] * SUB_ROWS

    def aligned(v, m):
        return v if isinstance(v, int) else pl.multiple_of(v, m)

    def y_copy(sub, slot, col, first_row):
        return pltpu.make_async_copy(
            ybuf_ref.at[slot, pl.ds(aligned(sub * SUB_ROWS, SUB_ROWS), SUB_ROWS), :],
            yg_ref.at[pl.ds(aligned(first_row + sub * SUB_ROWS, SUB_ROWS), SUB_ROWS),
                      pl.ds(aligned(col * tn, tn), tn)],
            ysem.at[slot])

    def x_copy(first_row, sub, slot):
        return pltpu.make_async_copy(
            xg_ref.at[pl.ds(aligned(first_row + sub * SUB_ROWS, SUB_ROWS), SUB_ROWS), :],
            stage_ref.at[slot], xsem.at[slot])

    def stage_to_rows(buf, sub, slot):
        rows = pl.ds(aligned(sub * SUB_ROWS, SUB_ROWS), SUB_ROWS)
        lo, hi = _unpack_bf16_pairs(stage_ref[slot])
        half = lo.shape[1]
        xb_ref[buf, rows, 0:half] = lo
        xb_ref[buf, rows, half:2 * half] = hi

    def wait_pending(slot):
        def body(i, c):
            y_copy(0, slot, 0, 0).wait()
            return c

        lax.fori_loop(0, pend_ref[slot], body, 0)
        pend_ref[slot] = 0

    @pl.when((g == 0) & (s == 0))
    def _():
        pend_ref[0] = 0
        pend_ref[1] = 0
        ybuf_ref[0, 0:SUB_ROWS, :] = jnp.zeros((SUB_ROWS, tn), F32)
        lo = tail_ref[0]
        hi = tail_ref[1]

        def start_body(i, c):
            for col in range(nn):
                y_copy(0, 0, col, i * SUB_ROWS).start()
            return c

        def wait_body(i, c):
            for col in range(nn):
                y_copy(0, 0, col, i * SUB_ROWS).wait()
            return c

        lax.fori_loop(lo, hi, start_body, 0)
        lax.fori_loop(lo, hi, wait_body, 0)

        x_copy(row0, 0, 0).start()

        def first_body(sub, c):
            slot = lax.rem(sub, 2)

            @pl.when(sub + 1 < ns)
            def _():
                x_copy(row0, sub + 1, 1 - slot).start()

            x_copy(row0, sub, slot).wait()
            stage_to_rows(0, sub, slot)
            return c

        lax.fori_loop(0, ns, first_body, 0)

    @pl.when(ns > 0)
    def _():
        @pl.when(s < nj)
        def _():
            bg = bg_ref[gexp_ref[g], pl.ds(s, 1), :]
            bu = bu_ref[gexp_ref[g], pl.ds(s, 1), :]
            cur = lax.rem(g, 2)

            last_g = pl.num_programs(0) - 1
            g_next = jnp.minimum(g + 1, last_g)
            ns_next = jnp.where(g < last_g, gns_ref[g_next], 0)
            row0_next = gsub_ref[g_next] * SUB_ROWS
            for k in range(2):
                @pl.when(2 * s + k < ns_next)
                def _():
                    x_copy(row0_next, 2 * s + k, k).start()

            def block(start, rows_n):
                rows = pl.ds(start, rows_n)
                x = xb_ref[cur, rows, :]
                gt = jnp.dot(x, wg_ref[0].astype(BF16), preferred_element_type=F32) + bg
                up = jnp.dot(x, wu_ref[0].astype(BF16), preferred_element_type=F32) + bu
                gt = jnp.minimum(gt, SWIGLU_LIMIT)
                up = jnp.clip(up, -SWIGLU_LIMIT, SWIGLU_LIMIT)
                hid = (up + 1.0) * gt * jax.nn.sigmoid(SWIGLU_ALPHA * gt)
                hid_ref[s, rows, :] = hid.astype(hid_ref.dtype)

            _for_row_blocks(ns, block)

            for k in range(2):
                @pl.when(2 * s + k < ns_next)
                def _():
                    x_copy(row0_next, 2 * s + k, k).wait()
                    stage_to_rows(1 - cur, 2 * s + k, k)

        @pl.when(s >= nj)
        def _():
            col = s - nj
            slot = lax.rem(col, 2)
            bd = bd_ref[gexp_ref[g], pl.ds(col, 1), :]
            wait_pending(slot)

            def block(start, rows_n):
                rows = pl.ds(start, rows_n)
                acc = jnp.dot(hid_ref[0, rows, :], wd_ref[0, 0:tf, :].astype(BF16),
                              preferred_element_type=F32)
                for c in range(1, nj):
                    acc = acc + jnp.dot(hid_ref[c, rows, :], wd_ref[0, c * tf:(c + 1) * tf, :].astype(BF16),
                                        preferred_element_type=F32)
                ybuf_ref[slot, rows, :] = acc + bd

            _for_row_blocks(ns, block)

            def start_body(sub, c):
                y_copy(sub, slot, col, row0).start()
                return c

            lax.fori_loop(0, ns, start_body, 0)
            pend_ref[slot] = ns

    @pl.when((g == pl.num_programs(0) - 1) & (s == nj + nn - 1))
    def _():
        wait_pending(0)
        wait_pending(1)


def _experts(gexp, gsub, gns, tail, xg, w_gate, w_up, b_gate, b_up, w_down, b_down, n_groups):
    rows = xg.shape[0]
    d = w_gate.shape[1]
    f = w_gate.shape[2]
    tf, tn = GATE_UP_COLS, DOWN_COLS
    nj, nn = f // tf, d // tn
    assert GROUP_SUBS <= 2 * nj, "next group's rows are fetched two sub-blocks per gate/up step"

    def up_idx(g, s, ge, gs, gn, tl):
        act = jnp.minimum(gn[g], 1)
        return (ge[g], 0, jnp.minimum(s, nj - 1) * act + (nj - 1) * (1 - act))

    def down_idx(g, s, ge, gs, gn, tl):
        act = jnp.minimum(gn[g], 1)
        in_down = jnp.where(s >= nj, act, 0)
        g_src = jnp.where(in_down == 1, g, jnp.maximum(g - act, 0))
        return (ge[g_src], 0, (s - nj) * in_down + (nn - 1) * (1 - in_down))

    def whole(g, s, ge, gs, gn, tl):
        return (0, 0, 0)

    grid_spec = pltpu.PrefetchScalarGridSpec(
        num_scalar_prefetch=4,
        grid=(n_groups, nj + nn),
        in_specs=[
            pl.BlockSpec(memory_space=pl.ANY),
            pl.BlockSpec((1, d, tf), up_idx),
            pl.BlockSpec((1, d, tf), up_idx),
            pl.BlockSpec((N_EXPERTS, nj, tf), whole),
            pl.BlockSpec((N_EXPERTS, nj, tf), whole),
            pl.BlockSpec((1, f, tn), down_idx),
            pl.BlockSpec((N_EXPERTS, nn, tn), whole),
        ],
        out_specs=pl.BlockSpec(memory_space=pl.ANY),
        scratch_shapes=[pltpu.VMEM((2, GROUP_ROWS, d), BF16),
                        pltpu.VMEM((nj, GROUP_ROWS, tf), BF16),
                        pltpu.VMEM((2, SUB_ROWS, d // 2), xg.dtype),
                        pltpu.VMEM((2, GROUP_ROWS, tn), F32),
                        pltpu.SMEM((2,), jnp.int32),
                        pltpu.SemaphoreType.DMA((2,)),
                        pltpu.SemaphoreType.DMA((2,))],
    )
    return pl.pallas_call(
        functools.partial(_expert_kernel, nj=nj, nn=nn),
        grid_spec=grid_spec,
        out_shape=jax.ShapeDtypeStruct((rows, d), F32),
        compiler_params=_params(("arbitrary", "arbitrary")),
        name="experts",
    )(gexp, gsub, gns, tail, xg, w_gate, w_up, b_gate.reshape(N_EXPERTS, nj, tf),
      b_up.reshape(N_EXPERTS, nj, tf), w_down, b_down.reshape(N_EXPERTS, nn, tn))


def _combine_kernel(dcur_ref, dnext_ref, route_ref, h_ref, g_ref, yg_ref, o_ref, buf_ref, sem, *, tt):
    i = pl.program_id(0)
    n_steps = pl.num_programs(0)
    slot = lax.rem(i, 2)

    def start_tile(dref, sl):
        def body(t, c):
            for k in range(TOP_K):
                pltpu.make_async_copy(yg_ref.at[pl.ds(dref[t * TOP_K + k], 1), :],
                                      buf_ref.at[sl, k, pl.ds(t, 1), :], sem.at[sl]).start()
            return c

        lax.fori_loop(0, tt, body, 0)

    @pl.when(i == 0)
    def _():
        start_tile(dcur_ref, 0)

    @pl.when(i + 1 < n_steps)
    def _():
        start_tile(dnext_ref, 1 - slot)

    def wait_body(t, c):
        for k in range(TOP_K):
            pltpu.make_async_copy(yg_ref.at[pl.ds(0, 1), :],
                                  buf_ref.at[slot, k, pl.ds(t, 1), :], sem.at[slot]).wait()
        return c

    lax.fori_loop(0, tt, wait_body, 0)

    d = o_ref.shape[1]
    cw = 2 * LANES
    gates = [route_ref[:, k:k + 1] for k in range(TOP_K)]
    ssq = jnp.zeros((tt, 1), F32)
    for cs in range(0, d, cw):
        acc = h_ref[:, cs:cs + cw]
        for k in range(TOP_K):
            acc = acc + buf_ref[slot, k, :, cs:cs + cw] * gates[k]
        o_ref[:, cs:cs + cw] = acc
        ssq = ssq + jnp.sum(acc * acc, axis=-1, keepdims=True)
    scale = lax.rsqrt(ssq * (1.0 / d) + NORM_EPS)
    for cs in range(0, d, cw):
        o_ref[:, cs:cs + cw] = o_ref[:, cs:cs + cw] * scale * g_ref[:, cs:cs + cw]


def _combine(dest_flat, route, h2, g, yg, tt):
    n, d = h2.shape
    steps = n // tt
    return pl.pallas_call(
        functools.partial(_combine_kernel, tt=tt),
        grid=(steps,),
        in_specs=[pl.BlockSpec((tt * TOP_K,), lambda i: (i,), memory_space=pltpu.SMEM),
                  pl.BlockSpec((tt * TOP_K,), lambda i: (jnp.minimum(i + 1, steps - 1),),
                               memory_space=pltpu.SMEM),
                  pl.BlockSpec((tt, ROUTE_LANES), lambda i: (i, 0)),
                  pl.BlockSpec((tt, d), lambda i: (i, 0)),
                  pl.BlockSpec((1, d), lambda i: (0, 0)),
                  pl.BlockSpec(memory_space=pl.ANY)],
        out_specs=pl.BlockSpec((tt, d), lambda i: (i, 0)),
        out_shape=jax.ShapeDtypeStruct((n, d), F32),
        scratch_shapes=[pltpu.VMEM((2, TOP_K, tt, d), F32),
                        pltpu.SemaphoreType.DMA((2,))],
        compiler_params=_params(("arbitrary",)),
        name="combine",
    )(dest_flat, dest_flat, route, h2, g, yg)


def _retention_tables(dh):
    lg = jnp.log1p(-jnp.exp2(-5.0 - jnp.arange(RET_HEADS, dtype=F32)))
    i = jnp.arange(CHUNK, dtype=F32)
    rel = i[:, None] - i[None, :]
    causal = rel >= 0
    dmask = jnp.where(causal[None], jnp.exp(jnp.where(causal, rel, 0.0)[None] * lg[:, None, None]), 0.0)
    q_dec = jnp.exp((i + 1.0)[None, :] * lg[:, None])
    k_dec = jnp.exp((CHUNK - 1.0 - i)[None, :] * lg[:, None])
    c_dec = jnp.exp(CHUNK * lg)
    bcast = lambda t: jnp.broadcast_to(t[:, :, None], (RET_HEADS, CHUNK, LANES))
    return dmask, bcast(q_dec), bcast(k_dec), c_dec


def _rotary_tables(length, dh):
    half = dh // 2
    inv = ROPE_BASE ** (-jnp.arange(half, dtype=F32) / half)
    ang = jnp.arange(length, dtype=jnp.int32).astype(F32)[:, None] * inv[None, :]
    return jnp.cos(ang), jnp.sin(ang)


def kernel(x, meta_tokens, norm_mix, w_in, conv_w, w_out, norm_ffn, router_w, router_b,
           w_gate, b_gate, w_up, b_up, w_down, b_down, norm_final):
    bsz, seq, d = x.shape
    assert w_in.shape[0] == 1, "single-layer stack expected"
    assert meta_tokens.shape[0] == N_META and seq % CHUNK == 0
    r = d // 2
    dh = r // RET_HEADS
    assert dh // 2 == LANES
    n = bsz * seq
    x2d = x.reshape(n, d)

    a = _prenorm(x2d, norm_mix[0][None, :], 512)
    a_meta = _prenorm(meta_tokens.astype(x.dtype), norm_mix[0][None, :], N_META)
    p, pm = _inproj(a, a_meta, w_in[0], 512, 1024)

    cos, sin = _rotary_tables(N_META + seq, dh)
    pad_meta = lambda t: jnp.pad(t, ((CHUNK - N_META, 0), (0, 0)))
    dmask, qdec, kdec, cdec = _retention_tables(dh)
    s0, u0 = _meta_state(pad_meta(pm), pad_meta(cos[:N_META]), pad_meta(sin[:N_META]), kdec, r, dh)
    mix = _mixer(p, cos[N_META:], sin[N_META:], dmask, qdec, kdec, cdec, conv_w[0], s0, u0,
                 bsz, seq, r, dh)
    h2 = _outproj(mix, w_out[0], x2d, 512, 1024)

    rw = jnp.pad(router_w[0], ((0, 0), (0, ROUTE_LANES - N_EXPERTS))).astype(BF16)
    rb = jnp.pad(router_b[0].astype(F32), (0, ROUTE_LANES - N_EXPERTS), constant_values=-1e30)[None, :]
    m, route, cnt = _router(h2, norm_ffn[0][None, :], rw, rb, 256)

    side = route[:, TOP_K:3 * TOP_K].T.astype(jnp.int32)
    ids, rank = side[:TOP_K], side[TOP_K:]
    counts = cnt[0, :N_EXPERTS].astype(jnp.int32)

    total_sub = (n * TOP_K + N_EXPERTS * (SUB_ROWS - 1) + SUB_ROWS - 1) // SUB_ROWS
    psub = (counts + SUB_ROWS - 1) // SUB_ROWS
    pend = jnp.cumsum(psub) * SUB_ROWS
    pstart = pend - psub * SUB_ROWS
    dest_flat = (jnp.take(pstart, ids) + rank).T.reshape(-1).astype(jnp.int32)
    zlo = jnp.concatenate([pstart + counts, pend[-1:]]).astype(jnp.int32)
    zhi = jnp.concatenate([pend, jnp.full((1,), total_sub * SUB_ROWS)]).astype(jnp.int32)

    n_groups_max = N_EXPERTS + total_sub // GROUP_SUBS
    ngrp = (psub + GROUP_SUBS - 1) // GROUP_SUBS
    gend = jnp.cumsum(ngrp)
    gstart = gend - ngrp
    n_groups = gend[-1]
    gidx = jnp.arange(n_groups_max, dtype=jnp.int32)
    glast = jnp.minimum(gidx, n_groups - 1)
    gexp = jnp.minimum(jnp.searchsorted(gend, glast, side="right"), N_EXPERTS - 1).astype(jnp.int32)
    gq = glast - gstart[gexp]
    gsub = (pstart[gexp] // SUB_ROWS + gq * GROUP_SUBS).astype(jnp.int32)
    gns = jnp.where(gidx < n_groups, jnp.minimum(psub[gexp] - gq * GROUP_SUBS, GROUP_SUBS), 0).astype(jnp.int32)
    tail = jnp.stack([pend[-1] // SUB_ROWS, jnp.asarray(total_sub, pend.dtype)]).astype(jnp.int32)

    xg = _dispatch(dest_flat, zlo, zhi, m, total_sub * SUB_ROWS, 256)
    yg = _experts(gexp, gsub, gns, tail, xg, w_gate[0], w_up[0], b_gate[0], b_up[0],
                  w_down[0], b_down[0], n_groups.astype(jnp.int32))
    out = _combine(dest_flat, route, h2, norm_final[None, :], yg, 128)
    return out.reshape(bsz, seq, d)
```

```python
import functools

import jax
import jax.numpy as jnp
from jax import lax
from jax.experimental import pallas as pl
from jax.experimental.pallas import tpu as pltpu

F32 = jnp.float32
BF16 = jnp.bfloat16

N_META = 16
RET_HEADS = 8
CHUNK = 128
CONV_K = 3
ROPE_BASE = 10000.0
N_EXPERTS = 32
TOP_K = 4
SWIGLU_LIMIT = 7.0
SWIGLU_ALPHA = 1.702
NORM_EPS = 1e-5
GN_EPS = 1e-6

LANES = 128
SUBLANES = 8
VMEM_LIMIT = 58 * 1024 * 1024

SUB_ROWS = 128
GROUP_SUBS = 9
GROUP_ROWS = GROUP_SUBS * SUB_ROWS
GATE_UP_COLS = 256
DOWN_COLS = 512
ROUTE_LANES = 128
NORM_ROWS = 512
PROJ_ROWS = 512
PROJ_COLS = 1024
ROUTER_ROWS = 256
DISPATCH_ROWS = 256
COMBINE_ROWS = 128


def _params(sem, vmem=VMEM_LIMIT):
    return pltpu.CompilerParams(dimension_semantics=sem, vmem_limit_bytes=vmem)


def _prenorm_kernel(x_ref, g_ref, o_ref):
    x = x_ref[...]
    ms = jnp.mean(x * x, axis=-1, keepdims=True)
    o_ref[...] = (x * lax.rsqrt(ms + NORM_EPS) * g_ref[...]).astype(o_ref.dtype)


def _prenorm(x2d, g, tm):
    n, d = x2d.shape
    return pl.pallas_call(
        _prenorm_kernel,
        grid=(n // tm,),
        in_specs=[pl.BlockSpec((tm, d), lambda i: (i, 0)),
                  pl.BlockSpec((1, d), lambda i: (0, 0))],
        out_specs=pl.BlockSpec((tm, d), lambda i: (i, 0)),
        out_shape=jax.ShapeDtypeStruct((n, d), BF16),
        compiler_params=_params(("arbitrary",)),
        name="prenorm",
    )(x2d, g)


def _inproj_kernel(a_ref, am_ref, w_ref, o_ref, om_ref):
    @pl.when(pl.program_id(1) == 0)
    def _():
        om_ref[...] = jnp.dot(am_ref[...], w_ref[...].astype(BF16), preferred_element_type=F32)

    o_ref[...] = jnp.dot(a_ref[...], w_ref[...].astype(BF16), preferred_element_type=F32)


def _inproj(a, a_meta, w, tm, tn):
    n, d = a.shape
    c = w.shape[1]
    nm = a_meta.shape[0]
    return pl.pallas_call(
        _inproj_kernel,
        grid=(c // tn, n // tm),
        in_specs=[pl.BlockSpec((tm, d), lambda j, i: (i, 0)),
                  pl.BlockSpec((nm, d), lambda j, i: (0, 0)),
                  pl.BlockSpec((d, tn), lambda j, i: (0, j))],
        out_specs=[pl.BlockSpec((tm, tn), lambda j, i: (i, j)),
                   pl.BlockSpec((nm, tn), lambda j, i: (0, j))],
        out_shape=[jax.ShapeDtypeStruct((n, c), F32),
                   jax.ShapeDtypeStruct((nm, c), F32)],
        compiler_params=_params(("arbitrary", "arbitrary")),
        name="inproj",
    )(a, a_meta, w)


def _rotary_halves(ref, h, dh, cos, sin):
    half = dh // 2
    t1 = ref[:, h * dh:h * dh + half]
    t2 = ref[:, h * dh + half:(h + 1) * dh]
    return t1 * cos - t2 * sin, t2 * cos + t1 * sin


def _state_update(k1, k2, kdec, v_bf):
    kd = jnp.concatenate([k1 * kdec, k2 * kdec], axis=1).astype(BF16)
    return lax.dot_general(kd, v_bf, (((0,), (0,)), ((), ())), preferred_element_type=F32)


def _meta_kernel(k_ref, v_ref, cc_ref, ch_ref, cos_ref, sin_ref, kdec_ref, s0_ref, u0_ref, *, dh):
    cos = cos_ref[...]
    sin = sin_ref[...]
    scale = dh ** -0.5
    for h in range(RET_HEADS):
        k1, k2 = _rotary_halves(k_ref, h, dh, cos, sin)
        v_bf = v_ref[:, h * dh:(h + 1) * dh].astype(BF16)
        s0_ref[h] = _state_update(k1 * scale, k2 * scale, kdec_ref[h], v_bf)
    u0_ref[...] = cc_ref[CHUNK - SUBLANES:CHUNK, :] * ch_ref[CHUNK - SUBLANES:CHUNK, :]


def _meta_state(pm_pad, cosm, sinm, kdec, r, dh):
    col = lambda cb: pl.BlockSpec((CHUNK, r), lambda i: (0, cb))
    full2 = pl.BlockSpec((CHUNK, LANES), lambda i: (0, 0))
    return pl.pallas_call(
        functools.partial(_meta_kernel, dh=dh),
        grid=(1,),
        in_specs=[col(1), col(2), col(5), col(6), full2, full2,
                  pl.BlockSpec((RET_HEADS, CHUNK, LANES), lambda i: (0, 0, 0))],
        out_specs=[pl.BlockSpec((RET_HEADS, dh, dh), lambda i: (0, 0, 0)),
                   pl.BlockSpec((SUBLANES, r), lambda i: (0, 0))],
        out_shape=[jax.ShapeDtypeStruct((RET_HEADS, dh, dh), F32),
                   jax.ShapeDtypeStruct((SUBLANES, r), F32)],
        compiler_params=_params(("arbitrary",)),
        name="meta_state",
    )(pm_pad, pm_pad, pm_pad, pm_pad, cosm, sinm, kdec)


def _mixer_kernel(cdec_ref, q_ref, k_ref, v_ref, g_ref, cb_ref, cc_ref, ch_ref,
                  cos_ref, sin_ref, dmask_ref, qdec_ref, kdec_ref, cw_ref, s0_ref, u0_ref,
                  mix_ref, state_ref, uext_ref, *, dh, r):
    @pl.when(pl.program_id(1) == 0)
    def _():
        state_ref[...] = s0_ref[...]
        uext_ref[0:SUBLANES, :] = u0_ref[...]

    cos = cos_ref[...]
    sin = sin_ref[...]
    scale = dh ** -0.5
    for h in range(RET_HEADS):
        q1, q2 = _rotary_halves(q_ref, h, dh, cos, sin)
        k1, k2 = _rotary_halves(k_ref, h, dh, cos, sin)
        k1 = k1 * scale
        k2 = k2 * scale
        qb = jnp.concatenate([q1, q2], axis=1).astype(BF16)
        kb = jnp.concatenate([k1, k2], axis=1).astype(BF16)
        v_bf = v_ref[:, h * dh:(h + 1) * dh].astype(BF16)
        scores = lax.dot_general(qb, kb, (((1,), (1,)), ((), ())), preferred_element_type=F32)
        sm = (scores * dmask_ref[h]).astype(BF16)
        inner = jnp.dot(sm, v_bf, preferred_element_type=F32)
        qdec = qdec_ref[h]
        qd = jnp.concatenate([q1 * qdec, q2 * qdec], axis=1).astype(BF16)
        st = state_ref[h]
        cross = jnp.dot(qd, st.astype(BF16), preferred_element_type=F32)
        state_ref[h] = st * cdec_ref[h] + _state_update(k1, k2, kdec_ref[h], v_bf)
        y = inner + cross
        mu = jnp.mean(y, axis=-1, keepdims=True)
        yc = y - mu
        var = jnp.mean(yc * yc, axis=-1, keepdims=True)
        yn = yc * lax.rsqrt(var + GN_EPS)
        gh = g_ref[:, h * dh:(h + 1) * dh]
        mix_ref[:, h * dh:(h + 1) * dh] = (gh * jax.nn.sigmoid(gh) * yn).astype(mix_ref.dtype)

    cw = 512
    for cs in range(0, r, cw):
        sl = slice(cs, cs + cw)
        u = cc_ref[:, sl] * ch_ref[:, sl]
        uext_ref[SUBLANES:SUBLANES + CHUNK, sl] = u
        u1 = uext_ref[SUBLANES - 1:SUBLANES - 1 + CHUNK, sl]
        u2 = uext_ref[SUBLANES - 2:SUBLANES - 2 + CHUNK, sl]
        conv = cw_ref[0:1, sl] * u2 + cw_ref[1:2, sl] * u1 + cw_ref[2:3, sl] * u
        mix_ref[:, r + cs:r + cs + cw] = (cb_ref[:, sl] * conv).astype(mix_ref.dtype)
        uext_ref[0:SUBLANES, sl] = uext_ref[CHUNK:CHUNK + SUBLANES, sl]


def _mixer(p, cos, sin, dmask, qdec, kdec, cdec, conv_w, s0, u0, bsz, seq, r, dh):
    nc = seq // CHUNK
    d = 2 * r
    col = lambda cb: pl.BlockSpec((CHUNK, r), lambda b, c: (b * nc + c, cb))
    tab = pl.BlockSpec((CHUNK, LANES), lambda b, c: (c, 0))
    hconst = pl.BlockSpec((RET_HEADS, CHUNK, LANES), lambda b, c: (0, 0, 0))
    return pl.pallas_call(
        functools.partial(_mixer_kernel, dh=dh, r=r),
        grid=(bsz, nc),
        in_specs=[pl.BlockSpec(memory_space=pltpu.SMEM),
                  col(0), col(1), col(2), col(3), col(4), col(5), col(6),
                  tab, tab, hconst, hconst, hconst,
                  pl.BlockSpec((CONV_K, r), lambda b, c: (0, 0)),
                  pl.BlockSpec((RET_HEADS, dh, dh), lambda b, c: (0, 0, 0)),
                  pl.BlockSpec((SUBLANES, r), lambda b, c: (0, 0))],
        out_specs=pl.BlockSpec((CHUNK, d), lambda b, c: (b * nc + c, 0)),
        out_shape=jax.ShapeDtypeStruct((bsz * seq, d), BF16),
        scratch_shapes=[pltpu.VMEM((RET_HEADS, dh, dh), F32),
                        pltpu.VMEM((CHUNK + 2 * SUBLANES, r), F32)],
        compiler_params=_params(("arbitrary", "arbitrary")),
        name="mixer",
    )(cdec, p, p, p, p, p, p, p, cos, sin, dmask, qdec, kdec, conv_w, s0, u0)


def _outproj_kernel(a_ref, w_ref, x_ref, o_ref):
    o_ref[...] = x_ref[...] + jnp.dot(a_ref[...], w_ref[...].astype(BF16), preferred_element_type=F32)


def _outproj(mix, w, x2d, tm, tn):
    n, d = mix.shape
    c = w.shape[1]
    return pl.pallas_call(
        _outproj_kernel,
        grid=(c // tn, n // tm),
        in_specs=[pl.BlockSpec((tm, d), lambda j, i: (i, 0)),
                  pl.BlockSpec((d, tn), lambda j, i: (0, j)),
                  pl.BlockSpec((tm, tn), lambda j, i: (i, j))],
        out_specs=pl.BlockSpec((tm, tn), lambda j, i: (i, j)),
        out_shape=jax.ShapeDtypeStruct((n, c), F32),
        compiler_params=_params(("arbitrary", "arbitrary")),
        name="outproj",
    )(mix, w, x2d)


def _pack_bf16_pairs(x):
    half = x.shape[1] // 2
    bits = lax.bitcast_convert_type(x.astype(BF16).astype(F32), jnp.uint32)
    return bits[:, half:] | (bits[:, :half] >> 16)


def _unpack_bf16_pairs(w):
    lo = lax.bitcast_convert_type(w << 16, F32)
    hi = lax.bitcast_convert_type(w & jnp.uint32(0xFFFF0000), F32)
    return lo.astype(BF16), hi.astype(BF16)


def _router_kernel(h_ref, g_ref, w_ref, b_ref, m_ref, route_ref, cnt_ref, carry_ref, *, tr):
    @pl.when(pl.program_id(0) == 0)
    def _():
        carry_ref[...] = jnp.zeros_like(carry_ref)

    x = h_ref[...]
    ms = jnp.mean(x * x, axis=-1, keepdims=True)
    m = x * lax.rsqrt(ms + NORM_EPS) * g_ref[...]
    m_ref[...] = _pack_bf16_pairs(m)

    logits = jnp.dot(m.astype(BF16), w_ref[...], preferred_element_type=F32) + b_ref[...]

    lane = lax.broadcasted_iota(jnp.int32, (tr, ROUTE_LANES), 1)
    lane_f = lane.astype(F32)
    work = logits
    vals, onehots = [], []
    for _ in range(TOP_K):
        mx = jnp.max(work, axis=-1, keepdims=True)
        idx = jnp.min(jnp.where(work == mx, lane_f, float(ROUTE_LANES)), axis=-1, keepdims=True)
        oh = lane_f == idx
        vals.append(mx)
        onehots.append(oh)
        work = jnp.where(oh, -jnp.inf, work)

    exps = [jnp.exp(v - vals[0]) for v in vals]
    denom = exps[0] + exps[1] + exps[2] + exps[3]
    gates = [e / denom for e in exps]

    chosen = onehots[0] | onehots[1] | onehots[2] | onehots[3]
    cmat = jnp.where(chosen, 1.0, 0.0).astype(BF16)
    row = lax.broadcasted_iota(jnp.int32, (tr, tr), 0)
    colm = lax.broadcasted_iota(jnp.int32, (tr, tr), 1)
    lower = jnp.where(colm < row, 1.0, 0.0).astype(BF16)
    carry = carry_ref[0:1, :]
    rank_e = jnp.dot(lower, cmat, preferred_element_type=F32) + carry
    new_carry = carry + jnp.sum(cmat.astype(F32), axis=0, keepdims=True)
    carry_ref[...] = jnp.broadcast_to(new_carry, carry_ref.shape)
    cnt_ref[...] = jnp.broadcast_to(new_carry, cnt_ref.shape)

    out = jnp.zeros((tr, ROUTE_LANES), F32)
    for k in range(TOP_K):
        idx_k = jnp.sum(jnp.where(onehots[k], lane_f, 0.0), axis=-1, keepdims=True)
        rank_k = jnp.sum(jnp.where(onehots[k], rank_e, 0.0), axis=-1, keepdims=True)
        out = jnp.where(lane == k, gates[k], out)
        out = jnp.where(lane == TOP_K + k, idx_k, out)
        out = jnp.where(lane == 2 * TOP_K + k, rank_k, out)
    route_ref[...] = out


def _router(h2, g, w, b, tr):
    n, d = h2.shape
    return pl.pallas_call(
        functools.partial(_router_kernel, tr=tr),
        grid=(n // tr,),
        in_specs=[pl.BlockSpec((tr, d), lambda i: (i, 0)),
                  pl.BlockSpec((1, d), lambda i: (0, 0)),
                  pl.BlockSpec((d, ROUTE_LANES), lambda i: (0, 0)),
                  pl.BlockSpec((1, ROUTE_LANES), lambda i: (0, 0))],
        out_specs=[pl.BlockSpec((tr, d // 2), lambda i: (i, 0)),
                   pl.BlockSpec((tr, ROUTE_LANES), lambda i: (i, 0)),
                   pl.BlockSpec((SUBLANES, ROUTE_LANES), lambda i: (0, 0))],
        out_shape=[jax.ShapeDtypeStruct((n, d // 2), jnp.uint32),
                   jax.ShapeDtypeStruct((n, ROUTE_LANES), F32),
                   jax.ShapeDtypeStruct((SUBLANES, ROUTE_LANES), F32)],
        scratch_shapes=[pltpu.VMEM((SUBLANES, ROUTE_LANES), F32)],
        compiler_params=_params(("arbitrary",)),
        name="router",
    )(h2, g, w, b)


def _row_copy(src_ref, src_row, dst_ref, dst_row, sem):
    return pltpu.make_async_copy(src_ref.at[pl.ds(src_row, 1), :], dst_ref.at[pl.ds(dst_row, 1), :], sem)


def _dispatch_kernel(dest_ref, zlo_ref, zhi_ref, m_ref, xg_ref, zero_ref, sem, zsem, *, tt):
    @pl.when(pl.program_id(0) == 0)
    def _():
        zero_ref[...] = jnp.zeros_like(zero_ref)

        def range_body(e, carry):
            def zero_start(rr, c):
                _row_copy(zero_ref, 0, xg_ref, rr, zsem).start()
                return c

            lax.fori_loop(zlo_ref[e], zhi_ref[e], zero_start, 0)
            return carry

        lax.fori_loop(0, N_EXPERTS + 1, range_body, 0)

    @pl.when(pl.program_id(0) == pl.num_programs(0) - 1)
    def _():
        def range_body(e, carry):
            def zero_wait(rr, c):
                _row_copy(zero_ref, 0, xg_ref, rr, zsem).wait()
                return c

            lax.fori_loop(zlo_ref[e], zhi_ref[e], zero_wait, 0)
            return carry

        lax.fori_loop(0, N_EXPERTS + 1, range_body, 0)

    def start_body(t, c):
        for k in range(TOP_K):
            _row_copy(m_ref, t, xg_ref, dest_ref[t * TOP_K + k], sem).start()
        return c

    def wait_body(t, c):
        for k in range(TOP_K):
            _row_copy(m_ref, t, xg_ref, dest_ref[t * TOP_K + k], sem).wait()
        return c

    lax.fori_loop(0, tt, start_body, 0)
    lax.fori_loop(0, tt, wait_body, 0)


def _dispatch(dest_flat, zlo, zhi, m, rows, tt):
    n, d = m.shape
    return pl.pallas_call(
        functools.partial(_dispatch_kernel, tt=tt),
        grid=(n // tt,),
        in_specs=[pl.BlockSpec((tt * TOP_K,), lambda i: (i,), memory_space=pltpu.SMEM),
                  pl.BlockSpec(memory_space=pltpu.SMEM),
                  pl.BlockSpec(memory_space=pltpu.SMEM),
                  pl.BlockSpec((tt, d), lambda i: (i, 0))],
        out_specs=pl.BlockSpec(memory_space=pl.ANY),
        out_shape=jax.ShapeDtypeStruct((rows, d), m.dtype),
        scratch_shapes=[pltpu.VMEM((SUBLANES, d), m.dtype),
                        pltpu.SemaphoreType.DMA(()),
                        pltpu.SemaphoreType.DMA(())],
        compiler_params=_params(("arbitrary",)),
        name="dispatch",
    )(dest_flat, zlo, zhi, m)


def _for_row_blocks(nsub, block_fn):
    n8 = nsub // 8
    rem8 = nsub - n8 * 8

    def body(i, c):
        start = pl.multiple_of(i * (8 * SUB_ROWS), 8 * SUB_ROWS)
        block_fn(start, 4 * SUB_ROWS)
        block_fn(start + 4 * SUB_ROWS, 4 * SUB_ROWS)
        return c

    lax.fori_loop(0, n8, body, 0)
    base8 = n8 * (8 * SUB_ROWS)

    @pl.when(rem8 >= 4)
    def _():
        block_fn(pl.multiple_of(base8, 4 * SUB_ROWS), 4 * SUB_ROWS)

    rem = rem8 % 4
    base = base8 + (rem8 // 4) * (4 * SUB_ROWS)

    @pl.when(rem >= 2)
    def _():
        block_fn(pl.multiple_of(base, SUB_ROWS), 2 * SUB_ROWS)

    @pl.when(rem % 2 == 1)
    def _():
        block_fn(pl.multiple_of(base + (rem // 2) * (2 * SUB_ROWS), SUB_ROWS), SUB_ROWS)


def _expert_kernel(gexp_ref, gsub_ref, gns_ref, tail_ref,
                   xg_ref, wg_ref, wu_ref, bg_ref, bu_ref, wd_ref, bd_ref, yg_ref,
                   xb_ref, hid_ref, stage_ref, ybuf_ref, pend_ref,
                   xsem, ysem, *, nj, nn):
    g = pl.program_id(0)
    s = pl.program_id(1)
    tf = GATE_UP_COLS
    tn = DOWN_COLS
    ns = gns_ref[g]
    row0 = gsub_ref[g] * SUB_ROWS

    def aligned(v, m):
        return v if isinstance(v, int) else pl.multiple_of(v, m)

    def y_copy(sub, slot, col, first_row):
        return pltpu.make_async_copy(
            ybuf_ref.at[slot, pl.ds(aligned(sub * SUB_ROWS, SUB_ROWS), SUB_ROWS), :],
            yg_ref.at[pl.ds(aligned(first_row + sub * SUB_ROWS, SUB_ROWS), SUB_ROWS),
                      pl.ds(aligned(col * tn, tn), tn)],
            ysem.at[slot])

    def x_copy(first_row, sub, slot):
        return pltpu.make_async_copy(
            xg_ref.at[pl.ds(aligned(first_row + sub * SUB_ROWS, SUB_ROWS), SUB_ROWS), :],
            stage_ref.at[slot], xsem.at[slot])

    def stage_to_rows(buf, sub, slot):
        rows = pl.ds(aligned(sub * SUB_ROWS, SUB_ROWS), SUB_ROWS)
        lo, hi = _unpack_bf16_pairs(stage_ref[slot])
        half = lo.shape[1]
        xb_ref[buf, rows, 0:half] = lo
        xb_ref[buf, rows, half:2 * half] = hi

    def wait_pending(slot):
        def body(i, c):
            y_copy(0, slot, 0, 0).wait()
            return c

        lax.fori_loop(0, pend_ref[slot], body, 0)
        pend_ref[slot] = 0

    @pl.when((g == 0) & (s == 0))
    def _():
        pend_ref[0] = 0
        pend_ref[1] = 0
        ybuf_ref[0, 0:SUB_ROWS, :] = jnp.zeros((SUB_ROWS, tn), F32)
        lo = tail_ref[0]
        hi = tail_ref[1]

        def start_body(i, c):
            for col in range(nn):
                y_copy(0, 0, col, i * SUB_ROWS).start()
            return c

        def wait_body(i, c):
            for col in range(nn):
                y_copy(0, 0, col, i * SUB_ROWS).wait()
            return c

        lax.fori_loop(lo, hi, start_body, 0)
        lax.fori_loop(lo, hi, wait_body, 0)

        x_copy(row0, 0, 0).start()

        def first_body(sub, c):
            slot = lax.rem(sub, 2)

            @pl.when(sub + 1 < ns)
            def _():
                x_copy(row0, sub + 1, 1 - slot).start()

            x_copy(row0, sub, slot).wait()
            stage_to_rows(0, sub, slot)
            return c

        lax.fori_loop(0, ns, first_body, 0)

    @pl.when(ns > 0)
    def _():
        @pl.when(s < nj)
        def _():
            bg = bg_ref[gexp_ref[g], pl.ds(s, 1), :]
            bu = bu_ref[gexp_ref[g], pl.ds(s, 1), :]
            cur = lax.rem(g, 2)

            last_g = pl.num_programs(0) - 1
            g_next = jnp.minimum(g + 1, last_g)
            ns_next = jnp.where(g < last_g, gns_ref[g_next], 0)
            row0_next = gsub_ref[g_next] * SUB_ROWS
            for k in range(2):
                @pl.when(2 * s + k < ns_next)
                def _():
                    x_copy(row0_next, 2 * s + k, k).start()

            def block(start, rows_n):
                rows = pl.ds(start, rows_n)
                x = xb_ref[cur, rows, :]
                gt = jnp.dot(x, wg_ref[0].astype(BF16), preferred_element_type=F32) + bg
                up = jnp.dot(x, wu_ref[0].astype(BF16), preferred_element_type=F32) + bu
                gt = jnp.minimum(gt, SWIGLU_LIMIT)
                up = jnp.clip(up, -SWIGLU_LIMIT, SWIGLU_LIMIT)
                hid = (up + 1.0) * gt * jax.nn.sigmoid(SWIGLU_ALPHA * gt)
                hid_ref[s, rows, :] = hid.astype(hid_ref.dtype)

            _for_row_blocks(ns, block)

            for k in range(2):
                @pl.when(2 * s + k < ns_next)
                def _():
                    x_copy(row0_next, 2 * s + k, k).wait()
                    stage_to_rows(1 - cur, 2 * s + k, k)

        @pl.when(s >= nj)
        def _():
            col = s - nj
            slot = lax.rem(col, 2)
            bd = bd_ref[gexp_ref[g], pl.ds(col, 1), :]
            wait_pending(slot)

            def block(start, rows_n):
                rows = pl.ds(start, rows_n)
                acc = jnp.dot(hid_ref[0, rows, :], wd_ref[0, 0:tf, :].astype(BF16),
                              preferred_element_type=F32)
                for c in range(1, nj):
                    acc = acc + jnp.dot(hid_ref[c, rows, :], wd_ref[0, c * tf:(c + 1) * tf, :].astype(BF16),
                                        preferred_element_type=F32)
                ybuf_ref[slot, rows, :] = acc + bd

            _for_row_blocks(ns, block)

            def start_body(sub, c):
                y_copy(sub, slot, col, row0).start()
                return c

            lax.fori_loop(0, ns, start_body, 0)
            pend_ref[slot] = ns

    @pl.when((g == pl.num_programs(0) - 1) & (s == nj + nn - 1))
    def _():
        wait_pending(0)
        wait_pending(1)


def _experts(gexp, gsub, gns, tail, xg, w_gate, w_up, b_gate, b_up, w_down, b_down, n_groups):
    rows = xg.shape[0]
    d = w_gate.shape[1]
    f = w_gate.shape[2]
    tf, tn = GATE_UP_COLS, DOWN_COLS
    nj, nn = f // tf, d // tn
    assert GROUP_SUBS <= 2 * nj, "next group's rows are fetched two sub-blocks per gate/up step"

    def up_idx(g, s, ge, gs, gn, tl):
        act = jnp.minimum(gn[g], 1)
        return (ge[g], 0, jnp.minimum(s, nj - 1) * act + (nj - 1) * (1 - act))

    def down_idx(g, s, ge, gs, gn, tl):
        act = jnp.minimum(gn[g], 1)
        return (ge[g], 0, jnp.maximum(s - nj, 0) * act + (nn - 1) * (1 - act))

    def whole(g, s, ge, gs, gn, tl):
        return (0, 0, 0)

    grid_spec = pltpu.PrefetchScalarGridSpec(
        num_scalar_prefetch=4,
        grid=(n_groups, nj + nn),
        in_specs=[
            pl.BlockSpec(memory_space=pl.ANY),
            pl.BlockSpec((1, d, tf), up_idx),
            pl.BlockSpec((1, d, tf), up_idx),
            pl.BlockSpec((N_EXPERTS, nj, tf), whole),
            pl.BlockSpec((N_EXPERTS, nj, tf), whole),
            pl.BlockSpec((1, f, tn), down_idx),
            pl.BlockSpec((N_EXPERTS, nn, tn), whole),
        ],
        out_specs=pl.BlockSpec(memory_space=pl.ANY),
        scratch_shapes=[pltpu.VMEM((2, GROUP_ROWS, d), BF16),
                        pltpu.VMEM((nj, GROUP_ROWS, tf), BF16),
                        pltpu.VMEM((2, SUB_ROWS, d // 2), xg.dtype),
                        pltpu.VMEM((2, GROUP_ROWS, tn), F32),
                        pltpu.SMEM((2,), jnp.int32),
                        pltpu.SemaphoreType.DMA((2,)),
                        pltpu.SemaphoreType.DMA((2,))],
    )
    return pl.pallas_call(
        functools.partial(_expert_kernel, nj=nj, nn=nn),
        grid_spec=grid_spec,
        out_shape=jax.ShapeDtypeStruct((rows, d), F32),
        compiler_params=_params(("arbitrary", "arbitrary")),
        name="experts",
    )(gexp, gsub, gns, tail, xg, w_gate, w_up, b_gate.reshape(N_EXPERTS, nj, tf),
      b_up.reshape(N_EXPERTS, nj, tf), w_down, b_down.reshape(N_EXPERTS, nn, tn))


def _combine_kernel(dcur_ref, dnext_ref, route_ref, h_ref, g_ref, yg_ref, o_ref, buf_ref, sem, *, tt):
    i = pl.program_id(0)
    n_steps = pl.num_programs(0)
    slot = lax.rem(i, 2)

    def start_tile(dref, sl):
        def body(t, c):
            for k in range(TOP_K):
                pltpu.make_async_copy(yg_ref.at[pl.ds(dref[t * TOP_K + k], 1), :],
                                      buf_ref.at[sl, k, pl.ds(t, 1), :], sem.at[sl]).start()
            return c

        lax.fori_loop(0, tt, body, 0)

    @pl.when(i == 0)
    def _():
        start_tile(dcur_ref, 0)

    @pl.when(i + 1 < n_steps)
    def _():
        start_tile(dnext_ref, 1 - slot)

    def wait_body(t, c):
        for k in range(TOP_K):
            pltpu.make_async_copy(yg_ref.at[pl.ds(0, 1), :],
                                  buf_ref.at[slot, k, pl.ds(t, 1), :], sem.at[slot]).wait()
        return c

    lax.fori_loop(0, tt, wait_body, 0)

    d = o_ref.shape[1]
    cw = 2 * LANES
    gates = [route_ref[:, k:k + 1] for k in range(TOP_K)]
    ssq = jnp.zeros((tt, 1), F32)
    for cs in range(0, d, cw):
        acc = h_ref[:, cs:cs + cw]
        for k in range(TOP_K):
            acc = acc + buf_ref[slot, k, :, cs:cs + cw] * gates[k]
        o_ref[:, cs:cs + cw] = acc
        ssq = ssq + jnp.sum(acc * acc, axis=-1, keepdims=True)
    scale = lax.rsqrt(ssq * (1.0 / d) + NORM_EPS)
    for cs in range(0, d, cw):
        o_ref[:, cs:cs + cw] = o_ref[:, cs:cs + cw] * scale * g_ref[:, cs:cs + cw]


def _combine(dest_flat, route, h2, g, yg, tt):
    n, d = h2.shape
    steps = n // tt
    return pl.pallas_call(
        functools.partial(_combine_kernel, tt=tt),
        grid=(steps,),
        in_specs=[pl.BlockSpec((tt * TOP_K,), lambda i: (i,), memory_space=pltpu.SMEM),
                  pl.BlockSpec((tt * TOP_K,), lambda i: (jnp.minimum(i + 1, steps - 1),),
                               memory_space=pltpu.SMEM),
                  pl.BlockSpec((tt, ROUTE_LANES), lambda i: (i, 0)),
                  pl.BlockSpec((tt, d), lambda i: (i, 0)),
                  pl.BlockSpec((1, d), lambda i: (0, 0)),
                  pl.BlockSpec(memory_space=pl.ANY)],
        out_specs=pl.BlockSpec((tt, d), lambda i: (i, 0)),
        out_shape=jax.ShapeDtypeStruct((n, d), F32),
        scratch_shapes=[pltpu.VMEM((2, TOP_K, tt, d), F32),
                        pltpu.SemaphoreType.DMA((2,))],
        compiler_params=_params(("arbitrary",)),
        name="combine",
    )(dest_flat, dest_flat, route, h2, g, yg)


def _retention_tables(dh):
    lg = jnp.log1p(-jnp.exp2(-5.0 - jnp.arange(RET_HEADS, dtype=F32)))
    i = jnp.arange(CHUNK, dtype=F32)
    rel = i[:, None] - i[None, :]
    causal = rel >= 0
    dmask = jnp.where(causal[None], jnp.exp(jnp.where(causal, rel, 0.0)[None] * lg[:, None, None]), 0.0)
    q_dec = jnp.exp((i + 1.0)[None, :] * lg[:, None])
    k_dec = jnp.exp((CHUNK - 1.0 - i)[None, :] * lg[:, None])
    c_dec = jnp.exp(CHUNK * lg)
    bcast = lambda t: jnp.broadcast_to(t[:, :, None], (RET_HEADS, CHUNK, LANES))
    return dmask, bcast(q_dec), bcast(k_dec), c_dec


def _rotary_tables(length, dh):
    half = dh // 2
    inv = ROPE_BASE ** (-jnp.arange(half, dtype=F32) / half)
    ang = jnp.arange(length, dtype=jnp.int32).astype(F32)[:, None] * inv[None, :]
    return jnp.cos(ang), jnp.sin(ang)


def kernel(x, meta_tokens, norm_mix, w_in, conv_w, w_out, norm_ffn, router_w, router_b,
           w_gate, b_gate, w_up, b_up, w_down, b_down, norm_final):
    bsz, seq, d = x.shape
    assert w_in.shape[0] == 1, "single-layer stack expected"
    assert meta_tokens.shape[0] == N_META and seq % CHUNK == 0
    r = d // 2
    dh = r // RET_HEADS
    assert dh // 2 == LANES
    n = bsz * seq
    x2d = x.reshape(n, d)

    a = _prenorm(x2d, norm_mix[0][None, :], NORM_ROWS)
    a_meta = _prenorm(meta_tokens.astype(x.dtype), norm_mix[0][None, :], N_META)
    p, pm = _inproj(a, a_meta, w_in[0], PROJ_ROWS, PROJ_COLS)

    cos, sin = _rotary_tables(N_META + seq, dh)
    pad_meta = lambda t: jnp.pad(t, ((CHUNK - N_META, 0), (0, 0)))
    dmask, qdec, kdec, cdec = _retention_tables(dh)
    s0, u0 = _meta_state(pad_meta(pm), pad_meta(cos[:N_META]), pad_meta(sin[:N_META]), kdec, r, dh)
    mix = _mixer(p, cos[N_META:], sin[N_META:], dmask, qdec, kdec, cdec, conv_w[0], s0, u0,
                 bsz, seq, r, dh)
    h2 = _outproj(mix, w_out[0], x2d, PROJ_ROWS, PROJ_COLS)

    rw = jnp.pad(router_w[0], ((0, 0), (0, ROUTE_LANES - N_EXPERTS))).astype(BF16)
    rb = jnp.pad(router_b[0].astype(F32), (0, ROUTE_LANES - N_EXPERTS), constant_values=-1e30)[None, :]
    m, route, cnt = _router(h2, norm_ffn[0][None, :], rw, rb, ROUTER_ROWS)

    side = route[:, TOP_K:3 * TOP_K].T.astype(jnp.int32)
    ids, rank = side[:TOP_K], side[TOP_K:]
    counts = cnt[0, :N_EXPERTS].astype(jnp.int32)

    total_sub = (n * TOP_K + N_EXPERTS * (SUB_ROWS - 1) + SUB_ROWS - 1) // SUB_ROWS
    psub = (counts + SUB_ROWS - 1) // SUB_ROWS
    pend = jnp.cumsum(psub) * SUB_ROWS
    pstart = pend - psub * SUB_ROWS
    first_row = jnp.zeros_like(ids)
    for e in range(N_EXPERTS):
        first_row = jnp.where(ids == e, pstart[e], first_row)
    dest_flat = (first_row + rank).T.reshape(-1).astype(jnp.int32)
    zlo = jnp.concatenate([pstart + counts, pend[-1:]]).astype(jnp.int32)
    zhi = jnp.concatenate([pend, jnp.full((1,), total_sub * SUB_ROWS)]).astype(jnp.int32)

    n_groups_max = N_EXPERTS + total_sub // GROUP_SUBS
    ngrp = (psub + GROUP_SUBS - 1) // GROUP_SUBS
    gend = jnp.cumsum(ngrp)
    gstart = gend - ngrp
    n_groups = gend[-1]
    gidx = jnp.arange(n_groups_max, dtype=jnp.int32)
    glast = jnp.minimum(gidx, n_groups - 1)
    gexp = jnp.minimum(jnp.searchsorted(gend, glast, side="right"), N_EXPERTS - 1).astype(jnp.int32)
    gq = glast - gstart[gexp]
    gsub = (pstart[gexp] // SUB_ROWS + gq * GROUP_SUBS).astype(jnp.int32)
    gns = jnp.where(gidx < n_groups, jnp.minimum(psub[gexp] - gq * GROUP_SUBS, GROUP_SUBS), 0).astype(jnp.int32)
    tail = jnp.stack([pend[-1] // SUB_ROWS, jnp.asarray(total_sub, pend.dtype)]).astype(jnp.int32)

    xg = _dispatch(dest_flat, zlo, zhi, m, total_sub * SUB_ROWS, DISPATCH_ROWS)
    yg = _experts(gexp, gsub, gns, tail, xg, w_gate[0], w_up[0], b_gate[0], b_up[0],
                  w_down[0], b_down[0], n_groups.astype(jnp.int32))
    out = _combine(dest_flat, route, h2, norm_final[None, :], yg, COMBINE_ROWS)
    return out.reshape(bsz, seq, d)
```

```python
import functools

import jax
import jax.numpy as jnp
from jax import lax
from jax.experimental import pallas as pl
from jax.experimental.pallas import tpu as pltpu

F32 = jnp.float32
BF16 = jnp.bfloat16

N_META = 16
RET_HEADS = 8
CHUNK = 128
CONV_K = 3
ROPE_BASE = 10000.0
N_EXPERTS = 32
TOP_K = 4
SWIGLU_LIMIT = 7.0
SWIGLU_ALPHA = 1.702
NORM_EPS = 1e-5
GN_EPS = 1e-6

LANES = 128
SUBLANES = 8
VMEM_LIMIT = 58 * 1024 * 1024

SUB_ROWS = 128
GROUP_SUBS = 9
GROUP_ROWS = GROUP_SUBS * SUB_ROWS
GATE_UP_COLS = 256
DOWN_COLS = 512
ROUTE_LANES = 128
NORM_ROWS = 512
PROJ_ROWS = 512
PROJ_COLS = 1024
ROUTER_ROWS = 256
DISPATCH_ROWS = 256
COMBINE_ROWS = 128


def _params(sem, vmem=VMEM_LIMIT):
    return pltpu.CompilerParams(dimension_semantics=sem, vmem_limit_bytes=vmem)


def _prenorm_kernel(x_ref, g_ref, o_ref):
    x = x_ref[...]
    ms = jnp.mean(x * x, axis=-1, keepdims=True)
    o_ref[...] = (x * lax.rsqrt(ms + NORM_EPS) * g_ref[...]).astype(o_ref.dtype)


def _prenorm(x2d, g, tm):
    n, d = x2d.shape
    return pl.pallas_call(
        _prenorm_kernel,
        grid=(n // tm,),
        in_specs=[pl.BlockSpec((tm, d), lambda i: (i, 0)),
                  pl.BlockSpec((1, d), lambda i: (0, 0))],
        out_specs=pl.BlockSpec((tm, d), lambda i: (i, 0)),
        out_shape=jax.ShapeDtypeStruct((n, d), BF16),
        compiler_params=_params(("arbitrary",)),
        name="prenorm",
    )(x2d, g)


def _inproj_kernel(a_ref, am_ref, w_ref, o_ref, om_ref):
    @pl.when(pl.program_id(1) == 0)
    def _():
        om_ref[...] = jnp.dot(am_ref[...], w_ref[...].astype(BF16), preferred_element_type=F32)

    o_ref[...] = jnp.dot(a_ref[...], w_ref[...].astype(BF16), preferred_element_type=F32)


def _inproj(a, a_meta, w, tm, tn):
    n, d = a.shape
    c = w.shape[1]
    nm = a_meta.shape[0]
    return pl.pallas_call(
        _inproj_kernel,
        grid=(c // tn, n // tm),
        in_specs=[pl.BlockSpec((tm, d), lambda j, i: (i, 0)),
                  pl.BlockSpec((nm, d), lambda j, i: (0, 0)),
                  pl.BlockSpec((d, tn), lambda j, i: (0, j))],
        out_specs=[pl.BlockSpec((tm, tn), lambda j, i: (i, j)),
                   pl.BlockSpec((nm, tn), lambda j, i: (0, j))],
        out_shape=[jax.ShapeDtypeStruct((n, c), F32),
                   jax.ShapeDtypeStruct((nm, c), F32)],
        compiler_params=_params(("arbitrary", "arbitrary")),
        name="inproj",
    )(a, a_meta, w)


def _rotary_halves(ref, h, dh, cos, sin):
    half = dh // 2
    t1 = ref[:, h * dh:h * dh + half]
    t2 = ref[:, h * dh + half:(h + 1) * dh]
    return t1 * cos - t2 * sin, t2 * cos + t1 * sin


def _state_update(k1, k2, kdec, v_bf):
    kd = jnp.concatenate([k1 * kdec, k2 * kdec], axis=1).astype(BF16)
    return lax.dot_general(kd, v_bf, (((0,), (0,)), ((), ())), preferred_element_type=F32)


def _meta_kernel(k_ref, v_ref, cc_ref, ch_ref, cos_ref, sin_ref, kdec_ref, s0_ref, u0_ref, *, dh):
    cos = cos_ref[...]
    sin = sin_ref[...]
    scale = dh ** -0.5
    for h in range(RET_HEADS):
        k1, k2 = _rotary_halves(k_ref, h, dh, cos, sin)
        v_bf = v_ref[:, h * dh:(h + 1) * dh].astype(BF16)
        s0_ref[h] = _state_update(k1 * scale, k2 * scale, kdec_ref[h], v_bf)
    u0_ref[...] = cc_ref[CHUNK - SUBLANES:CHUNK, :] * ch_ref[CHUNK - SUBLANES:CHUNK, :]


def _meta_state(pm_pad, cosm, sinm, kdec, r, dh):
    col = lambda cb: pl.BlockSpec((CHUNK, r), lambda i: (0, cb))
    full2 = pl.BlockSpec((CHUNK, LANES), lambda i: (0, 0))
    return pl.pallas_call(
        functools.partial(_meta_kernel, dh=dh),
        grid=(1,),
        in_specs=[col(1), col(2), col(5), col(6), full2, full2,
                  pl.BlockSpec((RET_HEADS, CHUNK, LANES), lambda i: (0, 0, 0))],
        out_specs=[pl.BlockSpec((RET_HEADS, dh, dh), lambda i: (0, 0, 0)),
                   pl.BlockSpec((SUBLANES, r), lambda i: (0, 0))],
        out_shape=[jax.ShapeDtypeStruct((RET_HEADS, dh, dh), F32),
                   jax.ShapeDtypeStruct((SUBLANES, r), F32)],
        compiler_params=_params(("arbitrary",)),
        name="meta_state",
    )(pm_pad, pm_pad, pm_pad, pm_pad, cosm, sinm, kdec)


def _mixer_kernel(cdec_ref, q_ref, k_ref, v_ref, g_ref, cb_ref, cc_ref, ch_ref,
                  cos_ref, sin_ref, dmask_ref, qdec_ref, kdec_ref, cw_ref, s0_ref, u0_ref,
                  mix_ref, state_ref, uext_ref, *, dh, r):
    @pl.when(pl.program_id(1) == 0)
    def _():
        state_ref[...] = s0_ref[...]
        uext_ref[0:SUBLANES, :] = u0_ref[...]

    cos = cos_ref[...]
    sin = sin_ref[...]
    scale = dh ** -0.5
    for h in range(RET_HEADS):
        q1, q2 = _rotary_halves(q_ref, h, dh, cos, sin)
        k1, k2 = _rotary_halves(k_ref, h, dh, cos, sin)
        k1 = k1 * scale
        k2 = k2 * scale
        qb = jnp.concatenate([q1, q2], axis=1).astype(BF16)
        kb = jnp.concatenate([k1, k2], axis=1).astype(BF16)
        v_bf = v_ref[:, h * dh:(h + 1) * dh].astype(BF16)
        scores = lax.dot_general(qb, kb, (((1,), (1,)), ((), ())), preferred_element_type=F32)
        sm = (scores * dmask_ref[h]).astype(BF16)
        inner = jnp.dot(sm, v_bf, preferred_element_type=F32)
        qdec = qdec_ref[h]
        qd = jnp.concatenate([q1 * qdec, q2 * qdec], axis=1).astype(BF16)
        st = state_ref[h]
        cross = jnp.dot(qd, st.astype(BF16), preferred_element_type=F32)
        state_ref[h] = st * cdec_ref[h] + _state_update(k1, k2, kdec_ref[h], v_bf)
        y = inner + cross
        mu = jnp.mean(y, axis=-1, keepdims=True)
        yc = y - mu
        var = jnp.mean(yc * yc, axis=-1, keepdims=True)
        yn = yc * lax.rsqrt(var + GN_EPS)
        gh = g_ref[:, h * dh:(h + 1) * dh]
        mix_ref[:, h * dh:(h + 1) * dh] = (gh * jax.nn.sigmoid(gh) * yn).astype(mix_ref.dtype)

    cw = 512
    for cs in range(0, r, cw):
        sl = slice(cs, cs + cw)
        u = cc_ref[:, sl] * ch_ref[:, sl]
        uext_ref[SUBLANES:SUBLANES + CHUNK, sl] = u
        u1 = uext_ref[SUBLANES - 1:SUBLANES - 1 + CHUNK, sl]
        u2 = uext_ref[SUBLANES - 2:SUBLANES - 2 + CHUNK, sl]
        conv = cw_ref[0:1, sl] * u2 + cw_ref[1:2, sl] * u1 + cw_ref[2:3, sl] * u
        mix_ref[:, r + cs:r + cs + cw] = (cb_ref[:, sl] * conv).astype(mix_ref.dtype)
        uext_ref[0:SUBLANES, sl] = uext_ref[CHUNK:CHUNK + SUBLANES, sl]


def _mixer(p, cos, sin, dmask, qdec, kdec, cdec, conv_w, s0, u0, bsz, seq, r, dh):
    nc = seq // CHUNK
    d = 2 * r
    col = lambda cb: pl.BlockSpec((CHUNK, r), lambda b, c: (b * nc + c, cb))
    tab = pl.BlockSpec((CHUNK, LANES), lambda b, c: (c, 0))
    hconst = pl.BlockSpec((RET_HEADS, CHUNK, LANES), lambda b, c: (0, 0, 0))
    return pl.pallas_call(
        functools.partial(_mixer_kernel, dh=dh, r=r),
        grid=(bsz, nc),
        in_specs=[pl.BlockSpec(memory_space=pltpu.SMEM),
                  col(0), col(1), col(2), col(3), col(4), col(5), col(6),
                  tab, tab, hconst, hconst, hconst,
                  pl.BlockSpec((CONV_K, r), lambda b, c: (0, 0)),
                  pl.BlockSpec((RET_HEADS, dh, dh), lambda b, c: (0, 0, 0)),
                  pl.BlockSpec((SUBLANES, r), lambda b, c: (0, 0))],
        out_specs=pl.BlockSpec((CHUNK, d), lambda b, c: (b * nc + c, 0)),
        out_shape=jax.ShapeDtypeStruct((bsz * seq, d), BF16),
        scratch_shapes=[pltpu.VMEM((RET_HEADS, dh, dh), F32),
                        pltpu.VMEM((CHUNK + 2 * SUBLANES, r), F32)],
        compiler_params=_params(("arbitrary", "arbitrary")),
        name="mixer",
    )(cdec, p, p, p, p, p, p, p, cos, sin, dmask, qdec, kdec, conv_w, s0, u0)


def _outproj_kernel(a_ref, w_ref, x_ref, o_ref):
    o_ref[...] = x_ref[...] + jnp.dot(a_ref[...], w_ref[...].astype(BF16), preferred_element_type=F32)


def _outproj(mix, w, x2d, tm, tn):
    n, d = mix.shape
    c = w.shape[1]
    return pl.pallas_call(
        _outproj_kernel,
        grid=(c // tn, n // tm),
        in_specs=[pl.BlockSpec((tm, d), lambda j, i: (i, 0)),
                  pl.BlockSpec((d, tn), lambda j, i: (0, j)),
                  pl.BlockSpec((tm, tn), lambda j, i: (i, j))],
        out_specs=pl.BlockSpec((tm, tn), lambda j, i: (i, j)),
        out_shape=jax.ShapeDtypeStruct((n, c), F32),
        compiler_params=_params(("arbitrary", "arbitrary")),
        name="outproj",
    )(mix, w, x2d)


def _pack_bf16_pairs(x):
    half = x.shape[1] // 2
    bits = lax.bitcast_convert_type(x.astype(BF16).astype(F32), jnp.uint32)
    return bits[:, half:] | (bits[:, :half] >> 16)


def _unpack_bf16_pairs(w):
    lo = lax.bitcast_convert_type(w << 16, F32)
    hi = lax.bitcast_convert_type(w & jnp.uint32(0xFFFF0000), F32)
    return lo.astype(BF16), hi.astype(BF16)


def _router_kernel(h_ref, g_ref, w_ref, b_ref, m_ref, route_ref, route_t_ref, cnt_ref, carry_ref, *, tr):
    @pl.when(pl.program_id(0) == 0)
    def _():
        carry_ref[...] = jnp.zeros_like(carry_ref)

    x = h_ref[...]
    ms = jnp.mean(x * x, axis=-1, keepdims=True)
    m = x * lax.rsqrt(ms + NORM_EPS) * g_ref[...]
    m_ref[...] = _pack_bf16_pairs(m)

    logits = jnp.dot(m.astype(BF16), w_ref[...], preferred_element_type=F32) + b_ref[...]

    lane = lax.broadcasted_iota(jnp.int32, (tr, ROUTE_LANES), 1)
    lane_f = lane.astype(F32)
    work = logits
    vals, onehots = [], []
    for _ in range(TOP_K):
        mx = jnp.max(work, axis=-1, keepdims=True)
        idx = jnp.min(jnp.where(work == mx, lane_f, float(ROUTE_LANES)), axis=-1, keepdims=True)
        oh = lane_f == idx
        vals.append(mx)
        onehots.append(oh)
        work = jnp.where(oh, -jnp.inf, work)

    exps = [jnp.exp(v - vals[0]) for v in vals]
    denom = exps[0] + exps[1] + exps[2] + exps[3]
    gates = [e / denom for e in exps]

    chosen = onehots[0] | onehots[1] | onehots[2] | onehots[3]
    cmat = jnp.where(chosen, 1.0, 0.0).astype(BF16)
    row = lax.broadcasted_iota(jnp.int32, (tr, tr), 0)
    colm = lax.broadcasted_iota(jnp.int32, (tr, tr), 1)
    lower = jnp.where(colm < row, 1.0, 0.0).astype(BF16)
    carry = carry_ref[0:1, :]
    rank_e = jnp.dot(lower, cmat, preferred_element_type=F32) + carry
    new_carry = carry + jnp.sum(cmat.astype(F32), axis=0, keepdims=True)
    carry_ref[...] = jnp.broadcast_to(new_carry, carry_ref.shape)
    cnt_ref[...] = jnp.broadcast_to(new_carry, cnt_ref.shape)

    out = jnp.zeros((tr, ROUTE_LANES), F32)
    for k in range(TOP_K):
        idx_k = jnp.sum(jnp.where(onehots[k], lane_f, 0.0), axis=-1, keepdims=True)
        rank_k = jnp.sum(jnp.where(onehots[k], rank_e, 0.0), axis=-1, keepdims=True)
        out = jnp.where(lane == k, gates[k], out)
        out = jnp.where(lane == TOP_K + k, idx_k, out)
        out = jnp.where(lane == 2 * TOP_K + k, rank_k, out)
    route_ref[...] = out
    route_t_ref[...] = out.T


def _router(h2, g, w, b, tr):
    n, d = h2.shape
    return pl.pallas_call(
        functools.partial(_router_kernel, tr=tr),
        grid=(n // tr,),
        in_specs=[pl.BlockSpec((tr, d), lambda i: (i, 0)),
                  pl.BlockSpec((1, d), lambda i: (0, 0)),
                  pl.BlockSpec((d, ROUTE_LANES), lambda i: (0, 0)),
                  pl.BlockSpec((1, ROUTE_LANES), lambda i: (0, 0))],
        out_specs=[pl.BlockSpec((tr, d // 2), lambda i: (i, 0)),
                   pl.BlockSpec((tr, ROUTE_LANES), lambda i: (i, 0)),
                   pl.BlockSpec((ROUTE_LANES, tr), lambda i: (0, i)),
                   pl.BlockSpec((SUBLANES, ROUTE_LANES), lambda i: (0, 0))],
        out_shape=[jax.ShapeDtypeStruct((n, d // 2), jnp.uint32),
                   jax.ShapeDtypeStruct((n, ROUTE_LANES), F32),
                   jax.ShapeDtypeStruct((ROUTE_LANES, n), F32),
                   jax.ShapeDtypeStruct((SUBLANES, ROUTE_LANES), F32)],
        scratch_shapes=[pltpu.VMEM((SUBLANES, ROUTE_LANES), F32)],
        compiler_params=_params(("arbitrary",)),
        name="router",
    )(h2, g, w, b)


def _row_copy(src_ref, src_row, dst_ref, dst_row, sem):
    return pltpu.make_async_copy(src_ref.at[pl.ds(src_row, 1), :], dst_ref.at[pl.ds(dst_row, 1), :], sem)


def _dispatch_kernel(d0_ref, d1_ref, d2_ref, d3_ref, zlo_ref, zhi_ref, m_ref, xg_ref, zero_ref, sem, zsem,
                     *, tt):
    dest_refs = (d0_ref, d1_ref, d2_ref, d3_ref)
    @pl.when(pl.program_id(0) == 0)
    def _():
        zero_ref[...] = jnp.zeros_like(zero_ref)

        def range_body(e, carry):
            def zero_start(rr, c):
                _row_copy(zero_ref, 0, xg_ref, rr, zsem).start()
                return c

            lax.fori_loop(zlo_ref[e], zhi_ref[e], zero_start, 0)
            return carry

        lax.fori_loop(0, N_EXPERTS + 1, range_body, 0)

    @pl.when(pl.program_id(0) == pl.num_programs(0) - 1)
    def _():
        def range_body(e, carry):
            def zero_wait(rr, c):
                _row_copy(zero_ref, 0, xg_ref, rr, zsem).wait()
                return c

            lax.fori_loop(zlo_ref[e], zhi_ref[e], zero_wait, 0)
            return carry

        lax.fori_loop(0, N_EXPERTS + 1, range_body, 0)

    def start_body(t, c):
        for k in range(TOP_K):
            _row_copy(m_ref, t, xg_ref, dest_refs[k][t], sem).start()
        return c

    def wait_body(t, c):
        for k in range(TOP_K):
            _row_copy(m_ref, t, xg_ref, dest_refs[k][t], sem).wait()
        return c

    lax.fori_loop(0, tt, start_body, 0)
    lax.fori_loop(0, tt, wait_body, 0)


def _slot_specs(steps, tt, step_of):
    return [pl.BlockSpec((tt,), lambda i, k=k: (k * steps + step_of(i),), memory_space=pltpu.SMEM)
            for k in range(TOP_K)]


def _dispatch(dest_flat, zlo, zhi, m, rows, tt):
    n, d = m.shape
    steps = n // tt
    return pl.pallas_call(
        functools.partial(_dispatch_kernel, tt=tt),
        grid=(steps,),
        in_specs=_slot_specs(steps, tt, lambda i: i) + [
                  pl.BlockSpec(memory_space=pltpu.SMEM),
                  pl.BlockSpec(memory_space=pltpu.SMEM),
                  pl.BlockSpec((tt, d), lambda i: (i, 0))],
        out_specs=pl.BlockSpec(memory_space=pl.ANY),
        out_shape=jax.ShapeDtypeStruct((rows, d), m.dtype),
        scratch_shapes=[pltpu.VMEM((SUBLANES, d), m.dtype),
                        pltpu.SemaphoreType.DMA(()),
                        pltpu.SemaphoreType.DMA(())],
        compiler_params=_params(("arbitrary",)),
        name="dispatch",
    )(dest_flat, dest_flat, dest_flat, dest_flat, zlo, zhi, m)


def _for_row_blocks(nsub, block_fn):
    n8 = nsub // 8
    rem8 = nsub - n8 * 8

    def body(i, c):
        start = pl.multiple_of(i * (8 * SUB_ROWS), 8 * SUB_ROWS)
        block_fn(start, 4 * SUB_ROWS)
        block_fn(start + 4 * SUB_ROWS, 4 * SUB_ROWS)
        return c

    lax.fori_loop(0, n8, body, 0)
    base8 = n8 * (8 * SUB_ROWS)

    @pl.when(rem8 >= 4)
    def _():
        block_fn(pl.multiple_of(base8, 4 * SUB_ROWS), 4 * SUB_ROWS)

    rem = rem8 % 4
    base = base8 + (rem8 // 4) * (4 * SUB_ROWS)

    @pl.when(rem >= 2)
    def _():
        block_fn(pl.multiple_of(base, SUB_ROWS), 2 * SUB_ROWS)

    @pl.when(rem % 2 == 1)
    def _():
        block_fn(pl.multiple_of(base + (rem // 2) * (2 * SUB_ROWS), SUB_ROWS), SUB_ROWS)


def _expert_kernel(gexp_ref, gsub_ref, gns_ref, tail_ref,
                   xg_ref, wg_ref, wu_ref, bg_ref, bu_ref, wd_ref, bd_ref, yg_ref,
                   xb_ref, hid_ref, stage_ref, ybuf_ref, pend_ref,
                   xsem, ysem, *, nj, nn):
    g = pl.program_id(0)
    s = pl.program_id(1)
    tf = GATE_UP_COLS
    tn = DOWN_COLS
    ns = gns_ref[g]
    row0 = gsub_ref[g] * SUB_ROWS

    def aligned(v, m):
        return v if isinstance(v, int) else pl.multiple_of(v, m)

    def y_copy(sub, slot, col, first_row):
        return pltpu.make_async_copy(
            ybuf_ref.at[slot, pl.ds(aligned(sub * SUB_ROWS, SUB_ROWS), SUB_ROWS), :],
            yg_ref.at[pl.ds(aligned(first_row + sub * SUB_ROWS, SUB_ROWS), SUB_ROWS),
                      pl.ds(aligned(col * tn, tn), tn)],
            ysem.at[slot])

    def x_copy(first_row, sub, slot):
        return pltpu.make_async_copy(
            xg_ref.at[pl.ds(aligned(first_row + sub * SUB_ROWS, SUB_ROWS), SUB_ROWS), :],
            stage_ref.at[slot], xsem.at[slot])

    def stage_to_rows(buf, sub, slot):
        rows = pl.ds(aligned(sub * SUB_ROWS, SUB_ROWS), SUB_ROWS)
        lo, hi = _unpack_bf16_pairs(stage_ref[slot])
        half = lo.shape[1]
        xb_ref[buf, rows, 0:half] = lo
        xb_ref[buf, rows, half:2 * half] = hi

    def wait_pending(slot):
        def body(i, c):
            y_copy(0, slot, 0, 0).wait()
            return c

        lax.fori_loop(0, pend_ref[slot], body, 0)
        pend_ref[slot] = 0

    @pl.when((g == 0) & (s == 0))
    def _():
        pend_ref[0] = 0
        pend_ref[1] = 0
        ybuf_ref[0, 0:SUB_ROWS, :] = jnp.zeros((SUB_ROWS, tn), F32)
        lo = tail_ref[0]
        hi = tail_ref[1]

        def start_body(i, c):
            for col in range(nn):
                y_copy(0, 0, col, i * SUB_ROWS).start()
            return c

        def wait_body(i, c):
            for col in range(nn):
                y_copy(0, 0, col, i * SUB_ROWS).wait()
            return c

        lax.fori_loop(lo, hi, start_body, 0)
        lax.fori_loop(lo, hi, wait_body, 0)

        x_copy(row0, 0, 0).start()

        def first_body(sub, c):
            slot = lax.rem(sub, 2)

            @pl.when(sub + 1 < ns)
            def _():
                x_copy(row0, sub + 1, 1 - slot).start()

            x_copy(row0, sub, slot).wait()
            stage_to_rows(0, sub, slot)
            return c

        lax.fori_loop(0, ns, first_body, 0)

    @pl.when(ns > 0)
    def _():
        @pl.when(s < nj)
        def _():
            bg = bg_ref[gexp_ref[g], pl.ds(s, 1), :]
            bu = bu_ref[gexp_ref[g], pl.ds(s, 1), :]
            cur = lax.rem(g, 2)

            last_g = pl.num_programs(0) - 1
            g_next = jnp.minimum(g + 1, last_g)
            ns_next = jnp.where(g < last_g, gns_ref[g_next], 0)
            row0_next = gsub_ref[g_next] * SUB_ROWS
            for k in range(2):
                @pl.when(2 * s + k < ns_next)
                def _():
                    x_copy(row0_next, 2 * s + k, k).start()

            def block(start, rows_n):
                rows = pl.ds(start, rows_n)
                x = xb_ref[cur, rows, :]
                gt = jnp.dot(x, wg_ref[0].astype(BF16), preferred_element_type=F32) + bg
                up = jnp.dot(x, wu_ref[0].astype(BF16), preferred_element_type=F32) + bu
                gt = jnp.minimum(gt, SWIGLU_LIMIT)
                up = jnp.clip(up, -SWIGLU_LIMIT, SWIGLU_LIMIT)
                hid = (up + 1.0) * gt * jax.nn.sigmoid(SWIGLU_ALPHA * gt)
                hid_ref[s, rows, :] = hid.astype(hid_ref.dtype)

            _for_row_blocks(ns, block)

            for k in range(2):
                @pl.when(2 * s + k < ns_next)
                def _():
                    x_copy(row0_next, 2 * s + k, k).wait()
                    stage_to_rows(1 - cur, 2 * s + k, k)

        @pl.when(s >= nj)
        def _():
            col = s - nj
            slot = lax.rem(col, 2)
            bd = bd_ref[gexp_ref[g], pl.ds(col, 1), :]
            wait_pending(slot)

            def block(start, rows_n):
                rows = pl.ds(start, rows_n)
                acc = jnp.dot(hid_ref[0, rows, :], wd_ref[0, 0:tf, :].astype(BF16),
                              preferred_element_type=F32)
                for c in range(1, nj):
                    acc = acc + jnp.dot(hid_ref[c, rows, :], wd_ref[0, c * tf:(c + 1) * tf, :].astype(BF16),
                                        preferred_element_type=F32)
                ybuf_ref[slot, rows, :] = acc + bd

            _for_row_blocks(ns, block)

            def start_body(sub, c):
                y_copy(sub, slot, col, row0).start()
                return c

            lax.fori_loop(0, ns, start_body, 0)
            pend_ref[slot] = ns

    @pl.when((g == pl.num_programs(0) - 1) & (s == nj + nn - 1))
    def _():
        wait_pending(0)
        wait_pending(1)


def _experts(gexp, gsub, gns, tail, xg, w_gate, w_up, b_gate, b_up, w_down, b_down, n_groups):
    rows = xg.shape[0]
    d = w_gate.shape[1]
    f = w_gate.shape[2]
    tf, tn = GATE_UP_COLS, DOWN_COLS
    nj, nn = f // tf, d // tn
    assert GROUP_SUBS <= 2 * nj, "next group's rows are fetched two sub-blocks per gate/up step"

    def up_idx(g, s, ge, gs, gn, tl):
        act = jnp.minimum(gn[g], 1)
        return (ge[g], 0, jnp.minimum(s, nj - 1) * act + (nj - 1) * (1 - act))

    def down_idx(g, s, ge, gs, gn, tl):
        act = jnp.minimum(gn[g], 1)
        return (ge[g], 0, jnp.maximum(s - nj, 0) * act + (nn - 1) * (1 - act))

    def whole(g, s, ge, gs, gn, tl):
        return (0, 0, 0)

    grid_spec = pltpu.PrefetchScalarGridSpec(
        num_scalar_prefetch=4,
        grid=(n_groups, nj + nn),
        in_specs=[
            pl.BlockSpec(memory_space=pl.ANY),
            pl.BlockSpec((1, d, tf), up_idx),
            pl.BlockSpec((1, d, tf), up_idx),
            pl.BlockSpec((N_EXPERTS, nj, tf), whole),
            pl.BlockSpec((N_EXPERTS, nj, tf), whole),
            pl.BlockSpec((1, f, tn), down_idx),
            pl.BlockSpec((N_EXPERTS, nn, tn), whole),
        ],
        out_specs=pl.BlockSpec(memory_space=pl.ANY),
        scratch_shapes=[pltpu.VMEM((2, GROUP_ROWS, d), BF16),
                        pltpu.VMEM((nj, GROUP_ROWS, tf), BF16),
                        pltpu.VMEM((2, SUB_ROWS, d // 2), xg.dtype),
                        pltpu.VMEM((2, GROUP_ROWS, tn), F32),
                        pltpu.SMEM((2,), jnp.int32),
                        pltpu.SemaphoreType.DMA((2,)),
                        pltpu.SemaphoreType.DMA((2,))],
    )
    return pl.pallas_call(
        functools.partial(_expert_kernel, nj=nj, nn=nn),
        grid_spec=grid_spec,
        out_shape=jax.ShapeDtypeStruct((rows, d), F32),
        compiler_params=_params(("arbitrary", "arbitrary")),
        name="experts",
    )(gexp, gsub, gns, tail, xg, w_gate, w_up, b_gate.reshape(N_EXPERTS, nj, tf),
      b_up.reshape(N_EXPERTS, nj, tf), w_down, b_down.reshape(N_EXPERTS, nn, tn))


def _combine_kernel(c0_ref, c1_ref, c2_ref, c3_ref, n0_ref, n1_ref, n2_ref, n3_ref,
                    route_ref, h_ref, g_ref, yg_ref, o_ref, buf_ref, sem, *, tt):
    dcur_refs = (c0_ref, c1_ref, c2_ref, c3_ref)
    dnext_refs = (n0_ref, n1_ref, n2_ref, n3_ref)
    i = pl.program_id(0)
    n_steps = pl.num_programs(0)
    slot = lax.rem(i, 2)

    def start_tile(drefs, sl):
        def body(t, c):
            for k in range(TOP_K):
                pltpu.make_async_copy(yg_ref.at[pl.ds(drefs[k][t], 1), :],
                                      buf_ref.at[sl, k, pl.ds(t, 1), :], sem.at[sl]).start()
            return c

        lax.fori_loop(0, tt, body, 0)

    @pl.when(i == 0)
    def _():
        start_tile(dcur_refs, 0)

    @pl.when(i + 1 < n_steps)
    def _():
        start_tile(dnext_refs, 1 - slot)

    def wait_body(t, c):
        for k in range(TOP_K):
            pltpu.make_async_copy(yg_ref.at[pl.ds(0, 1), :],
                                  buf_ref.at[slot, k, pl.ds(t, 1), :], sem.at[slot]).wait()
        return c

    lax.fori_loop(0, tt, wait_body, 0)

    d = o_ref.shape[1]
    cw = 2 * LANES
    gates = [route_ref[:, k:k + 1] for k in range(TOP_K)]
    ssq = jnp.zeros((tt, 1), F32)
    for cs in range(0, d, cw):
        acc = h_ref[:, cs:cs + cw]
        for k in range(TOP_K):
            acc = acc + buf_ref[slot, k, :, cs:cs + cw] * gates[k]
        o_ref[:, cs:cs + cw] = acc
        ssq = ssq + jnp.sum(acc * acc, axis=-1, keepdims=True)
    scale = lax.rsqrt(ssq * (1.0 / d) + NORM_EPS)
    for cs in range(0, d, cw):
        o_ref[:, cs:cs + cw] = o_ref[:, cs:cs + cw] * scale * g_ref[:, cs:cs + cw]


def _combine(dest_flat, route, h2, g, yg, tt):
    n, d = h2.shape
    steps = n // tt
    return pl.pallas_call(
        functools.partial(_combine_kernel, tt=tt),
        grid=(steps,),
        in_specs=_slot_specs(steps, tt, lambda i: i)
        + _slot_specs(steps, tt, lambda i: jnp.minimum(i + 1, steps - 1)) + [
                  pl.BlockSpec((tt, ROUTE_LANES), lambda i: (i, 0)),
                  pl.BlockSpec((tt, d), lambda i: (i, 0)),
                  pl.BlockSpec((1, d), lambda i: (0, 0)),
                  pl.BlockSpec(memory_space=pl.ANY)],
        out_specs=pl.BlockSpec((tt, d), lambda i: (i, 0)),
        out_shape=jax.ShapeDtypeStruct((n, d), F32),
        scratch_shapes=[pltpu.VMEM((2, TOP_K, tt, d), F32),
                        pltpu.SemaphoreType.DMA((2,))],
        compiler_params=_params(("arbitrary",)),
        name="combine",
    )(*([dest_flat] * (2 * TOP_K)), route, h2, g, yg)


def _retention_tables(dh):
    lg = jnp.log1p(-jnp.exp2(-5.0 - jnp.arange(RET_HEADS, dtype=F32)))
    i = jnp.arange(CHUNK, dtype=F32)
    rel = i[:, None] - i[None, :]
    causal = rel >= 0
    dmask = jnp.where(causal[None], jnp.exp(jnp.where(causal, rel, 0.0)[None] * lg[:, None, None]), 0.0)
    q_dec = jnp.exp((i + 1.0)[None, :] * lg[:, None])
    k_dec = jnp.exp((CHUNK - 1.0 - i)[None, :] * lg[:, None])
    c_dec = jnp.exp(CHUNK * lg)
    bcast = lambda t: jnp.broadcast_to(t[:, :, None], (RET_HEADS, CHUNK, LANES))
    return dmask, bcast(q_dec), bcast(k_dec), c_dec


def _rotary_tables(length, dh):
    half = dh // 2
    inv = ROPE_BASE ** (-jnp.arange(half, dtype=F32) / half)
    ang = jnp.arange(length, dtype=jnp.int32).astype(F32)[:, None] * inv[None, :]
    return jnp.cos(ang), jnp.sin(ang)


def kernel(x, meta_tokens, norm_mix, w_in, conv_w, w_out, norm_ffn, router_w, router_b,
           w_gate, b_gate, w_up, b_up, w_down, b_down, norm_final):
    bsz, seq, d = x.shape
    assert w_in.shape[0] == 1, "single-layer stack expected"
    assert meta_tokens.shape[0] == N_META and seq % CHUNK == 0
    r = d // 2
    dh = r // RET_HEADS
    assert dh // 2 == LANES
    n = bsz * seq
    x2d = x.reshape(n, d)

    a = _prenorm(x2d, norm_mix[0][None, :], NORM_ROWS)
    a_meta = _prenorm(meta_tokens.astype(x.dtype), norm_mix[0][None, :], N_META)
    p, pm = _inproj(a, a_meta, w_in[0], PROJ_ROWS, PROJ_COLS)

    cos, sin = _rotary_tables(N_META + seq, dh)
    pad_meta = lambda t: jnp.pad(t, ((CHUNK - N_META, 0), (0, 0)))
    dmask, qdec, kdec, cdec = _retention_tables(dh)
    s0, u0 = _meta_state(pad_meta(pm), pad_meta(cos[:N_META]), pad_meta(sin[:N_META]), kdec, r, dh)
    mix = _mixer(p, cos[N_META:], sin[N_META:], dmask, qdec, kdec, cdec, conv_w[0], s0, u0,
                 bsz, seq, r, dh)
    h2 = _outproj(mix, w_out[0], x2d, PROJ_ROWS, PROJ_COLS)

    rw = jnp.pad(router_w[0], ((0, 0), (0, ROUTE_LANES - N_EXPERTS))).astype(BF16)
    rb = jnp.pad(router_b[0].astype(F32), (0, ROUTE_LANES - N_EXPERTS), constant_values=-1e30)[None, :]
    m, route, route_t, cnt = _router(h2, norm_ffn[0][None, :], rw, rb, ROUTER_ROWS)

    side = route_t[TOP_K:3 * TOP_K].astype(jnp.int32)
    ids, rank = side[:TOP_K], side[TOP_K:]
    counts = cnt[0, :N_EXPERTS].astype(jnp.int32)

    total_sub = (n * TOP_K + N_EXPERTS * (SUB_ROWS - 1) + SUB_ROWS - 1) // SUB_ROWS
    psub = (counts + SUB_ROWS - 1) // SUB_ROWS
    pend = jnp.cumsum(psub) * SUB_ROWS
    pstart = pend - psub * SUB_ROWS
    first_row = jnp.zeros_like(ids)
    for e in range(N_EXPERTS):
        first_row = jnp.where(ids == e, pstart[e], first_row)
    dest_flat = (first_row + rank).reshape(-1).astype(jnp.int32)
    zlo = jnp.concatenate([pstart + counts, pend[-1:]]).astype(jnp.int32)
    zhi = jnp.concatenate([pend, jnp.full((1,), total_sub * SUB_ROWS)]).astype(jnp.int32)

    n_groups_max = N_EXPERTS + total_sub // GROUP_SUBS
    ngrp = (psub + GROUP_SUBS - 1) // GROUP_SUBS
    gend = jnp.cumsum(ngrp)
    gstart = gend - ngrp
    n_groups = gend[-1]
    gidx = jnp.arange(n_groups_max, dtype=jnp.int32)
    glast = jnp.minimum(gidx, n_groups - 1)
    gexp = jnp.minimum(jnp.searchsorted(gend, glast, side="right"), N_EXPERTS - 1).astype(jnp.int32)
    gq = glast - gstart[gexp]
    gsub = (pstart[gexp] // SUB_ROWS + gq * GROUP_SUBS).astype(jnp.int32)
    gns = jnp.where(gidx < n_groups, jnp.minimum(psub[gexp] - gq * GROUP_SUBS, GROUP_SUBS), 0).astype(jnp.int32)
    tail = jnp.stack([pend[-1] // SUB_ROWS, jnp.asarray(total_sub, pend.dtype)]).astype(jnp.int32)

    xg = _dispatch(dest_flat, zlo, zhi, m, total_sub * SUB_ROWS, DISPATCH_ROWS)
    yg = _experts(gexp, gsub, gns, tail, xg, w_gate[0], w_up[0], b_gate[0], b_up[0],
                  w_down[0], b_down[0], n_groups.astype(jnp.int32))
    out = _combine(dest_flat, route, h2, norm_final[None, :], yg, COMBINE_ROWS)
    return out.reshape(bsz, seq, d)
```

```python
import functools

import jax
import jax.numpy as jnp
from jax import lax
from jax.experimental import pallas as pl
from jax.experimental.pallas import tpu as pltpu

F32 = jnp.float32
BF16 = jnp.bfloat16

N_META = 16
RET_HEADS = 8
CHUNK = 128
CONV_K = 3
ROPE_BASE = 10000.0
N_EXPERTS = 32
TOP_K = 4
SWIGLU_LIMIT = 7.0
SWIGLU_ALPHA = 1.702
NORM_EPS = 1e-5
GN_EPS = 1e-6

LANES = 128
SUBLANES = 8
VMEM_LIMIT = 58 * 1024 * 1024

SUB_ROWS = 128
GROUP_SUBS = 9
GROUP_ROWS = GROUP_SUBS * SUB_ROWS
GATE_UP_COLS = 256
DOWN_COLS = 512
ROUTE_LANES = 128
NORM_ROWS = 512
PROJ_ROWS = 512
PROJ_COLS = 1024
ROUTER_ROWS = 256
DISPATCH_ROWS = 256
COMBINE_ROWS = 128


def _params(sem, vmem=VMEM_LIMIT):
    return pltpu.CompilerParams(dimension_semantics=sem, vmem_limit_bytes=vmem)


def _prenorm_kernel(x_ref, g_ref, o_ref):
    x = x_ref[...]
    ms = jnp.mean(x * x, axis=-1, keepdims=True)
    o_ref[...] = (x * lax.rsqrt(ms + NORM_EPS) * g_ref[...]).astype(o_ref.dtype)


def _prenorm(x2d, g, tm):
    n, d = x2d.shape
    return pl.pallas_call(
        _prenorm_kernel,
        grid=(n // tm,),
        in_specs=[pl.BlockSpec((tm, d), lambda i: (i, 0)),
                  pl.BlockSpec((1, d), lambda i: (0, 0))],
        out_specs=pl.BlockSpec((tm, d), lambda i: (i, 0)),
        out_shape=jax.ShapeDtypeStruct((n, d), BF16),
        compiler_params=_params(("arbitrary",)),
        name="prenorm",
    )(x2d, g)


def _inproj_kernel(a_ref, am_ref, w_ref, o_ref, om_ref):
    @pl.when(pl.program_id(1) == 0)
    def _():
        om_ref[...] = jnp.dot(am_ref[...], w_ref[...].astype(BF16), preferred_element_type=F32)

    o_ref[...] = jnp.dot(a_ref[...], w_ref[...].astype(BF16), preferred_element_type=F32)


def _inproj(a, a_meta, w, tm, tn):
    n, d = a.shape
    c = w.shape[1]
    nm = a_meta.shape[0]
    return pl.pallas_call(
        _inproj_kernel,
        grid=(c // tn, n // tm),
        in_specs=[pl.BlockSpec((tm, d), lambda j, i: (i, 0)),
                  pl.BlockSpec((nm, d), lambda j, i: (0, 0)),
                  pl.BlockSpec((d, tn), lambda j, i: (0, j))],
        out_specs=[pl.BlockSpec((tm, tn), lambda j, i: (i, j)),
                   pl.BlockSpec((nm, tn), lambda j, i: (0, j))],
        out_shape=[jax.ShapeDtypeStruct((n, c), F32),
                   jax.ShapeDtypeStruct((nm, c), F32)],
        compiler_params=_params(("arbitrary", "arbitrary")),
        name="inproj",
    )(a, a_meta, w)


def _rotary_halves(ref, h, dh, cos, sin):
    half = dh // 2
    t1 = ref[:, h * dh:h * dh + half]
    t2 = ref[:, h * dh + half:(h + 1) * dh]
    return t1 * cos - t2 * sin, t2 * cos + t1 * sin


def _state_update(k1, k2, kdec, v_bf):
    kd = jnp.concatenate([k1 * kdec, k2 * kdec], axis=1).astype(BF16)
    return lax.dot_general(kd, v_bf, (((0,), (0,)), ((), ())), preferred_element_type=F32)


def _meta_kernel(k_ref, v_ref, cc_ref, ch_ref, cos_ref, sin_ref, kdec_ref, s0_ref, u0_ref, *, dh):
    cos = cos_ref[...]
    sin = sin_ref[...]
    scale = dh ** -0.5
    for h in range(RET_HEADS):
        k1, k2 = _rotary_halves(k_ref, h, dh, cos, sin)
        v_bf = v_ref[:, h * dh:(h + 1) * dh].astype(BF16)
        s0_ref[h] = _state_update(k1 * scale, k2 * scale, kdec_ref[h], v_bf)
    u0_ref[...] = cc_ref[CHUNK - SUBLANES:CHUNK, :] * ch_ref[CHUNK - SUBLANES:CHUNK, :]


def _meta_state(pm_pad, cosm, sinm, kdec, r, dh):
    col = lambda cb: pl.BlockSpec((CHUNK, r), lambda i: (0, cb))
    full2 = pl.BlockSpec((CHUNK, LANES), lambda i: (0, 0))
    return pl.pallas_call(
        functools.partial(_meta_kernel, dh=dh),
        grid=(1,),
        in_specs=[col(1), col(2), col(5), col(6), full2, full2,
                  pl.BlockSpec((RET_HEADS, CHUNK, LANES), lambda i: (0, 0, 0))],
        out_specs=[pl.BlockSpec((RET_HEADS, dh, dh), lambda i: (0, 0, 0)),
                   pl.BlockSpec((SUBLANES, r), lambda i: (0, 0))],
        out_shape=[jax.ShapeDtypeStruct((RET_HEADS, dh, dh), F32),
                   jax.ShapeDtypeStruct((SUBLANES, r), F32)],
        compiler_params=_params(("arbitrary",)),
        name="meta_state",
    )(pm_pad, pm_pad, pm_pad, pm_pad, cosm, sinm, kdec)


def _mixer_kernel(cdec_ref, q_ref, k_ref, v_ref, g_ref, cb_ref, cc_ref, ch_ref,
                  cos_ref, sin_ref, dmask_ref, qdec_ref, kdec_ref, cw_ref, s0_ref, u0_ref,
                  mix_ref, state_ref, uext_ref, *, dh, r):
    @pl.when(pl.program_id(1) == 0)
    def _():
        state_ref[...] = s0_ref[...]
        uext_ref[0:SUBLANES, :] = u0_ref[...]

    cos = cos_ref[...]
    sin = sin_ref[...]
    scale = dh ** -0.5
    for h in range(RET_HEADS):
        q1, q2 = _rotary_halves(q_ref, h, dh, cos, sin)
        k1, k2 = _rotary_halves(k_ref, h, dh, cos, sin)
        k1 = k1 * scale
        k2 = k2 * scale
        qb = jnp.concatenate([q1, q2], axis=1).astype(BF16)
        kb = jnp.concatenate([k1, k2], axis=1).astype(BF16)
        v_bf = v_ref[:, h * dh:(h + 1) * dh].astype(BF16)
        scores = lax.dot_general(qb, kb, (((1,), (1,)), ((), ())), preferred_element_type=F32)
        sm = (scores * dmask_ref[h]).astype(BF16)
        inner = jnp.dot(sm, v_bf, preferred_element_type=F32)
        qdec = qdec_ref[h]
        qd = jnp.concatenate([q1 * qdec, q2 * qdec], axis=1).astype(BF16)
        st = state_ref[h]
        cross = jnp.dot(qd, st.astype(BF16), preferred_element_type=F32)
        state_ref[h] = st * cdec_ref[h] + _state_update(k1, k2, kdec_ref[h], v_bf)
        y = inner + cross
        mu = jnp.mean(y, axis=-1, keepdims=True)
        yc = y - mu
        var = jnp.mean(yc * yc, axis=-1, keepdims=True)
        yn = yc * lax.rsqrt(var + GN_EPS)
        gh = g_ref[:, h * dh:(h + 1) * dh]
        mix_ref[:, h * dh:(h + 1) * dh] = (gh * jax.nn.sigmoid(gh) * yn).astype(mix_ref.dtype)

    cw = 512
    for cs in range(0, r, cw):
        sl = slice(cs, cs + cw)
        u = cc_ref[:, sl] * ch_ref[:, sl]
        uext_ref[SUBLANES:SUBLANES + CHUNK, sl] = u
        u1 = uext_ref[SUBLANES - 1:SUBLANES - 1 + CHUNK, sl]
        u2 = uext_ref[SUBLANES - 2:SUBLANES - 2 + CHUNK, sl]
        conv = cw_ref[0:1, sl] * u2 + cw_ref[1:2, sl] * u1 + cw_ref[2:3, sl] * u
        mix_ref[:, r + cs:r + cs + cw] = (cb_ref[:, sl] * conv).astype(mix_ref.dtype)
        uext_ref[0:SUBLANES, sl] = uext_ref[CHUNK:CHUNK + SUBLANES, sl]


def _mixer(p, cos, sin, dmask, qdec, kdec, cdec, conv_w, s0, u0, bsz, seq, r, dh):
    nc = seq // CHUNK
    d = 2 * r
    col = lambda cb: pl.BlockSpec((CHUNK, r), lambda b, c: (b * nc + c, cb))
    tab = pl.BlockSpec((CHUNK, LANES), lambda b, c: (c, 0))
    hconst = pl.BlockSpec((RET_HEADS, CHUNK, LANES), lambda b, c: (0, 0, 0))
    return pl.pallas_call(
        functools.partial(_mixer_kernel, dh=dh, r=r),
        grid=(bsz, nc),
        in_specs=[pl.BlockSpec(memory_space=pltpu.SMEM),
                  col(0), col(1), col(2), col(3), col(4), col(5), col(6),
                  tab, tab, hconst, hconst, hconst,
                  pl.BlockSpec((CONV_K, r), lambda b, c: (0, 0)),
                  pl.BlockSpec((RET_HEADS, dh, dh), lambda b, c: (0, 0, 0)),
                  pl.BlockSpec((SUBLANES, r), lambda b, c: (0, 0))],
        out_specs=pl.BlockSpec((CHUNK, d), lambda b, c: (b * nc + c, 0)),
        out_shape=jax.ShapeDtypeStruct((bsz * seq, d), BF16),
        scratch_shapes=[pltpu.VMEM((RET_HEADS, dh, dh), F32),
                        pltpu.VMEM((CHUNK + 2 * SUBLANES, r), F32)],
        compiler_params=_params(("arbitrary", "arbitrary")),
        name="mixer",
    )(cdec, p, p, p, p, p, p, p, cos, sin, dmask, qdec, kdec, conv_w, s0, u0)


def _outproj_kernel(a_ref, w_ref, x_ref, o_ref):
    o_ref[...] = x_ref[...] + jnp.dot(a_ref[...], w_ref[...].astype(BF16), preferred_element_type=F32)


def _outproj(mix, w, x2d, tm, tn):
    n, d = mix.shape
    c = w.shape[1]
    return pl.pallas_call(
        _outproj_kernel,
        grid=(c // tn, n // tm),
        in_specs=[pl.BlockSpec((tm, d), lambda j, i: (i, 0)),
                  pl.BlockSpec((d, tn), lambda j, i: (0, j)),
                  pl.BlockSpec((tm, tn), lambda j, i: (i, j))],
        out_specs=pl.BlockSpec((tm, tn), lambda j, i: (i, j)),
        out_shape=jax.ShapeDtypeStruct((n, c), F32),
        compiler_params=_params(("arbitrary", "arbitrary")),
        name="outproj",
    )(mix, w, x2d)


def _pack_bf16_pairs(x):
    half = x.shape[1] // 2
    bits = lax.bitcast_convert_type(x.astype(BF16).astype(F32), jnp.uint32)
    return bits[:, half:] | (bits[:, :half] >> 16)


def _unpack_bf16_pairs(w):
    lo = lax.bitcast_convert_type(w << 16, F32)
    hi = lax.bitcast_convert_type(w & jnp.uint32(0xFFFF0000), F32)
    return lo.astype(BF16), hi.astype(BF16)


def _router_kernel(h_ref, g_ref, w_ref, b_ref, m_ref, route_ref, cnt_ref, carry_ref, *, tr):
    @pl.when(pl.program_id(0) == 0)
    def _():
        carry_ref[...] = jnp.zeros_like(carry_ref)

    x = h_ref[...]
    ms = jnp.mean(x * x, axis=-1, keepdims=True)
    m = x * lax.rsqrt(ms + NORM_EPS) * g_ref[...]
    m_ref[...] = _pack_bf16_pairs(m)

    logits = jnp.dot(m.astype(BF16), w_ref[...], preferred_element_type=F32) + b_ref[...]

    lane = lax.broadcasted_iota(jnp.int32, (tr, ROUTE_LANES), 1)
    lane_f = lane.astype(F32)
    work = logits
    vals, onehots = [], []
    for _ in range(TOP_K):
        mx = jnp.max(work, axis=-1, keepdims=True)
        idx = jnp.min(jnp.where(work == mx, lane_f, float(ROUTE_LANES)), axis=-1, keepdims=True)
        oh = lane_f == idx
        vals.append(mx)
        onehots.append(oh)
        work = jnp.where(oh, -jnp.inf, work)

    exps = [jnp.exp(v - vals[0]) for v in vals]
    denom = exps[0] + exps[1] + exps[2] + exps[3]
    gates = [e / denom for e in exps]

    chosen = onehots[0] | onehots[1] | onehots[2] | onehots[3]
    cmat = jnp.where(chosen, 1.0, 0.0).astype(BF16)
    row = lax.broadcasted_iota(jnp.int32, (tr, tr), 0)
    colm = lax.broadcasted_iota(jnp.int32, (tr, tr), 1)
    lower = jnp.where(colm < row, 1.0, 0.0).astype(BF16)
    carry = carry_ref[0:1, :]
    rank_e = jnp.dot(lower, cmat, preferred_element_type=F32) + carry
    new_carry = carry + jnp.sum(cmat.astype(F32), axis=0, keepdims=True)
    carry_ref[...] = jnp.broadcast_to(new_carry, carry_ref.shape)
    cnt_ref[...] = jnp.broadcast_to(new_carry, cnt_ref.shape)

    out = jnp.zeros((tr, ROUTE_LANES), F32)
    for k in range(TOP_K):
        idx_k = jnp.sum(jnp.where(onehots[k], lane_f, 0.0), axis=-1, keepdims=True)
        rank_k = jnp.sum(jnp.where(onehots[k], rank_e, 0.0), axis=-1, keepdims=True)
        out = jnp.where(lane == k, gates[k], out)
        out = jnp.where(lane == TOP_K + k, idx_k, out)
        out = jnp.where(lane == 2 * TOP_K + k, rank_k, out)
    route_ref[...] = out


def _router(h2, g, w, b, tr):
    n, d = h2.shape
    return pl.pallas_call(
        functools.partial(_router_kernel, tr=tr),
        grid=(n // tr,),
        in_specs=[pl.BlockSpec((tr, d), lambda i: (i, 0)),
                  pl.BlockSpec((1, d), lambda i: (0, 0)),
                  pl.BlockSpec((d, ROUTE_LANES), lambda i: (0, 0)),
                  pl.BlockSpec((1, ROUTE_LANES), lambda i: (0, 0))],
        out_specs=[pl.BlockSpec((tr, d // 2), lambda i: (i, 0)),
                   pl.BlockSpec((tr, ROUTE_LANES), lambda i: (i, 0)),
                   pl.BlockSpec((SUBLANES, ROUTE_LANES), lambda i: (0, 0))],
        out_shape=[jax.ShapeDtypeStruct((n, d // 2), jnp.uint32),
                   jax.ShapeDtypeStruct((n, ROUTE_LANES), F32),
                   jax.ShapeDtypeStruct((SUBLANES, ROUTE_LANES), F32)],
        scratch_shapes=[pltpu.VMEM((SUBLANES, ROUTE_LANES), F32)],
        compiler_params=_params(("arbitrary",)),
        name="router",
    )(h2, g, w, b)


def _row_copy(src_ref, src_row, dst_ref, dst_row, sem):
    return pltpu.make_async_copy(src_ref.at[pl.ds(src_row, 1), :], dst_ref.at[pl.ds(dst_row, 1), :], sem)


def _dispatch_kernel(dest_ref, zlo_ref, zhi_ref, m_ref, xg_ref, zero_ref, sem, zsem, *, tt):
    @pl.when(pl.program_id(0) == 0)
    def _():
        zero_ref[...] = jnp.zeros_like(zero_ref)

        def range_body(e, carry):
            def zero_start(rr, c):
                _row_copy(zero_ref, 0, xg_ref, rr, zsem).start()
                return c

            lax.fori_loop(zlo_ref[e], zhi_ref[e], zero_start, 0)
            return carry

        lax.fori_loop(0, N_EXPERTS + 1, range_body, 0)

    @pl.when(pl.program_id(0) == pl.num_programs(0) - 1)
    def _():
        def range_body(e, carry):
            def zero_wait(rr, c):
                _row_copy(zero_ref, 0, xg_ref, rr, zsem).wait()
                return c

            lax.fori_loop(zlo_ref[e], zhi_ref[e], zero_wait, 0)
            return carry

        lax.fori_loop(0, N_EXPERTS + 1, range_body, 0)

    def start_body(t, c):
        for k in range(TOP_K):
            _row_copy(m_ref, t, xg_ref, dest_ref[t * TOP_K + k], sem).start()
        return c

    lax.fori_loop(0, tt, start_body, 0)
    for k in range(TOP_K):
        pltpu.make_async_copy(m_ref, xg_ref.at[pl.ds(0, tt), :], sem).wait()


def _dispatch(dest_flat, zlo, zhi, m, rows, tt):
    n, d = m.shape
    return pl.pallas_call(
        functools.partial(_dispatch_kernel, tt=tt),
        grid=(n // tt,),
        in_specs=[pl.BlockSpec((tt * TOP_K,), lambda i: (i,), memory_space=pltpu.SMEM),
                  pl.BlockSpec(memory_space=pltpu.SMEM),
                  pl.BlockSpec(memory_space=pltpu.SMEM),
                  pl.BlockSpec((tt, d), lambda i: (i, 0))],
        out_specs=pl.BlockSpec(memory_space=pl.ANY),
        out_shape=jax.ShapeDtypeStruct((rows, d), m.dtype),
        scratch_shapes=[pltpu.VMEM((SUBLANES, d), m.dtype),
                        pltpu.SemaphoreType.DMA(()),
                        pltpu.SemaphoreType.DMA(())],
        compiler_params=_params(("arbitrary",)),
        name="dispatch",
    )(dest_flat, zlo, zhi, m)


def _for_row_blocks(nsub, block_fn):
    n8 = nsub // 8
    rem8 = nsub - n8 * 8
    merged = jnp.where((rem8 == 1) & (n8 >= 1), 1, 0)

    def body(i, c):
        start = pl.multiple_of(i * (8 * SUB_ROWS), 8 * SUB_ROWS)
        block_fn(start, 4 * SUB_ROWS)
        block_fn(start + 4 * SUB_ROWS, 4 * SUB_ROWS)
        return c

    lax.fori_loop(0, n8 - merged, body, 0)

    @pl.when(merged == 1)
    def _():
        start = pl.multiple_of((n8 - 1) * (8 * SUB_ROWS), 8 * SUB_ROWS)
        block_fn(start, 4 * SUB_ROWS)
        block_fn(start + 4 * SUB_ROWS, 5 * SUB_ROWS)

    rem8 = rem8 * (1 - merged)
    base8 = n8 * (8 * SUB_ROWS)

    @pl.when(rem8 >= 4)
    def _():
        block_fn(pl.multiple_of(base8, 4 * SUB_ROWS), 4 * SUB_ROWS)

    rem = rem8 % 4
    base = base8 + (rem8 // 4) * (4 * SUB_ROWS)

    @pl.when(rem >= 2)
    def _():
        block_fn(pl.multiple_of(base, SUB_ROWS), 2 * SUB_ROWS)

    @pl.when(rem % 2 == 1)
    def _():
        block_fn(pl.multiple_of(base + (rem // 2) * (2 * SUB_ROWS), SUB_ROWS), SUB_ROWS)


def _expert_kernel(gexp_ref, gsub_ref, gns_ref, tail_ref,
                   xg_ref, wg_ref, wu_ref, bg_ref, bu_ref, wd_ref, bd_ref, yg_ref,
                   xb_ref, hid_ref, stage_ref, ybuf_ref, pend_ref,
                   xsem, ysem, *, nj, nn):
    g = pl.program_id(0)
    s = pl.program_id(1)
    tf = GATE_UP_COLS
    tn = DOWN_COLS
    ns = gns_ref[g]
    row0 = gsub_ref[g] * SUB_ROWS

    def aligned(v, m):
        return v if isinstance(v, int) else pl.multiple_of(v, m)

    def y_copy(sub, slot, col, first_row):
        return pltpu.make_async_copy(
            ybuf_ref.at[slot, pl.ds(aligned(sub * SUB_ROWS, SUB_ROWS), SUB_ROWS), :],
            yg_ref.at[pl.ds(aligned(first_row + sub * SUB_ROWS, SUB_ROWS), SUB_ROWS),
                      pl.ds(aligned(col * tn, tn), tn)],
            ysem.at[slot])

    def x_copy(first_row, sub, slot):
        return pltpu.make_async_copy(
            xg_ref.at[pl.ds(aligned(first_row + sub * SUB_ROWS, SUB_ROWS), SUB_ROWS), :],
            stage_ref.at[slot], xsem.at[slot])

    def stage_to_rows(buf, sub, slot):
        rows = pl.ds(aligned(sub * SUB_ROWS, SUB_ROWS), SUB_ROWS)
        lo, hi = _unpack_bf16_pairs(stage_ref[slot])
        half = lo.shape[1]
        xb_ref[buf, rows, 0:half] = lo
        xb_ref[buf, rows, half:2 * half] = hi

    def wait_pending(slot):
        def body(i, c):
            y_copy(0, slot, 0, 0).wait()
            return c

        lax.fori_loop(0, pend_ref[slot], body, 0)
        pend_ref[slot] = 0

    @pl.when((g == 0) & (s == 0))
    def _():
        pend_ref[0] = 0
        pend_ref[1] = 0
        ybuf_ref[0, 0:SUB_ROWS, :] = jnp.zeros((SUB_ROWS, tn), F32)
        lo = tail_ref[0]
        hi = tail_ref[1]

        def start_body(i, c):
            for col in range(nn):
                y_copy(0, 0, col, i * SUB_ROWS).start()
            return c

        def wait_body(i, c):
            for col in range(nn):
                y_copy(0, 0, col, i * SUB_ROWS).wait()
            return c

        lax.fori_loop(lo, hi, start_body, 0)
        lax.fori_loop(lo, hi, wait_body, 0)

        x_copy(row0, 0, 0).start()

        def first_body(sub, c):
            slot = lax.rem(sub, 2)

            @pl.when(sub + 1 < ns)
            def _():
                x_copy(row0, sub + 1, 1 - slot).start()

            x_copy(row0, sub, slot).wait()
            stage_to_rows(0, sub, slot)
            return c

        lax.fori_loop(0, ns, first_body, 0)

    @pl.when(ns > 0)
    def _():
        @pl.when(s < nj)
        def _():
            bg = bg_ref[gexp_ref[g], pl.ds(s, 1), :]
            bu = bu_ref[gexp_ref[g], pl.ds(s, 1), :]
            cur = lax.rem(g, 2)

            last_g = pl.num_programs(0) - 1
            g_next = jnp.minimum(g + 1, last_g)
            ns_next = jnp.where(g < last_g, gns_ref[g_next], 0)
            row0_next = gsub_ref[g_next] * SUB_ROWS
            for k in range(2):
                @pl.when(2 * s + k < ns_next)
                def _():
                    x_copy(row0_next, 2 * s + k, k).start()

            def block(start, rows_n):
                rows = pl.ds(start, rows_n)
                x = xb_ref[cur, rows, :]
                gt = jnp.dot(x, wg_ref[0].astype(BF16), preferred_element_type=F32) + bg
                up = jnp.dot(x, wu_ref[0].astype(BF16), preferred_element_type=F32) + bu
                gt = jnp.minimum(gt, SWIGLU_LIMIT)
                up = jnp.clip(up, -SWIGLU_LIMIT, SWIGLU_LIMIT)
                hid = (up + 1.0) * gt * jax.nn.sigmoid(SWIGLU_ALPHA * gt)
                hid_ref[s, rows, :] = hid.astype(hid_ref.dtype)

            _for_row_blocks(ns, block)

            for k in range(2):
                @pl.when(2 * s + k < ns_next)
                def _():
                    x_copy(row0_next, 2 * s + k, k).wait()
                    stage_to_rows(1 - cur, 2 * s + k, k)

        @pl.when(s >= nj)
        def _():
            col = s - nj
            slot = lax.rem(col, 2)
            bd = bd_ref[gexp_ref[g], pl.ds(col, 1), :]
            wait_pending(slot)

            def block(start, rows_n):
                rows = pl.ds(start, rows_n)
                acc = jnp.dot(hid_ref[0, rows, :], wd_ref[0, 0:tf, :].astype(BF16),
                              preferred_element_type=F32)
                for c in range(1, nj):
                    acc = acc + jnp.dot(hid_ref[c, rows, :], wd_ref[0, c * tf:(c + 1) * tf, :].astype(BF16),
                                        preferred_element_type=F32)
                ybuf_ref[slot, rows, :] = acc + bd

            _for_row_blocks(ns, block)

            def start_body(sub, c):
                y_copy(sub, slot, col, row0).start()
                return c

            lax.fori_loop(0, ns, start_body, 0)
            pend_ref[slot] = ns

    @pl.when((g == pl.num_programs(0) - 1) & (s == nj + nn - 1))
    def _():
        wait_pending(0)
        wait_pending(1)


def _experts(gexp, gsub, gns, tail, xg, w_gate, w_up, b_gate, b_up, w_down, b_down, n_groups):
    rows = xg.shape[0]
    d = w_gate.shape[1]
    f = w_gate.shape[2]
    tf, tn = GATE_UP_COLS, DOWN_COLS
    nj, nn = f // tf, d // tn
    assert GROUP_SUBS <= 2 * nj, "next group's rows are fetched two sub-blocks per gate/up step"

    def up_idx(g, s, ge, gs, gn, tl):
        act = jnp.minimum(gn[g], 1)
        return (ge[g], 0, jnp.minimum(s, nj - 1) * act + (nj - 1) * (1 - act))

    def down_idx(g, s, ge, gs, gn, tl):
        act = jnp.minimum(gn[g], 1)
        return (ge[g], 0, jnp.maximum(s - nj, 0) * act + (nn - 1) * (1 - act))

    def whole(g, s, ge, gs, gn, tl):
        return (0, 0, 0)

    grid_spec = pltpu.PrefetchScalarGridSpec(
        num_scalar_prefetch=4,
        grid=(n_groups, nj + nn),
        in_specs=[
            pl.BlockSpec(memory_space=pl.ANY),
            pl.BlockSpec((1, d, tf), up_idx),
            pl.BlockSpec((1, d, tf), up_idx),
            pl.BlockSpec((N_EXPERTS, nj, tf), whole),
            pl.BlockSpec((N_EXPERTS, nj, tf), whole),
            pl.BlockSpec((1, f, tn), down_idx),
            pl.BlockSpec((N_EXPERTS, nn, tn), whole),
        ],
        out_specs=pl.BlockSpec(memory_space=pl.ANY),
        scratch_shapes=[pltpu.VMEM((2, GROUP_ROWS, d), BF16),
                        pltpu.VMEM((nj, GROUP_ROWS, tf), BF16),
                        pltpu.VMEM((2, SUB_ROWS, d // 2), xg.dtype),
                        pltpu.VMEM((2, GROUP_ROWS, tn), F32),
                        pltpu.SMEM((2,), jnp.int32),
                        pltpu.SemaphoreType.DMA((2,)),
                        pltpu.SemaphoreType.DMA((2,))],
    )
    return pl.pallas_call(
        functools.partial(_expert_kernel, nj=nj, nn=nn),
        grid_spec=grid_spec,
        out_shape=jax.ShapeDtypeStruct((rows, d), F32),
        compiler_params=_params(("arbitrary", "arbitrary")),
        name="experts",
    )(gexp, gsub, gns, tail, xg, w_gate, w_up, b_gate.reshape(N_EXPERTS, nj, tf),
      b_up.reshape(N_EXPERTS, nj, tf), w_down, b_down.reshape(N_EXPERTS, nn, tn))


def _combine_kernel(dcur_ref, dnext_ref, route_ref, h_ref, g_ref, yg_ref, o_ref, buf_ref, sem, *, tt):
    i = pl.program_id(0)
    n_steps = pl.num_programs(0)
    slot = lax.rem(i, 2)

    def start_tile(dref, sl):
        def body(t, c):
            for k in range(TOP_K):
                pltpu.make_async_copy(yg_ref.at[pl.ds(dref[t * TOP_K + k], 1), :],
                                      buf_ref.at[sl, k, pl.ds(t, 1), :], sem.at[sl]).start()
            return c

        lax.fori_loop(0, tt, body, 0)

    @pl.when(i == 0)
    def _():
        start_tile(dcur_ref, 0)

    @pl.when(i + 1 < n_steps)
    def _():
        start_tile(dnext_ref, 1 - slot)

    for k in range(TOP_K):
        pltpu.make_async_copy(yg_ref.at[pl.ds(0, tt), :], buf_ref.at[slot, k], sem.at[slot]).wait()

    d = o_ref.shape[1]
    cw = 2 * LANES
    gates = [route_ref[:, k:k + 1] for k in range(TOP_K)]
    ssq = jnp.zeros((tt, 1), F32)
    for cs in range(0, d, cw):
        acc = h_ref[:, cs:cs + cw]
        for k in range(TOP_K):
            acc = acc + buf_ref[slot, k, :, cs:cs + cw] * gates[k]
        o_ref[:, cs:cs + cw] = acc
        ssq = ssq + jnp.sum(acc * acc, axis=-1, keepdims=True)
    scale = lax.rsqrt(ssq * (1.0 / d) + NORM_EPS)
    for cs in range(0, d, cw):
        o_ref[:, cs:cs + cw] = o_ref[:, cs:cs + cw] * scale * g_ref[:, cs:cs + cw]


def _combine(dest_flat, route, h2, g, yg, tt):
    n, d = h2.shape
    steps = n // tt
    return pl.pallas_call(
        functools.partial(_combine_kernel, tt=tt),
        grid=(steps,),
        in_specs=[pl.BlockSpec((tt * TOP_K,), lambda i: (i,), memory_space=pltpu.SMEM),
                  pl.BlockSpec((tt * TOP_K,), lambda i: (jnp.minimum(i + 1, steps - 1),),
                               memory_space=pltpu.SMEM),
                  pl.BlockSpec((tt, ROUTE_LANES), lambda i: (i, 0)),
                  pl.BlockSpec((tt, d), lambda i: (i, 0)),
                  pl.BlockSpec((1, d), lambda i: (0, 0)),
                  pl.BlockSpec(memory_space=pl.ANY)],
        out_specs=pl.BlockSpec((tt, d), lambda i: (i, 0)),
        out_shape=jax.ShapeDtypeStruct((n, d), F32),
        scratch_shapes=[pltpu.VMEM((2, TOP_K, tt, d), F32),
                        pltpu.SemaphoreType.DMA((2,))],
        compiler_params=_params(("arbitrary",)),
        name="combine",
    )(dest_flat, dest_flat, route, h2, g, yg)


def _retention_tables(dh):
    lg = jnp.log1p(-jnp.exp2(-5.0 - jnp.arange(RET_HEADS, dtype=F32)))
    i = jnp.arange(CHUNK, dtype=F32)
    rel = i[:, None] - i[None, :]
    causal = rel >= 0
    dmask = jnp.where(causal[None], jnp.exp(jnp.where(causal, rel, 0.0)[None] * lg[:, None, None]), 0.0)
    q_dec = jnp.exp((i + 1.0)[None, :] * lg[:, None])
    k_dec = jnp.exp((CHUNK - 1.0 - i)[None, :] * lg[:, None])
    c_dec = jnp.exp(CHUNK * lg)
    bcast = lambda t: jnp.broadcast_to(t[:, :, None], (RET_HEADS, CHUNK, LANES))
    return dmask, bcast(q_dec), bcast(k_dec), c_dec


def _rotary_tables(length, dh):
    half = dh // 2
    inv = ROPE_BASE ** (-jnp.arange(half, dtype=F32) / half)
    ang = jnp.arange(length, dtype=jnp.int32).astype(F32)[:, None] * inv[None, :]
    return jnp.cos(ang), jnp.sin(ang)


def kernel(x, meta_tokens, norm_mix, w_in, conv_w, w_out, norm_ffn, router_w, router_b,
           w_gate, b_gate, w_up, b_up, w_down, b_down, norm_final):
    bsz, seq, d = x.shape
    assert w_in.shape[0] == 1, "single-layer stack expected"
    assert meta_tokens.shape[0] == N_META and seq % CHUNK == 0
    r = d // 2
    dh = r // RET_HEADS
    assert dh // 2 == LANES
    n = bsz * seq
    x2d = x.reshape(n, d)

    a = _prenorm(x2d, norm_mix[0][None, :], NORM_ROWS)
    a_meta = _prenorm(meta_tokens.astype(x.dtype), norm_mix[0][None, :], N_META)
    p, pm = _inproj(a, a_meta, w_in[0], PROJ_ROWS, PROJ_COLS)

    cos, sin = _rotary_tables(N_META + seq, dh)
    pad_meta = lambda t: jnp.pad(t, ((CHUNK - N_META, 0), (0, 0)))
    dmask, qdec, kdec, cdec = _retention_tables(dh)
    s0, u0 = _meta_state(pad_meta(pm), pad_meta(cos[:N_META]), pad_meta(sin[:N_META]), kdec, r, dh)
    mix = _mixer(p, cos[N_META:], sin[N_META:], dmask, qdec, kdec, cdec, conv_w[0], s0, u0,
                 bsz, seq, r, dh)
    h2 = _outproj(mix, w_out[0], x2d, PROJ_ROWS, PROJ_COLS)

    rw = jnp.pad(router_w[0], ((0, 0), (0, ROUTE_LANES - N_EXPERTS))).astype(BF16)
    rb = jnp.pad(router_b[0].astype(F32), (0, ROUTE_LANES - N_EXPERTS), constant_values=-1e30)[None, :]
    m, route, cnt = _router(h2, norm_ffn[0][None, :], rw, rb, ROUTER_ROWS)

    side = route[:, TOP_K:3 * TOP_K].T.astype(jnp.int32)
    ids, rank = side[:TOP_K], side[TOP_K:]
    counts = cnt[0, :N_EXPERTS].astype(jnp.int32)

    total_sub = (n * TOP_K + N_EXPERTS * (SUB_ROWS - 1) + SUB_ROWS - 1) // SUB_ROWS
    psub = (counts + SUB_ROWS - 1) // SUB_ROWS
    pend = jnp.cumsum(psub) * SUB_ROWS
    pstart = pend - psub * SUB_ROWS
    first_row = jnp.zeros_like(ids)
    for e in range(N_EXPERTS):
        first_row = jnp.where(ids == e, pstart[e], first_row)
    dest_flat = (first_row + rank).T.reshape(-1).astype(jnp.int32)
    zlo = jnp.concatenate([pstart + counts, pend[-1:]]).astype(jnp.int32)
    zhi = jnp.concatenate([pend, jnp.full((1,), total_sub * SUB_ROWS)]).astype(jnp.int32)

    n_groups_max = N_EXPERTS + total_sub // GROUP_SUBS
    ngrp = (psub + GROUP_SUBS - 1) // GROUP_SUBS
    gend = jnp.cumsum(ngrp)
    gstart = gend - ngrp
    n_groups = gend[-1]
    gidx = jnp.arange(n_groups_max, dtype=jnp.int32)
    glast = jnp.minimum(gidx, n_groups - 1)
    gexp = jnp.minimum(jnp.searchsorted(gend, glast, side="right"), N_EXPERTS - 1).astype(jnp.int32)
    gq = glast - gstart[gexp]
    gsub = (pstart[gexp] // SUB_ROWS + gq * GROUP_SUBS).astype(jnp.int32)
    gns = jnp.where(gidx < n_groups, jnp.minimum(psub[gexp] - gq * GROUP_SUBS, GROUP_SUBS), 0).astype(jnp.int32)
    tail = jnp.stack([pend[-1] // SUB_ROWS, jnp.asarray(total_sub, pend.dtype)]).astype(jnp.int32)

    xg = _dispatch(dest_flat, zlo, zhi, m, total_sub * SUB_ROWS, DISPATCH_ROWS)
    yg = _experts(gexp, gsub, gns, tail, xg, w_gate[0], w_up[0], b_gate[0], b_up[0],
                  w_down[0], b_down[0], n_groups.astype(jnp.int32))
    out = _combine(dest_flat, route, h2, norm_final[None, :], yg, COMBINE_ROWS)
    return out.reshape(bsz, seq, d)
```

```python
import functools

import jax
import jax.numpy as jnp
from jax import lax
from jax.experimental import pallas as pl
from jax.experimental.pallas import tpu as pltpu

F32 = jnp.float32
BF16 = jnp.bfloat16

N_META = 16
RET_HEADS = 8
CHUNK = 128
CONV_K = 3
ROPE_BASE = 10000.0
N_EXPERTS = 32
TOP_K = 4
SWIGLU_LIMIT = 7.0
SWIGLU_ALPHA = 1.702
NORM_EPS = 1e-5
GN_EPS = 1e-6

LANES = 128
SUBLANES = 8
VMEM_LIMIT = 58 * 1024 * 1024

SUB_ROWS = 128
GROUP_SUBS = 9
GROUP_ROWS = GROUP_SUBS * SUB_ROWS
GATE_UP_COLS = 256
DOWN_COLS = 512
ROUTE_LANES = 128
NORM_ROWS = 512
PROJ_ROWS = 512
PROJ_COLS = 1024
ROUTER_ROWS = 256
DISPATCH_ROWS = 256
COMBINE_ROWS = 128


def _params(sem, vmem=VMEM_LIMIT):
    return pltpu.CompilerParams(dimension_semantics=sem, vmem_limit_bytes=vmem)


def _prenorm_kernel(x_ref, g_ref, o_ref):
    x = x_ref[...]
    ms = jnp.mean(x * x, axis=-1, keepdims=True)
    o_ref[...] = (x * lax.rsqrt(ms + NORM_EPS) * g_ref[...]).astype(o_ref.dtype)


def _prenorm(x2d, g, tm):
    n, d = x2d.shape
    return pl.pallas_call(
        _prenorm_kernel,
        grid=(n // tm,),
        in_specs=[pl.BlockSpec((tm, d), lambda i: (i, 0)),
                  pl.BlockSpec((1, d), lambda i: (0, 0))],
        out_specs=pl.BlockSpec((tm, d), lambda i: (i, 0)),
        out_shape=jax.ShapeDtypeStruct((n, d), BF16),
        compiler_params=_params(("arbitrary",)),
        name="prenorm",
    )(x2d, g)


def _inproj_kernel(a_ref, am_ref, w_ref, o_ref, om_ref):
    @pl.when(pl.program_id(1) == 0)
    def _():
        om_ref[...] = jnp.dot(am_ref[...], w_ref[...].astype(BF16), preferred_element_type=F32)

    o_ref[...] = jnp.dot(a_ref[...], w_ref[...].astype(BF16), preferred_element_type=F32)


def _inproj(a, a_meta, w, tm, tn):
    n, d = a.shape
    c = w.shape[1]
    nm = a_meta.shape[0]
    return pl.pallas_call(
        _inproj_kernel,
        grid=(c // tn, n // tm),
        in_specs=[pl.BlockSpec((tm, d), lambda j, i: (i, 0)),
                  pl.BlockSpec((nm, d), lambda j, i: (0, 0)),
                  pl.BlockSpec((d, tn), lambda j, i: (0, j))],
        out_specs=[pl.BlockSpec((tm, tn), lambda j, i: (i, j)),
                   pl.BlockSpec((nm, tn), lambda j, i: (0, j))],
        out_shape=[jax.ShapeDtypeStruct((n, c), F32),
                   jax.ShapeDtypeStruct((nm, c), F32)],
        compiler_params=_params(("arbitrary", "arbitrary")),
        name="inproj",
    )(a, a_meta, w)


def _rotary_halves(ref, h, dh, cos, sin):
    half = dh // 2
    t1 = ref[:, h * dh:h * dh + half]
    t2 = ref[:, h * dh + half:(h + 1) * dh]
    return t1 * cos - t2 * sin, t2 * cos + t1 * sin


def _state_update(k1, k2, kdec, v_bf):
    kd = jnp.concatenate([k1 * kdec, k2 * kdec], axis=1).astype(BF16)
    return lax.dot_general(kd, v_bf, (((0,), (0,)), ((), ())), preferred_element_type=F32)


def _meta_kernel(k_ref, v_ref, cc_ref, ch_ref, cos_ref, sin_ref, kdec_ref, s0_ref, u0_ref, *, dh):
    cos = cos_ref[...]
    sin = sin_ref[...]
    scale = dh ** -0.5
    for h in range(RET_HEADS):
        k1, k2 = _rotary_halves(k_ref, h, dh, cos, sin)
        v_bf = v_ref[:, h * dh:(h + 1) * dh].astype(BF16)
        s0_ref[h] = _state_update(k1 * scale, k2 * scale, kdec_ref[h], v_bf)
    u0_ref[...] = cc_ref[CHUNK - SUBLANES:CHUNK, :] * ch_ref[CHUNK - SUBLANES:CHUNK, :]


def _meta_state(pm_pad, cosm, sinm, kdec, r, dh):
    col = lambda cb: pl.BlockSpec((CHUNK, r), lambda i: (0, cb))
    full2 = pl.BlockSpec((CHUNK, LANES), lambda i: (0, 0))
    return pl.pallas_call(
        functools.partial(_meta_kernel, dh=dh),
        grid=(1,),
        in_specs=[col(1), col(2), col(5), col(6), full2, full2,
                  pl.BlockSpec((RET_HEADS, CHUNK, LANES), lambda i: (0, 0, 0))],
        out_specs=[pl.BlockSpec((RET_HEADS, dh, dh), lambda i: (0, 0, 0)),
                   pl.BlockSpec((SUBLANES, r), lambda i: (0, 0))],
        out_shape=[jax.ShapeDtypeStruct((RET_HEADS, dh, dh), F32),
                   jax.ShapeDtypeStruct((SUBLANES, r), F32)],
        compiler_params=_params(("arbitrary",)),
        name="meta_state",
    )(pm_pad, pm_pad, pm_pad, pm_pad, cosm, sinm, kdec)


def _mixer_kernel(cdec_ref, q_ref, k_ref, v_ref, g_ref, cb_ref, cc_ref, ch_ref,
                  cos_ref, sin_ref, dmask_ref, qdec_ref, kdec_ref, cw_ref, s0_ref, u0_ref,
                  mix_ref, state_ref, uext_ref, *, dh, r):
    @pl.when(pl.program_id(1) == 0)
    def _():
        state_ref[...] = s0_ref[...]
        uext_ref[0:SUBLANES, :] = u0_ref[...]

    cos = cos_ref[...]
    sin = sin_ref[...]
    scale = dh ** -0.5
    for h in range(RET_HEADS):
        q1, q2 = _rotary_halves(q_ref, h, dh, cos, sin)
        k1, k2 = _rotary_halves(k_ref, h, dh, cos, sin)
        k1 = k1 * scale
        k2 = k2 * scale
        qb = jnp.concatenate([q1, q2], axis=1).astype(BF16)
        kb = jnp.concatenate([k1, k2], axis=1).astype(BF16)
        v_bf = v_ref[:, h * dh:(h + 1) * dh].astype(BF16)
        scores = lax.dot_general(qb, kb, (((1,), (1,)), ((), ())), preferred_element_type=F32)
        sm = (scores * dmask_ref[h]).astype(BF16)
        inner = jnp.dot(sm, v_bf, preferred_element_type=F32)
        qdec = qdec_ref[h]
        qd = jnp.concatenate([q1 * qdec, q2 * qdec], axis=1).astype(BF16)
        st = state_ref[h]
        cross = jnp.dot(qd, st.astype(BF16), preferred_element_type=F32)
        state_ref[h] = st * cdec_ref[h] + _state_update(k1, k2, kdec_ref[h], v_bf)
        y = inner + cross
        mu = jnp.mean(y, axis=-1, keepdims=True)
        yc = y - mu
        var = jnp.mean(yc * yc, axis=-1, keepdims=True)
        yn = yc * lax.rsqrt(var + GN_EPS)
        gh = g_ref[:, h * dh:(h + 1) * dh]
        mix_ref[:, h * dh:(h + 1) * dh] = (gh * jax.nn.sigmoid(gh) * yn).astype(mix_ref.dtype)

    cw = 512
    for cs in range(0, r, cw):
        sl = slice(cs, cs + cw)
        u = cc_ref[:, sl] * ch_ref[:, sl]
        uext_ref[SUBLANES:SUBLANES + CHUNK, sl] = u
        u1 = uext_ref[SUBLANES - 1:SUBLANES - 1 + CHUNK, sl]
        u2 = uext_ref[SUBLANES - 2:SUBLANES - 2 + CHUNK, sl]
        conv = cw_ref[0:1, sl] * u2 + cw_ref[1:2, sl] * u1 + cw_ref[2:3, sl] * u
        mix_ref[:, r + cs:r + cs + cw] = (cb_ref[:, sl] * conv).astype(mix_ref.dtype)
        uext_ref[0:SUBLANES, sl] = uext_ref[CHUNK:CHUNK + SUBLANES, sl]


def _mixer(p, cos, sin, dmask, qdec, kdec, cdec, conv_w, s0, u0, bsz, seq, r, dh):
    nc = seq // CHUNK
    d = 2 * r
    col = lambda cb: pl.BlockSpec((CHUNK, r), lambda b, c: (b * nc + c, cb))
    tab = pl.BlockSpec((CHUNK, LANES), lambda b, c: (c, 0))
    hconst = pl.BlockSpec((RET_HEADS, CHUNK, LANES), lambda b, c: (0, 0, 0))
    return pl.pallas_call(
        functools.partial(_mixer_kernel, dh=dh, r=r),
        grid=(bsz, nc),
        in_specs=[pl.BlockSpec(memory_space=pltpu.SMEM),
                  col(0), col(1), col(2), col(3), col(4), col(5), col(6),
                  tab, tab, hconst, hconst, hconst,
                  pl.BlockSpec((CONV_K, r), lambda b, c: (0, 0)),
                  pl.BlockSpec((RET_HEADS, dh, dh), lambda b, c: (0, 0, 0)),
                  pl.BlockSpec((SUBLANES, r), lambda b, c: (0, 0))],
        out_specs=pl.BlockSpec((CHUNK, d), lambda b, c: (b * nc + c, 0)),
        out_shape=jax.ShapeDtypeStruct((bsz * seq, d), BF16),
        scratch_shapes=[pltpu.VMEM((RET_HEADS, dh, dh), F32),
                        pltpu.VMEM((CHUNK + 2 * SUBLANES, r), F32)],
        compiler_params=_params(("arbitrary", "arbitrary")),
        name="mixer",
    )(cdec, p, p, p, p, p, p, p, cos, sin, dmask, qdec, kdec, conv_w, s0, u0)


def _outproj_kernel(a_ref, w_ref, x_ref, o_ref):
    o_ref[...] = x_ref[...] + jnp.dot(a_ref[...], w_ref[...].astype(BF16), preferred_element_type=F32)


def _outproj(mix, w, x2d, tm, tn):
    n, d = mix.shape
    c = w.shape[1]
    return pl.pallas_call(
        _outproj_kernel,
        grid=(c // tn, n // tm),
        in_specs=[pl.BlockSpec((tm, d), lambda j, i: (i, 0)),
                  pl.BlockSpec((d, tn), lambda j, i: (0, j)),
                  pl.BlockSpec((tm, tn), lambda j, i: (i, j))],
        out_specs=pl.BlockSpec((tm, tn), lambda j, i: (i, j)),
        out_shape=jax.ShapeDtypeStruct((n, c), F32),
        compiler_params=_params(("arbitrary", "arbitrary")),
        name="outproj",
    )(mix, w, x2d)


def _pack_bf16_pairs(x):
    half = x.shape[1] // 2
    bits = lax.bitcast_convert_type(x.astype(BF16).astype(F32), jnp.uint32)
    return bits[:, half:] | (bits[:, :half] >> 16)


def _unpack_bf16_pairs(w):
    lo = lax.bitcast_convert_type(w << 16, F32)
    hi = lax.bitcast_convert_type(w & jnp.uint32(0xFFFF0000), F32)
    return lo.astype(BF16), hi.astype(BF16)


def _router_kernel(h_ref, g_ref, w_ref, b_ref, m_ref, route_ref, cnt_ref, carry_ref, *, tr):
    @pl.when(pl.program_id(0) == 0)
    def _():
        carry_ref[...] = jnp.zeros_like(carry_ref)

    x = h_ref[...]
    ms = jnp.mean(x * x, axis=-1, keepdims=True)
    m = x * lax.rsqrt(ms + NORM_EPS) * g_ref[...]
    m_ref[...] = _pack_bf16_pairs(m)

    logits = jnp.dot(m.astype(BF16), w_ref[...], preferred_element_type=F32) + b_ref[...]

    lane = lax.broadcasted_iota(jnp.int32, (tr, ROUTE_LANES), 1)
    lane_f = lane.astype(F32)
    work = logits
    vals, onehots = [], []
    for _ in range(TOP_K):
        mx = jnp.max(work, axis=-1, keepdims=True)
        idx = jnp.min(jnp.where(work == mx, lane_f, float(ROUTE_LANES)), axis=-1, keepdims=True)
        oh = lane_f == idx
        vals.append(mx)
        onehots.append(oh)
        work = jnp.where(oh, -jnp.inf, work)

    exps = [jnp.exp(v - vals[0]) for v in vals]
    denom = exps[0] + exps[1] + exps[2] + exps[3]
    gates = [e / denom for e in exps]

    chosen = onehots[0] | onehots[1] | onehots[2] | onehots[3]
    cmat = jnp.where(chosen, 1.0, 0.0).astype(BF16)
    row = lax.broadcasted_iota(jnp.int32, (tr, tr), 0)
    colm = lax.broadcasted_iota(jnp.int32, (tr, tr), 1)
    lower = jnp.where(colm < row, 1.0, 0.0).astype(BF16)
    carry = carry_ref[0:1, :]
    rank_e = jnp.dot(lower, cmat, preferred_element_type=F32) + carry
    new_carry = carry + jnp.sum(cmat.astype(F32), axis=0, keepdims=True)
    carry_ref[...] = jnp.broadcast_to(new_carry, carry_ref.shape)
    cnt_ref[...] = jnp.broadcast_to(new_carry, cnt_ref.shape)

    out = jnp.zeros((tr, ROUTE_LANES), F32)
    for k in range(TOP_K):
        idx_k = jnp.sum(jnp.where(onehots[k], lane_f, 0.0), axis=-1, keepdims=True)
        rank_k = jnp.sum(jnp.where(onehots[k], rank_e, 0.0), axis=-1, keepdims=True)
        out = jnp.where(lane == k, gates[k], out)
        out = jnp.where(lane == TOP_K + k, idx_k, out)
        out = jnp.where(lane == 2 * TOP_K + k, rank_k, out)
    route_ref[...] = out


def _router(h2, g, w, b, tr):
    n, d = h2.shape
    return pl.pallas_call(
        functools.partial(_router_kernel, tr=tr),
        grid=(n // tr,),
        in_specs=[pl.BlockSpec((tr, d), lambda i: (i, 0)),
                  pl.BlockSpec((1, d), lambda i: (0, 0)),
                  pl.BlockSpec((d, ROUTE_LANES), lambda i: (0, 0)),
                  pl.BlockSpec((1, ROUTE_LANES), lambda i: (0, 0))],
        out_specs=[pl.BlockSpec((tr, d // 2), lambda i: (i, 0)),
                   pl.BlockSpec((tr, ROUTE_LANES), lambda i: (i, 0)),
                   pl.BlockSpec((SUBLANES, ROUTE_LANES), lambda i: (0, 0))],
        out_shape=[jax.ShapeDtypeStruct((n, d // 2), jnp.uint32),
                   jax.ShapeDtypeStruct((n, ROUTE_LANES), F32),
                   jax.ShapeDtypeStruct((SUBLANES, ROUTE_LANES), F32)],
        scratch_shapes=[pltpu.VMEM((SUBLANES, ROUTE_LANES), F32)],
        compiler_params=_params(("arbitrary",)),
        name="router",
    )(h2, g, w, b)


def _row_copy(src_ref, src_row, dst_ref, dst_row, sem):
    return pltpu.make_async_copy(src_ref.at[pl.ds(src_row, 1), :], dst_ref.at[pl.ds(dst_row, 1), :], sem)


def _dispatch_kernel(dest_ref, zlo_ref, zhi_ref, m_ref, xg_ref, zero_ref, sem, zsem, *, tt, n_steps):
    @pl.when(pl.program_id(0) == 0)
    def _():
        zero_ref[...] = jnp.zeros_like(zero_ref)

        def range_body(e, carry):
            def zero_start(rr, c):
                _row_copy(zero_ref, 0, xg_ref, rr, zsem).start()
                return c

            lax.fori_loop(zlo_ref[e], zhi_ref[e], zero_start, 0)
            return carry

        lax.fori_loop(0, N_EXPERTS + 1, range_body, 0)

    @pl.when(pl.program_id(0) == pl.num_programs(0) - 1)
    def _():
        n_zero = xg_ref.shape[0] - TOP_K * tt * n_steps
        pltpu.make_async_copy(xg_ref.at[pl.ds(0, n_zero), :], xg_ref.at[pl.ds(0, n_zero), :], zsem).wait()

    def start_body(t, c):
        for k in range(TOP_K):
            _row_copy(m_ref, t, xg_ref, dest_ref[t * TOP_K + k], sem).start()
        return c

    lax.fori_loop(0, tt, start_body, 0)
    for k in range(TOP_K):
        pltpu.make_async_copy(m_ref, xg_ref.at[pl.ds(0, tt), :], sem).wait()


def _dispatch(dest_flat, zlo, zhi, m, rows, tt):
    n, d = m.shape
    return pl.pallas_call(
        functools.partial(_dispatch_kernel, tt=tt, n_steps=n // tt),
        grid=(n // tt,),
        in_specs=[pl.BlockSpec((tt * TOP_K,), lambda i: (i,), memory_space=pltpu.SMEM),
                  pl.BlockSpec(memory_space=pltpu.SMEM),
                  pl.BlockSpec(memory_space=pltpu.SMEM),
                  pl.BlockSpec((tt, d), lambda i: (i, 0))],
        out_specs=pl.BlockSpec(memory_space=pl.ANY),
        out_shape=jax.ShapeDtypeStruct((rows, d), m.dtype),
        scratch_shapes=[pltpu.VMEM((SUBLANES, d), m.dtype),
                        pltpu.SemaphoreType.DMA(()),
                        pltpu.SemaphoreType.DMA(())],
        compiler_params=_params(("arbitrary",)),
        name="dispatch",
    )(dest_flat, zlo, zhi, m)


def _for_row_blocks(nsub, block_fn):
    n8 = nsub // 8
    rem8 = nsub - n8 * 8
    merged = jnp.where((rem8 == 1) & (n8 >= 1), 1, 0)

    def body(i, c):
        start = pl.multiple_of(i * (8 * SUB_ROWS), 8 * SUB_ROWS)
        block_fn(start, 4 * SUB_ROWS)
        block_fn(start + 4 * SUB_ROWS, 4 * SUB_ROWS)
        return c

    lax.fori_loop(0, n8 - merged, body, 0)

    @pl.when(merged == 1)
    def _():
        start = pl.multiple_of((n8 - 1) * (8 * SUB_ROWS), 8 * SUB_ROWS)
        block_fn(start, 4 * SUB_ROWS)
        block_fn(start + 4 * SUB_ROWS, 5 * SUB_ROWS)

    rem8 = rem8 * (1 - merged)
    base8 = n8 * (8 * SUB_ROWS)

    @pl.when(rem8 >= 4)
    def _():
        block_fn(pl.multiple_of(base8, 4 * SUB_ROWS), 4 * SUB_ROWS)

    rem = rem8 % 4
    base = base8 + (rem8 // 4) * (4 * SUB_ROWS)

    @pl.when(rem >= 2)
    def _():
        block_fn(pl.multiple_of(base, SUB_ROWS), 2 * SUB_ROWS)

    @pl.when(rem % 2 == 1)
    def _():
        block_fn(pl.multiple_of(base + (rem // 2) * (2 * SUB_ROWS), SUB_ROWS), SUB_ROWS)


def _expert_kernel(gexp_ref, gsub_ref, gns_ref, tail_ref,
                   xg_ref, wg_ref, wu_ref, bg_ref, bu_ref, wd_ref, bd_ref, yg_ref,
                   xb_ref, hid_ref, stage_ref, ybuf_ref, pend_ref,
                   xsem, ysem, *, nj, nn):
    g = pl.program_id(0)
    s = pl.program_id(1)
    tf = GATE_UP_COLS
    tn = DOWN_COLS
    ns = gns_ref[g]
    row0 = gsub_ref[g] * SUB_ROWS

    def aligned(v, m):
        return v if isinstance(v, int) else pl.multiple_of(v, m)

    def y_copy(sub, slot, col, first_row):
        return pltpu.make_async_copy(
            ybuf_ref.at[slot, pl.ds(aligned(sub * SUB_ROWS, SUB_ROWS), SUB_ROWS), :],
            yg_ref.at[pl.ds(aligned(first_row + sub * SUB_ROWS, SUB_ROWS), SUB_ROWS),
                      pl.ds(aligned(col * tn, tn), tn)],
            ysem.at[slot])

    def x_copy(first_row, sub, slot):
        return pltpu.make_async_copy(
            xg_ref.at[pl.ds(aligned(first_row + sub * SUB_ROWS, SUB_ROWS), SUB_ROWS), :],
            stage_ref.at[slot], xsem.at[slot])

    def stage_to_rows(buf, sub, slot):
        rows = pl.ds(aligned(sub * SUB_ROWS, SUB_ROWS), SUB_ROWS)
        lo, hi = _unpack_bf16_pairs(stage_ref[slot])
        half = lo.shape[1]
        xb_ref[buf, rows, 0:half] = lo
        xb_ref[buf, rows, half:2 * half] = hi

    def wait_pending(slot):
        def body(i, c):
            y_copy(0, slot, 0, 0).wait()
            return c

        lax.fori_loop(0, pend_ref[slot], body, 0)
        pend_ref[slot] = 0

    @pl.when((g == 0) & (s == 0))
    def _():
        pend_ref[0] = 0
        pend_ref[1] = 0
        ybuf_ref[0, 0:SUB_ROWS, :] = jnp.zeros((SUB_ROWS, tn), F32)
        lo = tail_ref[0]
        hi = tail_ref[1]

        def start_body(i, c):
            for col in range(nn):
                y_copy(0, 0, col, i * SUB_ROWS).start()
            return c

        def wait_body(i, c):
            for col in range(nn):
                y_copy(0, 0, col, i * SUB_ROWS).wait()
            return c

        lax.fori_loop(lo, hi, start_body, 0)
        lax.fori_loop(lo, hi, wait_body, 0)

        x_copy(row0, 0, 0).start()

        def first_body(sub, c):
            slot = lax.rem(sub, 2)

            @pl.when(sub + 1 < ns)
            def _():
                x_copy(row0, sub + 1, 1 - slot).start()

            x_copy(row0, sub, slot).wait()
            stage_to_rows(0, sub, slot)
            return c

        lax.fori_loop(0, ns, first_body, 0)

    @pl.when(ns > 0)
    def _():
        @pl.when(s < nj)
        def _():
            bg = bg_ref[gexp_ref[g], pl.ds(s, 1), :]
            bu = bu_ref[gexp_ref[g], pl.ds(s, 1), :]
            cur = lax.rem(g, 2)

            last_g = pl.num_programs(0) - 1
            g_next = jnp.minimum(g + 1, last_g)
            ns_next = jnp.where(g < last_g, gns_ref[g_next], 0)
            row0_next = gsub_ref[g_next] * SUB_ROWS
            for k in range(2):
                @pl.when(2 * s + k < ns_next)
                def _():
                    x_copy(row0_next, 2 * s + k, k).start()

            def block(start, rows_n):
                rows = pl.ds(start, rows_n)
                x = xb_ref[cur, rows, :]
                gt = jnp.dot(x, wg_ref[0].astype(BF16), preferred_element_type=F32) + bg
                up = jnp.dot(x, wu_ref[0].astype(BF16), preferred_element_type=F32) + bu
                gt = jnp.minimum(gt, SWIGLU_LIMIT)
                up = jnp.clip(up, -SWIGLU_LIMIT, SWIGLU_LIMIT)
                hid = (up + 1.0) * gt * jax.nn.sigmoid(SWIGLU_ALPHA * gt)
                hid_ref[s, rows, :] = hid.astype(hid_ref.dtype)

            _for_row_blocks(ns, block)

            for k in range(2):
                @pl.when(2 * s + k < ns_next)
                def _():
                    x_copy(row0_next, 2 * s + k, k).wait()
                    stage_to_rows(1 - cur, 2 * s + k, k)

        @pl.when(s >= nj)
        def _():
            col = s - nj
            slot = lax.rem(col, 2)
            bd = bd_ref[gexp_ref[g], pl.ds(col, 1), :]
            wait_pending(slot)

            def block(start, rows_n):
                rows = pl.ds(start, rows_n)
                acc = jnp.dot(hid_ref[0, rows, :], wd_ref[0, 0:tf, :].astype(BF16),
                              preferred_element_type=F32)
                for c in range(1, nj):
                    acc = acc + jnp.dot(hid_ref[c, rows, :], wd_ref[0, c * tf:(c + 1) * tf, :].astype(BF16),
                                        preferred_element_type=F32)
                ybuf_ref[slot, rows, :] = acc + bd

            _for_row_blocks(ns, block)

            def start_body(sub, c):
                y_copy(sub, slot, col, row0).start()
                return c

            lax.fori_loop(0, ns, start_body, 0)
            pend_ref[slot] = ns

    @pl.when((g == pl.num_programs(0) - 1) & (s == nj + nn - 1))
    def _():
        wait_pending(0)
        wait_pending(1)


def _experts(gexp, gsub, gns, tail, xg, w_gate, w_up, b_gate, b_up, w_down, b_down, n_groups):
    rows = xg.shape[0]
    d = w_gate.shape[1]
    f = w_gate.shape[2]
    tf, tn = GATE_UP_COLS, DOWN_COLS
    nj, nn = f // tf, d // tn
    assert GROUP_SUBS <= 2 * nj, "next group's rows are fetched two sub-blocks per gate/up step"

    def up_idx(g, s, ge, gs, gn, tl):
        act = jnp.minimum(gn[g], 1)
        return (ge[g], 0, jnp.minimum(s, nj - 1) * act + (nj - 1) * (1 - act))

    def down_idx(g, s, ge, gs, gn, tl):
        act = jnp.minimum(gn[g], 1)
        return (ge[g], 0, jnp.maximum(s - nj, 0) * act + (nn - 1) * (1 - act))

    def whole(g, s, ge, gs, gn, tl):
        return (0, 0, 0)

    grid_spec = pltpu.PrefetchScalarGridSpec(
        num_scalar_prefetch=4,
        grid=(n_groups, nj + nn),
        in_specs=[
            pl.BlockSpec(memory_space=pl.ANY),
            pl.BlockSpec((1, d, tf), up_idx),
            pl.BlockSpec((1, d, tf), up_idx),
            pl.BlockSpec((N_EXPERTS, nj, tf), whole),
            pl.BlockSpec((N_EXPERTS, nj, tf), whole),
            pl.BlockSpec((1, f, tn), down_idx),
            pl.BlockSpec((N_EXPERTS, nn, tn), whole),
        ],
        out_specs=pl.BlockSpec(memory_space=pl.ANY),
        scratch_shapes=[pltpu.VMEM((2, GROUP_ROWS, d), BF16),
                        pltpu.VMEM((nj, GROUP_ROWS, tf), BF16),
                        pltpu.VMEM((2, SUB_ROWS, d // 2), xg.dtype),
                        pltpu.VMEM((2, GROUP_ROWS, tn), F32),
                        pltpu.SMEM((2,), jnp.int32),
                        pltpu.SemaphoreType.DMA((2,)),
                        pltpu.SemaphoreType.DMA((2,))],
    )
    return pl.pallas_call(
        functools.partial(_expert_kernel, nj=nj, nn=nn),
        grid_spec=grid_spec,
        out_shape=jax.ShapeDtypeStruct((rows, d), F32),
        compiler_params=_params(("arbitrary", "arbitrary")),
        name="experts",
    )(gexp, gsub, gns, tail, xg, w_gate, w_up, b_gate.reshape(N_EXPERTS, nj, tf),
      b_up.reshape(N_EXPERTS, nj, tf), w_down, b_down.reshape(N_EXPERTS, nn, tn))


def _combine_kernel(dcur_ref, dnext_ref, route_ref, h_ref, g_ref, yg_ref, o_ref, buf_ref, sem, *, tt):
    i = pl.program_id(0)
    n_steps = pl.num_programs(0)
    slot = lax.rem(i, 2)

    def start_tile(dref, sl):
        def body(t, c):
            for k in range(TOP_K):
                pltpu.make_async_copy(yg_ref.at[pl.ds(dref[t * TOP_K + k], 1), :],
                                      buf_ref.at[sl, k, pl.ds(t, 1), :], sem.at[sl]).start()
            return c

        lax.fori_loop(0, tt, body, 0, unroll=4)

    @pl.when(i == 0)
    def _():
        start_tile(dcur_ref, 0)

    @pl.when(i + 1 < n_steps)
    def _():
        start_tile(dnext_ref, 1 - slot)

    for k in range(TOP_K):
        pltpu.make_async_copy(yg_ref.at[pl.ds(0, tt), :], buf_ref.at[slot, k], sem.at[slot]).wait()

    d = o_ref.shape[1]
    cw = 2 * LANES
    gates = [route_ref[:, k:k + 1] for k in range(TOP_K)]
    ssq = jnp.zeros((tt, 1), F32)
    for cs in range(0, d, cw):
        acc = h_ref[:, cs:cs + cw]
        for k in range(TOP_K):
            acc = acc + buf_ref[slot, k, :, cs:cs + cw] * gates[k]
        o_ref[:, cs:cs + cw] = acc
        ssq = ssq + jnp.sum(acc * acc, axis=-1, keepdims=True)
    scale = lax.rsqrt(ssq * (1.0 / d) + NORM_EPS)
    for cs in range(0, d, cw):
        o_ref[:, cs:cs + cw] = o_ref[:, cs:cs + cw] * scale * g_ref[:, cs:cs + cw]


def _combine(dest_flat, route, h2, g, yg, tt):
    n, d = h2.shape
    steps = n // tt
    return pl.pallas_call(
        functools.partial(_combine_kernel, tt=tt),
        grid=(steps,),
        in_specs=[pl.BlockSpec((tt * TOP_K,), lambda i: (i,), memory_space=pltpu.SMEM),
                  pl.BlockSpec((tt * TOP_K,), lambda i: (jnp.minimum(i + 1, steps - 1),),
                               memory_space=pltpu.SMEM),
                  pl.BlockSpec((tt, ROUTE_LANES), lambda i: (i, 0)),
                  pl.BlockSpec((tt, d), lambda i: (i, 0)),
                  pl.BlockSpec((1, d), lambda i: (0, 0)),
                  pl.BlockSpec(memory_space=pl.ANY)],
        out_specs=pl.BlockSpec((tt, d), lambda i: (i, 0)),
        out_shape=jax.ShapeDtypeStruct((n, d), F32),
        scratch_shapes=[pltpu.VMEM((2, TOP_K, tt, d), F32),
                        pltpu.SemaphoreType.DMA((2,))],
        compiler_params=_params(("arbitrary",)),
        name="combine",
    )(dest_flat, dest_flat, route, h2, g, yg)


def _retention_tables(dh):
    lg = jnp.log1p(-jnp.exp2(-5.0 - jnp.arange(RET_HEADS, dtype=F32)))
    i = jnp.arange(CHUNK, dtype=F32)
    rel = i[:, None] - i[None, :]
    causal = rel >= 0
    dmask = jnp.where(causal[None], jnp.exp(jnp.where(causal, rel, 0.0)[None] * lg[:, None, None]), 0.0)
    q_dec = jnp.exp((i + 1.0)[None, :] * lg[:, None])
    k_dec = jnp.exp((CHUNK - 1.0 - i)[None, :] * lg[:, None])
    c_dec = jnp.exp(CHUNK * lg)
    bcast = lambda t: jnp.broadcast_to(t[:, :, None], (RET_HEADS, CHUNK, LANES))
    return dmask, bcast(q_dec), bcast(k_dec), c_dec


def _rotary_tables(length, dh):
    half = dh // 2
    inv = ROPE_BASE ** (-jnp.arange(half, dtype=F32) / half)
    ang = jnp.arange(length, dtype=jnp.int32).astype(F32)[:, None] * inv[None, :]
    return jnp.cos(ang), jnp.sin(ang)


def kernel(x, meta_tokens, norm_mix, w_in, conv_w, w_out, norm_ffn, router_w, router_b,
           w_gate, b_gate, w_up, b_up, w_down, b_down, norm_final):
    bsz, seq, d = x.shape
    assert w_in.shape[0] == 1, "single-layer stack expected"
    assert meta_tokens.shape[0] == N_META and seq % CHUNK == 0
    r = d // 2
    dh = r // RET_HEADS
    assert dh // 2 == LANES
    n = bsz * seq
    x2d = x.reshape(n, d)

    a = _prenorm(x2d, norm_mix[0][None, :], NORM_ROWS)
    a_meta = _prenorm(meta_tokens.astype(x.dtype), norm_mix[0][None, :], N_META)
    p, pm = _inproj(a, a_meta, w_in[0], PROJ_ROWS, PROJ_COLS)

    cos, sin = _rotary_tables(N_META + seq, dh)
    pad_meta = lambda t: jnp.pad(t, ((CHUNK - N_META, 0), (0, 0)))
    dmask, qdec, kdec, cdec = _retention_tables(dh)
    s0, u0 = _meta_state(pad_meta(pm), pad_meta(cos[:N_META]), pad_meta(sin[:N_META]), kdec, r, dh)
    mix = _mixer(p, cos[N_META:], sin[N_META:], dmask, qdec, kdec, cdec, conv_w[0], s0, u0,
                 bsz, seq, r, dh)
    h2 = _outproj(mix, w_out[0], x2d, PROJ_ROWS, PROJ_COLS)

    rw = jnp.pad(router_w[0], ((0, 0), (0, ROUTE_LANES - N_EXPERTS))).astype(BF16)
    rb = jnp.pad(router_b[0].astype(F32), (0, ROUTE_LANES - N_EXPERTS), constant_values=-1e30)[None, :]
    m, route, cnt = _router(h2, norm_ffn[0][None, :], rw, rb, ROUTER_ROWS)

    side = route[:, TOP_K:3 * TOP_K].T.astype(jnp.int32)
    ids, rank = side[:TOP_K], side[TOP_K:]
    counts = cnt[0, :N_EXPERTS].astype(jnp.int32)

    total_sub = (n * TOP_K + N_EXPERTS * (SUB_ROWS - 1) + SUB_ROWS - 1) // SUB_ROWS
    psub = (counts + SUB_ROWS - 1) // SUB_ROWS
    pend = jnp.cumsum(psub) * SUB_ROWS
    pstart = pend - psub * SUB_ROWS
    first_row = jnp.zeros_like(ids)
    for e in range(N_EXPERTS):
        first_row = jnp.where(ids == e, pstart[e], first_row)
    dest_flat = (first_row + rank).T.reshape(-1).astype(jnp.int32)
    zlo = jnp.concatenate([pstart + counts, pend[-1:]]).astype(jnp.int32)
    zhi = jnp.concatenate([pend, jnp.full((1,), total_sub * SUB_ROWS)]).astype(jnp.int32)

    n_groups_max = N_EXPERTS + total_sub // GROUP_SUBS
    ngrp = (psub + GROUP_SUBS - 1) // GROUP_SUBS
    gend = jnp.cumsum(ngrp)
    gstart = gend - ngrp
    n_groups = gend[-1]
    gidx = jnp.arange(n_groups_max, dtype=jnp.int32)
    glast = jnp.minimum(gidx, n_groups - 1)
    gexp = jnp.minimum(jnp.searchsorted(gend, glast, side="right"), N_EXPERTS - 1).astype(jnp.int32)
    gq = glast - gstart[gexp]
    gsub = (pstart[gexp] // SUB_ROWS + gq * GROUP_SUBS).astype(jnp.int32)
    gns = jnp.where(gidx < n_groups, jnp.minimum(psub[gexp] - gq * GROUP_SUBS, GROUP_SUBS), 0).astype(jnp.int32)
    tail = jnp.stack([pend[-1] // SUB_ROWS, jnp.asarray(total_sub, pend.dtype)]).astype(jnp.int32)

    xg = _dispatch(dest_flat, zlo, zhi, m, total_sub * SUB_ROWS, DISPATCH_ROWS)
    yg = _experts(gexp, gsub, gns, tail, xg, w_gate[0], w_up[0], b_gate[0], b_up[0],
                  w_down[0], b_down[0], n_groups.astype(jnp.int32))
    out = _combine(dest_flat, route, h2, norm_final[None, :], yg, COMBINE_ROWS)
    return out.reshape(bsz, seq, d)
```

```python
import functools

import jax
import jax.numpy as jnp
from jax import lax
from jax.experimental import pallas as pl
from jax.experimental.pallas import tpu as pltpu

F32 = jnp.float32
BF16 = jnp.bfloat16

N_META = 16
RET_HEADS = 8
CHUNK = 128
CONV_K = 3
ROPE_BASE = 10000.0
N_EXPERTS = 32
TOP_K = 4
SWIGLU_LIMIT = 7.0
SWIGLU_ALPHA = 1.702
NORM_EPS = 1e-5
GN_EPS = 1e-6

LANES = 128
SUBLANES = 8
VMEM_LIMIT = 58 * 1024 * 1024

SUB_ROWS = 128
GROUP_SUBS = 9
GROUP_ROWS = GROUP_SUBS * SUB_ROWS
GATE_UP_COLS = 256
DOWN_COLS = 512
ROUTE_LANES = 128
NORM_ROWS = 512
PROJ_ROWS = 512
PROJ_COLS = 1024
ROUTER_ROWS = 256
DISPATCH_ROWS = 256
COMBINE_ROWS = 256


def _params(sem, vmem=VMEM_LIMIT):
    return pltpu.CompilerParams(dimension_semantics=sem, vmem_limit_bytes=vmem)


def _prenorm_kernel(x_ref, g_ref, o_ref):
    x = x_ref[...]
    ms = jnp.mean(x * x, axis=-1, keepdims=True)
    o_ref[...] = (x * lax.rsqrt(ms + NORM_EPS) * g_ref[...]).astype(o_ref.dtype)


def _prenorm(x2d, g, tm):
    n, d = x2d.shape
    return pl.pallas_call(
        _prenorm_kernel,
        grid=(n // tm,),
        in_specs=[pl.BlockSpec((tm, d), lambda i: (i, 0)),
                  pl.BlockSpec((1, d), lambda i: (0, 0))],
        out_specs=pl.BlockSpec((tm, d), lambda i: (i, 0)),
        out_shape=jax.ShapeDtypeStruct((n, d), BF16),
        compiler_params=_params(("arbitrary",)),
        name="prenorm",
    )(x2d, g)


def _inproj_kernel(a_ref, am_ref, w_ref, o_ref, om_ref):
    @pl.when(pl.program_id(1) == 0)
    def _():
        om_ref[...] = jnp.dot(am_ref[...], w_ref[...].astype(BF16), preferred_element_type=F32)

    o_ref[...] = jnp.dot(a_ref[...], w_ref[...].astype(BF16), preferred_element_type=F32)


def _inproj(a, a_meta, w, tm, tn):
    n, d = a.shape
    c = w.shape[1]
    nm = a_meta.shape[0]
    return pl.pallas_call(
        _inproj_kernel,
        grid=(c // tn, n // tm),
        in_specs=[pl.BlockSpec((tm, d), lambda j, i: (i, 0)),
                  pl.BlockSpec((nm, d), lambda j, i: (0, 0)),
                  pl.BlockSpec((d, tn), lambda j, i: (0, j))],
        out_specs=[pl.BlockSpec((tm, tn), lambda j, i: (i, j)),
                   pl.BlockSpec((nm, tn), lambda j, i: (0, j))],
        out_shape=[jax.ShapeDtypeStruct((n, c), F32),
                   jax.ShapeDtypeStruct((nm, c), F32)],
        compiler_params=_params(("arbitrary", "arbitrary")),
        name="inproj",
    )(a, a_meta, w)


def _rotary_halves(ref, h, dh, cos, sin):
    half = dh // 2
    t1 = ref[:, h * dh:h * dh + half]
    t2 = ref[:, h * dh + half:(h + 1) * dh]
    return t1 * cos - t2 * sin, t2 * cos + t1 * sin


def _state_update(k1, k2, kdec, v_bf):
    kd = jnp.concatenate([k1 * kdec, k2 * kdec], axis=1).astype(BF16)
    return lax.dot_general(kd, v_bf, (((0,), (0,)), ((), ())), preferred_element_type=F32)


def _meta_kernel(k_ref, v_ref, cc_ref, ch_ref, cos_ref, sin_ref, kdec_ref, s0_ref, u0_ref, *, dh):
    cos = cos_ref[...]
    sin = sin_ref[...]
    scale = dh ** -0.5
    for h in range(RET_HEADS):
        k1, k2 = _rotary_halves(k_ref, h, dh, cos, sin)
        v_bf = v_ref[:, h * dh:(h + 1) * dh].astype(BF16)
        s0_ref[h] = _state_update(k1 * scale, k2 * scale, kdec_ref[h], v_bf)
    u0_ref[...] = cc_ref[CHUNK - SUBLANES:CHUNK, :] * ch_ref[CHUNK - SUBLANES:CHUNK, :]


def _meta_state(pm_pad, cosm, sinm, kdec, r, dh):
    col = lambda cb: pl.BlockSpec((CHUNK, r), lambda i: (0, cb))
    full2 = pl.BlockSpec((CHUNK, LANES), lambda i: (0, 0))
    return pl.pallas_call(
        functools.partial(_meta_kernel, dh=dh),
        grid=(1,),
        in_specs=[col(1), col(2), col(5), col(6), full2, full2,
                  pl.BlockSpec((RET_HEADS, CHUNK, LANES), lambda i: (0, 0, 0))],
        out_specs=[pl.BlockSpec((RET_HEADS, dh, dh), lambda i: (0, 0, 0)),
                   pl.BlockSpec((SUBLANES, r), lambda i: (0, 0))],
        out_shape=[jax.ShapeDtypeStruct((RET_HEADS, dh, dh), F32),
                   jax.ShapeDtypeStruct((SUBLANES, r), F32)],
        compiler_params=_params(("arbitrary",)),
        name="meta_state",
    )(pm_pad, pm_pad, pm_pad, pm_pad, cosm, sinm, kdec)


def _mixer_kernel(cdec_ref, q_ref, k_ref, v_ref, g_ref, cb_ref, cc_ref, ch_ref,
                  cos_ref, sin_ref, dmask_ref, qdec_ref, kdec_ref, cw_ref, s0_ref, u0_ref,
                  mix_ref, state_ref, uext_ref, *, dh, r):
    @pl.when(pl.program_id(1) == 0)
    def _():
        state_ref[...] = s0_ref[...]
        uext_ref[0:SUBLANES, :] = u0_ref[...]

    cos = cos_ref[...]
    sin = sin_ref[...]
    scale = dh ** -0.5
    for h in range(RET_HEADS):
        q1, q2 = _rotary_halves(q_ref, h, dh, cos, sin)
        k1, k2 = _rotary_halves(k_ref, h, dh, cos, sin)
        k1 = k1 * scale
        k2 = k2 * scale
        qb = jnp.concatenate([q1, q2], axis=1).astype(BF16)
        kb = jnp.concatenate([k1, k2], axis=1).astype(BF16)
        v_bf = v_ref[:, h * dh:(h + 1) * dh].astype(BF16)
        scores = lax.dot_general(qb, kb, (((1,), (1,)), ((), ())), preferred_element_type=F32)
        sm = (scores * dmask_ref[h]).astype(BF16)
        inner = jnp.dot(sm, v_bf, preferred_element_type=F32)
        qdec = qdec_ref[h]
        qd = jnp.concatenate([q1 * qdec, q2 * qdec], axis=1).astype(BF16)
        st = state_ref[h]
        cross = jnp.dot(qd, st.astype(BF16), preferred_element_type=F32)
        state_ref[h] = st * cdec_ref[h] + _state_update(k1, k2, kdec_ref[h], v_bf)
        y = inner + cross
        mu = jnp.mean(y, axis=-1, keepdims=True)
        yc = y - mu
        var = jnp.mean(yc * yc, axis=-1, keepdims=True)
        yn = yc * lax.rsqrt(var + GN_EPS)
        gh = g_ref[:, h * dh:(h + 1) * dh]
        mix_ref[:, h * dh:(h + 1) * dh] = (gh * jax.nn.sigmoid(gh) * yn).astype(mix_ref.dtype)

    cw = 512
    for cs in range(0, r, cw):
        sl = slice(cs, cs + cw)
        u = cc_ref[:, sl] * ch_ref[:, sl]
        uext_ref[SUBLANES:SUBLANES + CHUNK, sl] = u
        u1 = uext_ref[SUBLANES - 1:SUBLANES - 1 + CHUNK, sl]
        u2 = uext_ref[SUBLANES - 2:SUBLANES - 2 + CHUNK, sl]
        conv = cw_ref[0:1, sl] * u2 + cw_ref[1:2, sl] * u1 + cw_ref[2:3, sl] * u
        mix_ref[:, r + cs:r + cs + cw] = (cb_ref[:, sl] * conv).astype(mix_ref.dtype)
        uext_ref[0:SUBLANES, sl] = uext_ref[CHUNK:CHUNK + SUBLANES, sl]


def _mixer(p, cos, sin, dmask, qdec, kdec, cdec, conv_w, s0, u0, bsz, seq, r, dh):
    nc = seq // CHUNK
    d = 2 * r
    col = lambda cb: pl.BlockSpec((CHUNK, r), lambda b, c: (b * nc + c, cb))
    tab = pl.BlockSpec((CHUNK, LANES), lambda b, c: (c, 0))
    hconst = pl.BlockSpec((RET_HEADS, CHUNK, LANES), lambda b, c: (0, 0, 0))
    return pl.pallas_call(
        functools.partial(_mixer_kernel, dh=dh, r=r),
        grid=(bsz, nc),
        in_specs=[pl.BlockSpec(memory_space=pltpu.SMEM),
                  col(0), col(1), col(2), col(3), col(4), col(5), col(6),
                  tab, tab, hconst, hconst, hconst,
                  pl.BlockSpec((CONV_K, r), lambda b, c: (0, 0)),
                  pl.BlockSpec((RET_HEADS, dh, dh), lambda b, c: (0, 0, 0)),
                  pl.BlockSpec((SUBLANES, r), lambda b, c: (0, 0))],
        out_specs=pl.BlockSpec((CHUNK, d), lambda b, c: (b * nc + c, 0)),
        out_shape=jax.ShapeDtypeStruct((bsz * seq, d), BF16),
        scratch_shapes=[pltpu.VMEM((RET_HEADS, dh, dh), F32),
                        pltpu.VMEM((CHUNK + 2 * SUBLANES, r), F32)],
        compiler_params=_params(("arbitrary", "arbitrary")),
        name="mixer",
    )(cdec, p, p, p, p, p, p, p, cos, sin, dmask, qdec, kdec, conv_w, s0, u0)


def _outproj_kernel(a_ref, w_ref, x_ref, o_ref):
    o_ref[...] = x_ref[...] + jnp.dot(a_ref[...], w_ref[...].astype(BF16), preferred_element_type=F32)


def _outproj(mix, w, x2d, tm, tn):
    n, d = mix.shape
    c = w.shape[1]
    return pl.pallas_call(
        _outproj_kernel,
        grid=(c // tn, n // tm),
        in_specs=[pl.BlockSpec((tm, d), lambda j, i: (i, 0)),
                  pl.BlockSpec((d, tn), lambda j, i: (0, j)),
                  pl.BlockSpec((tm, tn), lambda j, i: (i, j))],
        out_specs=pl.BlockSpec((tm, tn), lambda j, i: (i, j)),
        out_shape=jax.ShapeDtypeStruct((n, c), F32),
        compiler_params=_params(("arbitrary", "arbitrary")),
        name="outproj",
    )(mix, w, x2d)


def _pack_bf16_pairs(x):
    half = x.shape[1] // 2
    bits = lax.bitcast_convert_type(x.astype(BF16).astype(F32), jnp.uint32)
    return bits[:, half:] | (bits[:, :half] >> 16)


def _unpack_bf16_pairs(w):
    lo = lax.bitcast_convert_type(w << 16, F32)
    hi = lax.bitcast_convert_type(w & jnp.uint32(0xFFFF0000), F32)
    return lo.astype(BF16), hi.astype(BF16)


def _router_kernel(h_ref, g_ref, w_ref, b_ref, m_ref, route_ref, cnt_ref, carry_ref, *, tr):
    @pl.when(pl.program_id(0) == 0)
    def _():
        carry_ref[...] = jnp.zeros_like(carry_ref)

    x = h_ref[...]
    ms = jnp.mean(x * x, axis=-1, keepdims=True)
    m = x * lax.rsqrt(ms + NORM_EPS) * g_ref[...]
    m_ref[...] = _pack_bf16_pairs(m)

    logits = jnp.dot(m.astype(BF16), w_ref[...], preferred_element_type=F32) + b_ref[...]

    lane = lax.broadcasted_iota(jnp.int32, (tr, ROUTE_LANES), 1)
    lane_f = lane.astype(F32)
    work = logits
    vals, onehots = [], []
    for _ in range(TOP_K):
        mx = jnp.max(work, axis=-1, keepdims=True)
        idx = jnp.min(jnp.where(work == mx, lane_f, float(ROUTE_LANES)), axis=-1, keepdims=True)
        oh = lane_f == idx
        vals.append(mx)
        onehots.append(oh)
        work = jnp.where(oh, -jnp.inf, work)

    exps = [jnp.exp(v - vals[0]) for v in vals]
    denom = exps[0] + exps[1] + exps[2] + exps[3]
    gates = [e / denom for e in exps]

    chosen = onehots[0] | onehots[1] | onehots[2] | onehots[3]
    cmat = jnp.where(chosen, 1.0, 0.0).astype(BF16)
    row = lax.broadcasted_iota(jnp.int32, (tr, tr), 0)
    colm = lax.broadcasted_iota(jnp.int32, (tr, tr), 1)
    lower = jnp.where(colm < row, 1.0, 0.0).astype(BF16)
    carry = carry_ref[0:1, :]
    rank_e = jnp.dot(lower, cmat, preferred_element_type=F32) + carry
    new_carry = carry + jnp.sum(cmat.astype(F32), axis=0, keepdims=True)
    carry_ref[...] = jnp.broadcast_to(new_carry, carry_ref.shape)
    cnt_ref[...] = jnp.broadcast_to(new_carry, cnt_ref.shape)

    out = jnp.zeros((tr, ROUTE_LANES), F32)
    for k in range(TOP_K):
        idx_k = jnp.sum(jnp.where(onehots[k], lane_f, 0.0), axis=-1, keepdims=True)
        rank_k = jnp.sum(jnp.where(onehots[k], rank_e, 0.0), axis=-1, keepdims=True)
        out = jnp.where(lane == k, gates[k], out)
        out = jnp.where(lane == TOP_K + k, idx_k, out)
        out = jnp.where(lane == 2 * TOP_K + k, rank_k, out)
    route_ref[...] = out


def _router(h2, g, w, b, tr):
    n, d = h2.shape
    return pl.pallas_call(
        functools.partial(_router_kernel, tr=tr),
        grid=(n // tr,),
        in_specs=[pl.BlockSpec((tr, d), lambda i: (i, 0)),
                  pl.BlockSpec((1, d), lambda i: (0, 0)),
                  pl.BlockSpec((d, ROUTE_LANES), lambda i: (0, 0)),
                  pl.BlockSpec((1, ROUTE_LANES), lambda i: (0, 0))],
        out_specs=[pl.BlockSpec((tr, d // 2), lambda i: (i, 0)),
                   pl.BlockSpec((tr, ROUTE_LANES), lambda i: (i, 0)),
                   pl.BlockSpec((SUBLANES, ROUTE_LANES), lambda i: (0, 0))],
        out_shape=[jax.ShapeDtypeStruct((n, d // 2), jnp.uint32),
                   jax.ShapeDtypeStruct((n, ROUTE_LANES), F32),
                   jax.ShapeDtypeStruct((SUBLANES, ROUTE_LANES), F32)],
        scratch_shapes=[pltpu.VMEM((SUBLANES, ROUTE_LANES), F32)],
        compiler_params=_params(("arbitrary",)),
        name="router",
    )(h2, g, w, b)


def _row_copy(src_ref, src_row, dst_ref, dst_row, sem):
    return pltpu.make_async_copy(src_ref.at[pl.ds(src_row, 1), :], dst_ref.at[pl.ds(dst_row, 1), :], sem)


def _dispatch_kernel(dest_ref, zlo_ref, zhi_ref, m_ref, xg_ref, zero_ref, sem, zsem, *, tt, n_steps):
    @pl.when(pl.program_id(0) == 0)
    def _():
        zero_ref[...] = jnp.zeros_like(zero_ref)

        def range_body(e, carry):
            def zero_start(rr, c):
                _row_copy(zero_ref, 0, xg_ref, rr, zsem).start()
                return c

            lax.fori_loop(zlo_ref[e], zhi_ref[e], zero_start, 0)
            return carry

        lax.fori_loop(0, N_EXPERTS + 1, range_body, 0)

    @pl.when(pl.program_id(0) == pl.num_programs(0) - 1)
    def _():
        n_zero = xg_ref.shape[0] - TOP_K * tt * n_steps
        pltpu.make_async_copy(xg_ref.at[pl.ds(0, n_zero), :], xg_ref.at[pl.ds(0, n_zero), :], zsem).wait()

    def start_body(t, c):
        for k in range(TOP_K):
            _row_copy(m_ref, t, xg_ref, dest_ref[t * TOP_K + k], sem).start()
        return c

    lax.fori_loop(0, tt, start_body, 0)
    for k in range(TOP_K):
        pltpu.make_async_copy(m_ref, xg_ref.at[pl.ds(0, tt), :], sem).wait()


def _dispatch(dest_flat, zlo, zhi, m, rows, tt):
    n, d = m.shape
    return pl.pallas_call(
        functools.partial(_dispatch_kernel, tt=tt, n_steps=n // tt),
        grid=(n // tt,),
        in_specs=[pl.BlockSpec((tt * TOP_K,), lambda i: (i,), memory_space=pltpu.SMEM),
                  pl.BlockSpec(memory_space=pltpu.SMEM),
                  pl.BlockSpec(memory_space=pltpu.SMEM),
                  pl.BlockSpec((tt, d), lambda i: (i, 0))],
        out_specs=pl.BlockSpec(memory_space=pl.ANY),
        out_shape=jax.ShapeDtypeStruct((rows, d), m.dtype),
        scratch_shapes=[pltpu.VMEM((SUBLANES, d), m.dtype),
                        pltpu.SemaphoreType.DMA(()),
                        pltpu.SemaphoreType.DMA(())],
        compiler_params=_params(("arbitrary",)),
        name="dispatch",
    )(dest_flat, zlo, zhi, m)


def _for_row_blocks(nsub, block_fn):
    n8 = nsub // 8
    rem8 = nsub - n8 * 8
    merged = jnp.where((rem8 == 1) & (n8 >= 1), 1, 0)

    def body(i, c):
        start = pl.multiple_of(i * (8 * SUB_ROWS), 8 * SUB_ROWS)
        block_fn(start, 4 * SUB_ROWS)
        block_fn(start + 4 * SUB_ROWS, 4 * SUB_ROWS)
        return c

    lax.fori_loop(0, n8 - merged, body, 0)

    @pl.when(merged == 1)
    def _():
        start = pl.multiple_of((n8 - 1) * (8 * SUB_ROWS), 8 * SUB_ROWS)
        block_fn(start, 4 * SUB_ROWS)
        block_fn(start + 4 * SUB_ROWS, 5 * SUB_ROWS)

    rem8 = rem8 * (1 - merged)
    base8 = n8 * (8 * SUB_ROWS)

    @pl.when(rem8 >= 4)
    def _():
        block_fn(pl.multiple_of(base8, 4 * SUB_ROWS), 4 * SUB_ROWS)

    rem = rem8 % 4
    base = base8 + (rem8 // 4) * (4 * SUB_ROWS)

    @pl.when(rem >= 2)
    def _():
        block_fn(pl.multiple_of(base, SUB_ROWS), 2 * SUB_ROWS)

    @pl.when(rem % 2 == 1)
    def _():
        block_fn(pl.multiple_of(base + (rem // 2) * (2 * SUB_ROWS), SUB_ROWS), SUB_ROWS)


def _expert_kernel(gexp_ref, gsub_ref, gns_ref, tail_ref,
                   xg_ref, wg_ref, wu_ref, bg_ref, bu_ref, wd_ref, bd_ref, yg_ref,
                   xb_ref, hid_ref, stage_ref, ybuf_ref, pend_ref,
                   xsem, ysem, *, nj, nn):
    g = pl.program_id(0)
    s = pl.program_id(1)
    tf = GATE_UP_COLS
    tn = DOWN_COLS
    ns = gns_ref[g]
    row0 = gsub_ref[g] * SUB_ROWS

    def aligned(v, m):
        return v if isinstance(v, int) else pl.multiple_of(v, m)

    def y_copy(sub, slot, col, first_row):
        return pltpu.make_async_copy(
            ybuf_ref.at[slot, pl.ds(aligned(sub * SUB_ROWS, SUB_ROWS), SUB_ROWS), :],
            yg_ref.at[pl.ds(aligned(first_row + sub * SUB_ROWS, SUB_ROWS), SUB_ROWS),
                      pl.ds(aligned(col * tn, tn), tn)],
            ysem.at[slot])

    def x_copy(first_row, sub, slot):
        return pltpu.make_async_copy(
            xg_ref.at[pl.ds(aligned(first_row + sub * SUB_ROWS, SUB_ROWS), SUB_ROWS), :],
            stage_ref.at[slot], xsem.at[slot])

    def stage_to_rows(buf, sub, slot):
        rows = pl.ds(aligned(sub * SUB_ROWS, SUB_ROWS), SUB_ROWS)
        lo, hi = _unpack_bf16_pairs(stage_ref[slot])
        half = lo.shape[1]
        xb_ref[buf, rows, 0:half] = lo
        xb_ref[buf, rows, half:2 * half] = hi

    def wait_pending(slot):
        def body(i, c):
            y_copy(0, slot, 0, 0).wait()
            return c

        lax.fori_loop(0, pend_ref[slot], body, 0)
        pend_ref[slot] = 0

    @pl.when((g == 0) & (s == 0))
    def _():
        pend_ref[0] = 0
        pend_ref[1] = 0
        ybuf_ref[0, 0:SUB_ROWS, :] = jnp.zeros((SUB_ROWS, tn), F32)
        lo = tail_ref[0]
        hi = tail_ref[1]

        def start_body(i, c):
            for col in range(nn):
                y_copy(0, 0, col, i * SUB_ROWS).start()
            return c

        def wait_body(i, c):
            for col in range(nn):
                y_copy(0, 0, col, i * SUB_ROWS).wait()
            return c

        lax.fori_loop(lo, hi, start_body, 0)
        lax.fori_loop(lo, hi, wait_body, 0)

        x_copy(row0, 0, 0).start()

        def first_body(sub, c):
            slot = lax.rem(sub, 2)

            @pl.when(sub + 1 < ns)
            def _():
                x_copy(row0, sub + 1, 1 - slot).start()

            x_copy(row0, sub, slot).wait()
            stage_to_rows(0, sub, slot)
            return c

        lax.fori_loop(0, ns, first_body, 0)

    @pl.when(ns > 0)
    def _():
        @pl.when(s < nj)
        def _():
            bg = bg_ref[gexp_ref[g], pl.ds(s, 1), :]
            bu = bu_ref[gexp_ref[g], pl.ds(s, 1), :]
            cur = lax.rem(g, 2)

            last_g = pl.num_programs(0) - 1
            g_next = jnp.minimum(g + 1, last_g)
            ns_next = jnp.where(g < last_g, gns_ref[g_next], 0)
            row0_next = gsub_ref[g_next] * SUB_ROWS
            for k in range(2):
                @pl.when(2 * s + k < ns_next)
                def _():
                    x_copy(row0_next, 2 * s + k, k).start()

            def block(start, rows_n):
                rows = pl.ds(start, rows_n)
                x = xb_ref[cur, rows, :]
                gt = jnp.dot(x, wg_ref[0].astype(BF16), preferred_element_type=F32) + bg
                up = jnp.dot(x, wu_ref[0].astype(BF16), preferred_element_type=F32) + bu
                gt = jnp.minimum(gt, SWIGLU_LIMIT)
                up = jnp.clip(up, -SWIGLU_LIMIT, SWIGLU_LIMIT)
                hid = (up + 1.0) * gt * jax.nn.sigmoid(SWIGLU_ALPHA * gt)
                hid_ref[s, rows, :] = hid.astype(hid_ref.dtype)

            _for_row_blocks(ns, block)

            for k in range(2):
                @pl.when(2 * s + k < ns_next)
                def _():
                    x_copy(row0_next, 2 * s + k, k).wait()
                    stage_to_rows(1 - cur, 2 * s + k, k)

        @pl.when(s >= nj)
        def _():
            col = s - nj
            slot = lax.rem(col, 2)
            bd = bd_ref[gexp_ref[g], pl.ds(col, 1), :]
            wait_pending(slot)

            def block(start, rows_n):
                rows = pl.ds(start, rows_n)
                acc = jnp.dot(hid_ref[0, rows, :], wd_ref[0, 0:tf, :].astype(BF16),
                              preferred_element_type=F32)
                for c in range(1, nj):
                    acc = acc + jnp.dot(hid_ref[c, rows, :], wd_ref[0, c * tf:(c + 1) * tf, :].astype(BF16),
                                        preferred_element_type=F32)
                ybuf_ref[slot, rows, :] = acc + bd

            _for_row_blocks(ns, block)

            def start_body(sub, c):
                y_copy(sub, slot, col, row0).start()
                return c

            lax.fori_loop(0, ns, start_body, 0)
            pend_ref[slot] = ns

    @pl.when((g == pl.num_programs(0) - 1) & (s == nj + nn - 1))
    def _():
        wait_pending(0)
        wait_pending(1)


def _experts(gexp, gsub, gns, tail, xg, w_gate, w_up, b_gate, b_up, w_down, b_down, n_groups):
    rows = xg.shape[0]
    d = w_gate.shape[1]
    f = w_gate.shape[2]
    tf, tn = GATE_UP_COLS, DOWN_COLS
    nj, nn = f // tf, d // tn
    assert GROUP_SUBS <= 2 * nj, "next group's rows are fetched two sub-blocks per gate/up step"

    def up_idx(g, s, ge, gs, gn, tl):
        act = jnp.minimum(gn[g], 1)
        return (ge[g], 0, jnp.minimum(s, nj - 1) * act + (nj - 1) * (1 - act))

    def down_idx(g, s, ge, gs, gn, tl):
        act = jnp.minimum(gn[g], 1)
        return (ge[g], 0, jnp.maximum(s - nj, 0) * act + (nn - 1) * (1 - act))

    def whole(g, s, ge, gs, gn, tl):
        return (0, 0, 0)

    grid_spec = pltpu.PrefetchScalarGridSpec(
        num_scalar_prefetch=4,
        grid=(n_groups, nj + nn),
        in_specs=[
            pl.BlockSpec(memory_space=pl.ANY),
            pl.BlockSpec((1, d, tf), up_idx),
            pl.BlockSpec((1, d, tf), up_idx),
            pl.BlockSpec((N_EXPERTS, nj, tf), whole),
            pl.BlockSpec((N_EXPERTS, nj, tf), whole),
            pl.BlockSpec((1, f, tn), down_idx),
            pl.BlockSpec((N_EXPERTS, nn, tn), whole),
        ],
        out_specs=pl.BlockSpec(memory_space=pl.ANY),
        scratch_shapes=[pltpu.VMEM((2, GROUP_ROWS, d), BF16),
                        pltpu.VMEM((nj, GROUP_ROWS, tf), BF16),
                        pltpu.VMEM((2, SUB_ROWS, d // 2), xg.dtype),
                        pltpu.VMEM((2, GROUP_ROWS, tn), F32),
                        pltpu.SMEM((2,), jnp.int32),
                        pltpu.SemaphoreType.DMA((2,)),
                        pltpu.SemaphoreType.DMA((2,))],
    )
    return pl.pallas_call(
        functools.partial(_expert_kernel, nj=nj, nn=nn),
        grid_spec=grid_spec,
        out_shape=jax.ShapeDtypeStruct((rows, d), F32),
        compiler_params=_params(("arbitrary", "arbitrary")),
        name="experts",
    )(gexp, gsub, gns, tail, xg, w_gate, w_up, b_gate.reshape(N_EXPERTS, nj, tf),
      b_up.reshape(N_EXPERTS, nj, tf), w_down, b_down.reshape(N_EXPERTS, nn, tn))


def _combine_kernel(dcur_ref, dnext_ref, route_ref, h_ref, g_ref, yg_ref, o_ref, buf_ref, sem, *, tt):
    i = pl.program_id(0)
    n_steps = pl.num_programs(0)
    slot = lax.rem(i, 2)

    def start_tile(dref, sl):
        def body(t, c):
            for k in range(TOP_K):
                pltpu.make_async_copy(yg_ref.at[pl.ds(dref[t * TOP_K + k], 1), :],
                                      buf_ref.at[sl, k, pl.ds(t, 1), :], sem.at[sl]).start()
            return c

        lax.fori_loop(0, tt, body, 0, unroll=8)

    @pl.when(i == 0)
    def _():
        start_tile(dcur_ref, 0)

    @pl.when(i + 1 < n_steps)
    def _():
        start_tile(dnext_ref, 1 - slot)

    for k in range(TOP_K):
        pltpu.make_async_copy(yg_ref.at[pl.ds(0, tt), :], buf_ref.at[slot, k], sem.at[slot]).wait()

    d = o_ref.shape[1]
    cw = 2 * LANES
    gates = [route_ref[:, k:k + 1] for k in range(TOP_K)]
    ssq = jnp.zeros((tt, 1), F32)
    for cs in range(0, d, cw):
        acc = h_ref[:, cs:cs + cw]
        for k in range(TOP_K):
            acc = acc + buf_ref[slot, k, :, cs:cs + cw] * gates[k]
        o_ref[:, cs:cs + cw] = acc
        ssq = ssq + jnp.sum(acc * acc, axis=-1, keepdims=True)
    scale = lax.rsqrt(ssq * (1.0 / d) + NORM_EPS)
    for cs in range(0, d, cw):
        o_ref[:, cs:cs + cw] = o_ref[:, cs:cs + cw] * scale * g_ref[:, cs:cs + cw]


def _combine(dest_flat, route, h2, g, yg, tt):
    n, d = h2.shape
    steps = n // tt
    return pl.pallas_call(
        functools.partial(_combine_kernel, tt=tt),
        grid=(steps,),
        in_specs=[pl.BlockSpec((tt * TOP_K,), lambda i: (i,), memory_space=pltpu.SMEM),
                  pl.BlockSpec((tt * TOP_K,), lambda i: (jnp.minimum(i + 1, steps - 1),),
                               memory_space=pltpu.SMEM),
                  pl.BlockSpec((tt, ROUTE_LANES), lambda i: (i, 0)),
                  pl.BlockSpec((tt, d), lambda i: (i, 0)),
                  pl.BlockSpec((1, d), lambda i: (0, 0)),
                  pl.BlockSpec(memory_space=pl.ANY)],
        out_specs=pl.BlockSpec((tt, d), lambda i: (i, 0)),
        out_shape=jax.ShapeDtypeStruct((n, d), F32),
        scratch_shapes=[pltpu.VMEM((2, TOP_K, tt, d), F32),
                        pltpu.SemaphoreType.DMA((2,))],
        compiler_params=_params(("arbitrary",)),
        name="combine",
    )(dest_flat, dest_flat, route, h2, g, yg)


def _retention_tables(dh):
    lg = jnp.log1p(-jnp.exp2(-5.0 - jnp.arange(RET_HEADS, dtype=F32)))
    i = jnp.arange(CHUNK, dtype=F32)
    rel = i[:, None] - i[None, :]
    causal = rel >= 0
    dmask = jnp.where(causal[None], jnp.exp(jnp.where(causal, rel, 0.0)[None] * lg[:, None, None]), 0.0)
    q_dec = jnp.exp((i + 1.0)[None, :] * lg[:, None])
    k_dec = jnp.exp((CHUNK - 1.0 - i)[None, :] * lg[:, None])
    c_dec = jnp.exp(CHUNK * lg)
    bcast = lambda t: jnp.broadcast_to(t[:, :, None], (RET_HEADS, CHUNK, LANES))
    return dmask, bcast(q_dec), bcast(k_dec), c_dec


def _rotary_tables(length, dh):
    half = dh // 2
    inv = ROPE_BASE ** (-jnp.arange(half, dtype=F32) / half)
    ang = jnp.arange(length, dtype=jnp.int32).astype(F32)[:, None] * inv[None, :]
    return jnp.cos(ang), jnp.sin(ang)


def kernel(x, meta_tokens, norm_mix, w_in, conv_w, w_out, norm_ffn, router_w, router_b,
           w_gate, b_gate, w_up, b_up, w_down, b_down, norm_final):
    bsz, seq, d = x.shape
    assert w_in.shape[0] == 1, "single-layer stack expected"
    assert meta_tokens.shape[0] == N_META and seq % CHUNK == 0
    r = d // 2
    dh = r // RET_HEADS
    assert dh // 2 == LANES
    n = bsz * seq
    x2d = x.reshape(n, d)

    a = _prenorm(x2d, norm_mix[0][None, :], NORM_ROWS)
    a_meta = _prenorm(meta_tokens.astype(x.dtype), norm_mix[0][None, :], N_META)
    p, pm = _inproj(a, a_meta, w_in[0], PROJ_ROWS, PROJ_COLS)

    cos, sin = _rotary_tables(N_META + seq, dh)
    pad_meta = lambda t: jnp.pad(t, ((CHUNK - N_META, 0), (0, 0)))
    dmask, qdec, kdec, cdec = _retention_tables(dh)
    s0, u0 = _meta_state(pad_meta(pm), pad_meta(cos[:N_META]), pad_meta(sin[:N_META]), kdec, r, dh)
    mix = _mixer(p, cos[N_META:], sin[N_META:], dmask, qdec, kdec, cdec, conv_w[0], s0, u0,
                 bsz, seq, r, dh)
    h2 = _outproj(mix, w_out[0], x2d, PROJ_ROWS, PROJ_COLS)

    rw = jnp.pad(router_w[0], ((0, 0), (0, ROUTE_LANES - N_EXPERTS))).astype(BF16)
    rb = jnp.pad(router_b[0].astype(F32), (0, ROUTE_LANES - N_EXPERTS), constant_values=-1e30)[None, :]
    m, route, cnt = _router(h2, norm_ffn[0][None, :], rw, rb, ROUTER_ROWS)

    side = route[:, TOP_K:3 * TOP_K].T.astype(jnp.int32)
    ids, rank = side[:TOP_K], side[TOP_K:]
    counts = cnt[0, :N_EXPERTS].astype(jnp.int32)

    total_sub = (n * TOP_K + N_EXPERTS * (SUB_ROWS - 1) + SUB_ROWS - 1) // SUB_ROWS
    psub = (counts + SUB_ROWS - 1) // SUB_ROWS
    pend = jnp.cumsum(psub) * SUB_ROWS
    pstart = pend - psub * SUB_ROWS
    first_row = jnp.zeros_like(ids)
    for e in range(N_EXPERTS):
        first_row = jnp.where(ids == e, pstart[e], first_row)
    dest_flat = (first_row + rank).T.reshape(-1).astype(jnp.int32)
    zlo = jnp.concatenate([pstart + counts, pend[-1:]]).astype(jnp.int32)
    zhi = jnp.concatenate([pend, jnp.full((1,), total_sub * SUB_ROWS)]).astype(jnp.int32)

    n_groups_max = N_EXPERTS + total_sub // GROUP_SUBS
    ngrp = (psub + GROUP_SUBS - 1) // GROUP_SUBS
    gend = jnp.cumsum(ngrp)
    gstart = gend - ngrp
    n_groups = gend[-1]
    gidx = jnp.arange(n_groups_max, dtype=jnp.int32)
    glast = jnp.minimum(gidx, n_groups - 1)
    gexp = jnp.minimum(jnp.searchsorted(gend, glast, side="right"), N_EXPERTS - 1).astype(jnp.int32)
    gq = glast - gstart[gexp]
    gsub = (pstart[gexp] // SUB_ROWS + gq * GROUP_SUBS).astype(jnp.int32)
    gns = jnp.where(gidx < n_groups, jnp.minimum(psub[gexp] - gq * GROUP_SUBS, GROUP_SUBS), 0).astype(jnp.int32)
    tail = jnp.stack([pend[-1] // SUB_ROWS, jnp.asarray(total_sub, pend.dtype)]).astype(jnp.int32)

    xg = _dispatch(dest_flat, zlo, zhi, m, total_sub * SUB_ROWS, DISPATCH_ROWS)
    yg = _experts(gexp, gsub, gns, tail, xg, w_gate[0], w_up[0], b_gate[0], b_up[0],
                  w_down[0], b_down[0], n_groups.astype(jnp.int32))
    out = _combine(dest_flat, route, h2, norm_final[None, :], yg, COMBINE_ROWS)
    return out.reshape(bsz, seq, d)
```

```python
import functools

import jax
import jax.numpy as jnp
from jax import lax
from jax.experimental import pallas as pl
from jax.experimental.pallas import tpu as pltpu

F32 = jnp.float32
BF16 = jnp.bfloat16

N_META = 16
RET_HEADS = 8
CHUNK = 128
CONV_K = 3
ROPE_BASE = 10000.0
N_EXPERTS = 32
TOP_K = 4
SWIGLU_LIMIT = 7.0
SWIGLU_ALPHA = 1.702
NORM_EPS = 1e-5
GN_EPS = 1e-6

LANES = 128
SUBLANES = 8
VMEM_LIMIT = 58 * 1024 * 1024

SUB_ROWS = 128
GROUP_SUBS = 9
GROUP_ROWS = GROUP_SUBS * SUB_ROWS
GATE_UP_COLS = 256
DOWN_COLS = 512
ROUTE_LANES = 128
NORM_ROWS = 512
PROJ_ROWS = 512
PROJ_COLS = 1024
ROUTER_ROWS = 256
DISPATCH_ROWS = 256
COMBINE_ROWS = 128
CONV_COLS = 512


def _params(sem, vmem=VMEM_LIMIT):
    return pltpu.CompilerParams(dimension_semantics=sem, vmem_limit_bytes=vmem)


def _prenorm_kernel(x_ref, g_ref, o_ref):
    x = x_ref[...]
    ms = jnp.mean(x * x, axis=-1, keepdims=True)
    o_ref[...] = (x * lax.rsqrt(ms + NORM_EPS) * g_ref[...]).astype(o_ref.dtype)


def _prenorm(x2d, g, tm):
    n, d = x2d.shape
    return pl.pallas_call(
        _prenorm_kernel,
        grid=(n // tm,),
        in_specs=[pl.BlockSpec((tm, d), lambda i: (i, 0)),
                  pl.BlockSpec((1, d), lambda i: (0, 0))],
        out_specs=pl.BlockSpec((tm, d), lambda i: (i, 0)),
        out_shape=jax.ShapeDtypeStruct((n, d), BF16),
        compiler_params=_params(("arbitrary",)),
        name="prenorm",
    )(x2d, g)


def _inproj_kernel(a_ref, am_ref, w_ref, o_ref, om_ref):
    @pl.when(pl.program_id(1) == 0)
    def _():
        om_ref[...] = jnp.dot(am_ref[...], w_ref[...].astype(BF16), preferred_element_type=F32)

    o_ref[...] = jnp.dot(a_ref[...], w_ref[...].astype(BF16), preferred_element_type=F32)


def _inproj(a, a_meta, w, tm, tn):
    n, d = a.shape
    c = w.shape[1]
    nm = a_meta.shape[0]
    return pl.pallas_call(
        _inproj_kernel,
        grid=(c // tn, n // tm),
        in_specs=[pl.BlockSpec((tm, d), lambda j, i: (i, 0)),
                  pl.BlockSpec((nm, d), lambda j, i: (0, 0)),
                  pl.BlockSpec((d, tn), lambda j, i: (0, j))],
        out_specs=[pl.BlockSpec((tm, tn), lambda j, i: (i, j)),
                   pl.BlockSpec((nm, tn), lambda j, i: (0, j))],
        out_shape=[jax.ShapeDtypeStruct((n, c), F32),
                   jax.ShapeDtypeStruct((nm, c), F32)],
        compiler_params=_params(("arbitrary", "arbitrary")),
        name="inproj",
    )(a, a_meta, w)


def _rotary_halves(ref, h, dh, cos, sin):
    half = dh // 2
    t1 = ref[:, h * dh:h * dh + half]
    t2 = ref[:, h * dh + half:(h + 1) * dh]
    return t1 * cos - t2 * sin, t2 * cos + t1 * sin


def _state_update(k1, k2, kdec, v_bf):
    kd = jnp.concatenate([k1 * kdec, k2 * kdec], axis=1).astype(BF16)
    return lax.dot_general(kd, v_bf, (((0,), (0,)), ((), ())), preferred_element_type=F32)


def _meta_kernel(k_ref, v_ref, cc_ref, ch_ref, cos_ref, sin_ref, kdec_ref, s0_ref, u0_ref, *, dh):
    cos = cos_ref[...]
    sin = sin_ref[...]
    scale = dh ** -0.5
    for h in range(RET_HEADS):
        k1, k2 = _rotary_halves(k_ref, h, dh, cos, sin)
        v_bf = v_ref[:, h * dh:(h + 1) * dh].astype(BF16)
        s0_ref[h] = _state_update(k1 * scale, k2 * scale, kdec_ref[h], v_bf)
    u0_ref[...] = cc_ref[CHUNK - SUBLANES:CHUNK, :] * ch_ref[CHUNK - SUBLANES:CHUNK, :]


def _meta_state(pm_pad, cosm, sinm, kdec, r, dh):
    col = lambda cb: pl.BlockSpec((CHUNK, r), lambda i: (0, cb))
    full2 = pl.BlockSpec((CHUNK, LANES), lambda i: (0, 0))
    return pl.pallas_call(
        functools.partial(_meta_kernel, dh=dh),
        grid=(1,),
        in_specs=[col(1), col(2), col(5), col(6), full2, full2,
                  pl.BlockSpec((RET_HEADS, CHUNK, LANES), lambda i: (0, 0, 0))],
        out_specs=[pl.BlockSpec((RET_HEADS, dh, dh), lambda i: (0, 0, 0)),
                   pl.BlockSpec((SUBLANES, r), lambda i: (0, 0))],
        out_shape=[jax.ShapeDtypeStruct((RET_HEADS, dh, dh), F32),
                   jax.ShapeDtypeStruct((SUBLANES, r), F32)],
        compiler_params=_params(("arbitrary",)),
        name="meta_state",
    )(pm_pad, pm_pad, pm_pad, pm_pad, cosm, sinm, kdec)


def _mixer_kernel(cdec_ref, q_ref, k_ref, v_ref, g_ref, cb_ref, cc_ref, ch_ref,
                  cos_ref, sin_ref, dmask_ref, qdec_ref, kdec_ref, cw_ref, s0_ref, u0_ref,
                  mix_ref, state_ref, uext_ref, *, dh, r):
    @pl.when(pl.program_id(1) == 0)
    def _():
        state_ref[...] = s0_ref[...]
        uext_ref[0:SUBLANES, :] = u0_ref[...]

    cos = cos_ref[...]
    sin = sin_ref[...]
    scale = dh ** -0.5
    for h in range(RET_HEADS):
        q1, q2 = _rotary_halves(q_ref, h, dh, cos, sin)
        k1, k2 = _rotary_halves(k_ref, h, dh, cos, sin)
        k1 = k1 * scale
        k2 = k2 * scale
        qb = jnp.concatenate([q1, q2], axis=1).astype(BF16)
        kb = jnp.concatenate([k1, k2], axis=1).astype(BF16)
        v_bf = v_ref[:, h * dh:(h + 1) * dh].astype(BF16)
        scores = lax.dot_general(qb, kb, (((1,), (1,)), ((), ())), preferred_element_type=F32)
        sm = (scores * dmask_ref[h]).astype(BF16)
        inner = jnp.dot(sm, v_bf, preferred_element_type=F32)
        qdec = qdec_ref[h]
        qd = jnp.concatenate([q1 * qdec, q2 * qdec], axis=1).astype(BF16)
        st = state_ref[h]
        cross = jnp.dot(qd, st.astype(BF16), preferred_element_type=F32)
        state_ref[h] = st * cdec_ref[h] + _state_update(k1, k2, kdec_ref[h], v_bf)
        y = inner + cross
        mu = jnp.mean(y, axis=-1, keepdims=True)
        yc = y - mu
        var = jnp.mean(yc * yc, axis=-1, keepdims=True)
        yn = yc * lax.rsqrt(var + GN_EPS)
        gh = g_ref[:, h * dh:(h + 1) * dh]
        mix_ref[:, h * dh:(h + 1) * dh] = (gh * jax.nn.sigmoid(gh) * yn).astype(mix_ref.dtype)

    cw = CONV_COLS
    for cs in range(0, r, cw):
        sl = slice(cs, cs + cw)
        u = cc_ref[:, sl] * ch_ref[:, sl]
        uext_ref[SUBLANES:SUBLANES + CHUNK, sl] = u
        u1 = uext_ref[SUBLANES - 1:SUBLANES - 1 + CHUNK, sl]
        u2 = uext_ref[SUBLANES - 2:SUBLANES - 2 + CHUNK, sl]
        conv = cw_ref[0:1, sl] * u2 + cw_ref[1:2, sl] * u1 + cw_ref[2:3, sl] * u
        mix_ref[:, r + cs:r + cs + cw] = (cb_ref[:, sl] * conv).astype(mix_ref.dtype)
        uext_ref[0:SUBLANES, sl] = uext_ref[CHUNK:CHUNK + SUBLANES, sl]


def _mixer(p, cos, sin, dmask, qdec, kdec, cdec, conv_w, s0, u0, bsz, seq, r, dh):
    nc = seq // CHUNK
    d = 2 * r
    col = lambda cb: pl.BlockSpec((CHUNK, r), lambda b, c: (b * nc + c, cb))
    tab = pl.BlockSpec((CHUNK, LANES), lambda b, c: (c, 0))
    hconst = pl.BlockSpec((RET_HEADS, CHUNK, LANES), lambda b, c: (0, 0, 0))
    return pl.pallas_call(
        functools.partial(_mixer_kernel, dh=dh, r=r),
        grid=(bsz, nc),
        in_specs=[pl.BlockSpec(memory_space=pltpu.SMEM),
                  col(0), col(1), col(2), col(3), col(4), col(5), col(6),
                  tab, tab, hconst, hconst, hconst,
                  pl.BlockSpec((CONV_K, r), lambda b, c: (0, 0)),
                  pl.BlockSpec((RET_HEADS, dh, dh), lambda b, c: (0, 0, 0)),
                  pl.BlockSpec((SUBLANES, r), lambda b, c: (0, 0))],
        out_specs=pl.BlockSpec((CHUNK, d), lambda b, c: (b * nc + c, 0)),
        out_shape=jax.ShapeDtypeStruct((bsz * seq, d), BF16),
        scratch_shapes=[pltpu.VMEM((RET_HEADS, dh, dh), F32),
                        pltpu.VMEM((CHUNK + 2 * SUBLANES, r), F32)],
        compiler_params=_params(("arbitrary", "arbitrary")),
        name="mixer",
    )(cdec, p, p, p, p, p, p, p, cos, sin, dmask, qdec, kdec, conv_w, s0, u0)


def _outproj_kernel(a_ref, w_ref, x_ref, o_ref):
    o_ref[...] = x_ref[...] + jnp.dot(a_ref[...], w_ref[...].astype(BF16), preferred_element_type=F32)


def _outproj(mix, w, x2d, tm, tn):
    n, d = mix.shape
    c = w.shape[1]
    return pl.pallas_call(
        _outproj_kernel,
        grid=(c // tn, n // tm),
        in_specs=[pl.BlockSpec((tm, d), lambda j, i: (i, 0)),
                  pl.BlockSpec((d, tn), lambda j, i: (0, j)),
                  pl.BlockSpec((tm, tn), lambda j, i: (i, j))],
        out_specs=pl.BlockSpec((tm, tn), lambda j, i: (i, j)),
        out_shape=jax.ShapeDtypeStruct((n, c), F32),
        compiler_params=_params(("arbitrary", "arbitrary")),
        name="outproj",
    )(mix, w, x2d)


def _pack_bf16_pairs(x):
    half = x.shape[1] // 2
    bits = lax.bitcast_convert_type(x.astype(BF16).astype(F32), jnp.uint32)
    return bits[:, half:] | (bits[:, :half] >> 16)


def _unpack_bf16_pairs(w):
    lo = lax.bitcast_convert_type(w << 16, F32)
    hi = lax.bitcast_convert_type(w & jnp.uint32(0xFFFF0000), F32)
    return lo.astype(BF16), hi.astype(BF16)


def _router_kernel(h_ref, g_ref, w_ref, b_ref, m_ref, route_ref, cnt_ref, carry_ref, *, tr):
    @pl.when(pl.program_id(0) == 0)
    def _():
        carry_ref[...] = jnp.zeros_like(carry_ref)

    x = h_ref[...]
    ms = jnp.mean(x * x, axis=-1, keepdims=True)
    m = x * lax.rsqrt(ms + NORM_EPS) * g_ref[...]
    m_ref[...] = _pack_bf16_pairs(m)

    logits = jnp.dot(m.astype(BF16), w_ref[...], preferred_element_type=F32) + b_ref[...]

    lane = lax.broadcasted_iota(jnp.int32, (tr, ROUTE_LANES), 1)
    lane_f = lane.astype(F32)
    work = logits
    vals, onehots = [], []
    for _ in range(TOP_K):
        mx = jnp.max(work, axis=-1, keepdims=True)
        idx = jnp.min(jnp.where(work == mx, lane_f, float(ROUTE_LANES)), axis=-1, keepdims=True)
        oh = lane_f == idx
        vals.append(mx)
        onehots.append(oh)
        work = jnp.where(oh, -jnp.inf, work)

    exps = [jnp.exp(v - vals[0]) for v in vals]
    denom = exps[0] + exps[1] + exps[2] + exps[3]
    gates = [e / denom for e in exps]

    chosen = onehots[0] | onehots[1] | onehots[2] | onehots[3]
    cmat = jnp.where(chosen, 1.0, 0.0).astype(BF16)
    row = lax.broadcasted_iota(jnp.int32, (tr, tr), 0)
    colm = lax.broadcasted_iota(jnp.int32, (tr, tr), 1)
    lower = jnp.where(colm < row, 1.0, 0.0).astype(BF16)
    carry = carry_ref[0:1, :]
    rank_e = jnp.dot(lower, cmat, preferred_element_type=F32) + carry
    new_carry = carry + jnp.sum(cmat.astype(F32), axis=0, keepdims=True)
    carry_ref[...] = jnp.broadcast_to(new_carry, carry_ref.shape)
    cnt_ref[...] = jnp.broadcast_to(new_carry, cnt_ref.shape)

    out = jnp.zeros((tr, ROUTE_LANES), F32)
    for k in range(TOP_K):
        idx_k = jnp.sum(jnp.where(onehots[k], lane_f, 0.0), axis=-1, keepdims=True)
        rank_k = jnp.sum(jnp.where(onehots[k], rank_e, 0.0), axis=-1, keepdims=True)
        out = jnp.where(lane == k, gates[k], out)
        out = jnp.where(lane == TOP_K + k, idx_k, out)
        out = jnp.where(lane == 2 * TOP_K + k, rank_k, out)
    route_ref[...] = out


def _router(h2, g, w, b, tr):
    n, d = h2.shape
    return pl.pallas_call(
        functools.partial(_router_kernel, tr=tr),
        grid=(n // tr,),
        in_specs=[pl.BlockSpec((tr, d), lambda i: (i, 0)),
                  pl.BlockSpec((1, d), lambda i: (0, 0)),
                  pl.BlockSpec((d, ROUTE_LANES), lambda i: (0, 0)),
                  pl.BlockSpec((1, ROUTE_LANES), lambda i: (0, 0))],
        out_specs=[pl.BlockSpec((tr, d // 2), lambda i: (i, 0)),
                   pl.BlockSpec((tr, ROUTE_LANES), lambda i: (i, 0)),
                   pl.BlockSpec((SUBLANES, ROUTE_LANES), lambda i: (0, 0))],
        out_shape=[jax.ShapeDtypeStruct((n, d // 2), jnp.uint32),
                   jax.ShapeDtypeStruct((n, ROUTE_LANES), F32),
                   jax.ShapeDtypeStruct((SUBLANES, ROUTE_LANES), F32)],
        scratch_shapes=[pltpu.VMEM((SUBLANES, ROUTE_LANES), F32)],
        compiler_params=_params(("arbitrary",)),
        name="router",
    )(h2, g, w, b)


def _row_copy(src_ref, src_row, dst_ref, dst_row, sem):
    return pltpu.make_async_copy(src_ref.at[pl.ds(src_row, 1), :], dst_ref.at[pl.ds(dst_row, 1), :], sem)


def _dispatch_kernel(dest_ref, zlo_ref, zhi_ref, m_ref, xg_ref, zero_ref, sem, zsem, *, tt, n_steps):
    @pl.when(pl.program_id(0) == 0)
    def _():
        zero_ref[...] = jnp.zeros_like(zero_ref)

        def range_body(e, carry):
            def zero_start(rr, c):
                _row_copy(zero_ref, 0, xg_ref, rr, zsem).start()
                return c

            lax.fori_loop(zlo_ref[e], zhi_ref[e], zero_start, 0)
            return carry

        lax.fori_loop(0, N_EXPERTS + 1, range_body, 0)

    @pl.when(pl.program_id(0) == pl.num_programs(0) - 1)
    def _():
        n_zero = xg_ref.shape[0] - TOP_K * tt * n_steps
        pltpu.make_async_copy(xg_ref.at[pl.ds(0, n_zero), :], xg_ref.at[pl.ds(0, n_zero), :], zsem).wait()

    def start_body(t, c):
        for k in range(TOP_K):
            _row_copy(m_ref, t, xg_ref, dest_ref[t * TOP_K + k], sem).start()
        return c

    lax.fori_loop(0, tt, start_body, 0)
    for k in range(TOP_K):
        pltpu.make_async_copy(m_ref, xg_ref.at[pl.ds(0, tt), :], sem).wait()


def _dispatch(dest_flat, zlo, zhi, m, rows, tt):
    n, d = m.shape
    return pl.pallas_call(
        functools.partial(_dispatch_kernel, tt=tt, n_steps=n // tt),
        grid=(n // tt,),
        in_specs=[pl.BlockSpec((tt * TOP_K,), lambda i: (i,), memory_space=pltpu.SMEM),
                  pl.BlockSpec(memory_space=pltpu.SMEM),
                  pl.BlockSpec(memory_space=pltpu.SMEM),
                  pl.BlockSpec((tt, d), lambda i: (i, 0))],
        out_specs=pl.BlockSpec(memory_space=pl.ANY),
        out_shape=jax.ShapeDtypeStruct((rows, d), m.dtype),
        scratch_shapes=[pltpu.VMEM((SUBLANES, d), m.dtype),
                        pltpu.SemaphoreType.DMA(()),
                        pltpu.SemaphoreType.DMA(())],
        compiler_params=_params(("arbitrary",)),
        name="dispatch",
    )(dest_flat, zlo, zhi, m)


def _for_row_blocks(nsub, block_fn):
    n8 = nsub // 8
    rem8 = nsub - n8 * 8
    merged = jnp.where((rem8 == 1) & (n8 >= 1), 1, 0)

    def body(i, c):
        start = pl.multiple_of(i * (8 * SUB_ROWS), 8 * SUB_ROWS)
        block_fn(start, 4 * SUB_ROWS)
        block_fn(start + 4 * SUB_ROWS, 4 * SUB_ROWS)
        return c

    lax.fori_loop(0, n8 - merged, body, 0)

    @pl.when(merged == 1)
    def _():
        start = pl.multiple_of((n8 - 1) * (8 * SUB_ROWS), 8 * SUB_ROWS)
        block_fn(start, 4 * SUB_ROWS)
        block_fn(start + 4 * SUB_ROWS, 5 * SUB_ROWS)

    rem8 = rem8 * (1 - merged)
    base8 = n8 * (8 * SUB_ROWS)

    @pl.when(rem8 >= 4)
    def _():
        block_fn(pl.multiple_of(base8, 4 * SUB_ROWS), 4 * SUB_ROWS)

    rem = rem8 % 4
    base = base8 + (rem8 // 4) * (4 * SUB_ROWS)

    @pl.when(rem >= 2)
    def _():
        block_fn(pl.multiple_of(base, SUB_ROWS), 2 * SUB_ROWS)

    @pl.when(rem % 2 == 1)
    def _():
        block_fn(pl.multiple_of(base + (rem // 2) * (2 * SUB_ROWS), SUB_ROWS), SUB_ROWS)


def _expert_kernel(gexp_ref, gsub_ref, gns_ref, tail_ref,
                   xg_ref, wg_ref, wu_ref, bg_ref, bu_ref, wd_ref, bd_ref, yg_ref,
                   xb_ref, hid_ref, stage_ref, ybuf_ref, pend_ref,
                   xsem, ysem, *, nj, nn):
    g = pl.program_id(0)
    s = pl.program_id(1)
    tf = GATE_UP_COLS
    tn = DOWN_COLS
    ns = gns_ref[g]
    row0 = gsub_ref[g] * SUB_ROWS

    def aligned(v, m):
        return v if isinstance(v, int) else pl.multiple_of(v, m)

    def y_copy(sub, slot, col, first_row):
        return pltpu.make_async_copy(
            ybuf_ref.at[slot, pl.ds(aligned(sub * SUB_ROWS, SUB_ROWS), SUB_ROWS), :],
            yg_ref.at[pl.ds(aligned(first_row + sub * SUB_ROWS, SUB_ROWS), SUB_ROWS),
                      pl.ds(aligned(col * tn, tn), tn)],
            ysem.at[slot])

    def x_copy(first_row, sub, slot):
        return pltpu.make_async_copy(
            xg_ref.at[pl.ds(aligned(first_row + sub * SUB_ROWS, SUB_ROWS), SUB_ROWS), :],
            stage_ref.at[slot], xsem.at[slot])

    def stage_to_rows(buf, sub, slot):
        rows = pl.ds(aligned(sub * SUB_ROWS, SUB_ROWS), SUB_ROWS)
        lo, hi = _unpack_bf16_pairs(stage_ref[slot])
        half = lo.shape[1]
        xb_ref[buf, rows, 0:half] = lo
        xb_ref[buf, rows, half:2 * half] = hi

    def wait_pending(slot):
        def body(i, c):
            y_copy(0, slot, 0, 0).wait()
            return c

        lax.fori_loop(0, pend_ref[slot], body, 0)
        pend_ref[slot] = 0

    @pl.when((g == 0) & (s == 0))
    def _():
        pend_ref[0] = 0
        pend_ref[1] = 0
        ybuf_ref[0, 0:SUB_ROWS, :] = jnp.zeros((SUB_ROWS, tn), F32)
        lo = tail_ref[0]
        hi = tail_ref[1]

        def start_body(i, c):
            for col in range(nn):
                y_copy(0, 0, col, i * SUB_ROWS).start()
            return c

        def wait_body(i, c):
            for col in range(nn):
                y_copy(0, 0, col, i * SUB_ROWS).wait()
            return c

        lax.fori_loop(lo, hi, start_body, 0)
        lax.fori_loop(lo, hi, wait_body, 0)

        x_copy(row0, 0, 0).start()

        def first_body(sub, c):
            slot = lax.rem(sub, 2)

            @pl.when(sub + 1 < ns)
            def _():
                x_copy(row0, sub + 1, 1 - slot).start()

            x_copy(row0, sub, slot).wait()
            stage_to_rows(0, sub, slot)
            return c

        lax.fori_loop(0, ns, first_body, 0)

    @pl.when(ns > 0)
    def _():
        @pl.when(s < nj)
        def _():
            bg = bg_ref[gexp_ref[g], pl.ds(s, 1), :]
            bu = bu_ref[gexp_ref[g], pl.ds(s, 1), :]
            cur = lax.rem(g, 2)

            last_g = pl.num_programs(0) - 1
            g_next = jnp.minimum(g + 1, last_g)
            ns_next = jnp.where(g < last_g, gns_ref[g_next], 0)
            row0_next = gsub_ref[g_next] * SUB_ROWS
            for k in range(2):
                @pl.when(2 * s + k < ns_next)
                def _():
                    x_copy(row0_next, 2 * s + k, k).start()

            def block(start, rows_n):
                rows = pl.ds(start, rows_n)
                x = xb_ref[cur, rows, :]
                gt = jnp.dot(x, wg_ref[0].astype(BF16), preferred_element_type=F32) + bg
                up = jnp.dot(x, wu_ref[0].astype(BF16), preferred_element_type=F32) + bu
                gt = jnp.minimum(gt, SWIGLU_LIMIT)
                up = jnp.clip(up, -SWIGLU_LIMIT, SWIGLU_LIMIT)
                hid = (up + 1.0) * gt * jax.nn.sigmoid(SWIGLU_ALPHA * gt)
                hid_ref[s, rows, :] = hid.astype(hid_ref.dtype)

            _for_row_blocks(ns, block)

            for k in range(2):
                @pl.when(2 * s + k < ns_next)
                def _():
                    x_copy(row0_next, 2 * s + k, k).wait()
                    stage_to_rows(1 - cur, 2 * s + k, k)

        @pl.when(s >= nj)
        def _():
            col = s - nj
            slot = lax.rem(col, 2)
            bd = bd_ref[gexp_ref[g], pl.ds(col, 1), :]
            wait_pending(slot)

            def block(start, rows_n):
                rows = pl.ds(start, rows_n)
                acc = jnp.dot(hid_ref[0, rows, :], wd_ref[0, 0:tf, :].astype(BF16),
                              preferred_element_type=F32)
                for c in range(1, nj):
                    acc = acc + jnp.dot(hid_ref[c, rows, :], wd_ref[0, c * tf:(c + 1) * tf, :].astype(BF16),
                                        preferred_element_type=F32)
                ybuf_ref[slot, rows, :] = acc + bd

            _for_row_blocks(ns, block)

            def start_body(sub, c):
                y_copy(sub, slot, col, row0).start()
                return c

            lax.fori_loop(0, ns, start_body, 0)
            pend_ref[slot] = ns

    @pl.when((g == pl.num_programs(0) - 1) & (s == nj + nn - 1))
    def _():
        wait_pending(0)
        wait_pending(1)


def _experts(gexp, gsub, gns, tail, xg, w_gate, w_up, b_gate, b_up, w_down, b_down, n_groups):
    rows = xg.shape[0]
    d = w_gate.shape[1]
    f = w_gate.shape[2]
    tf, tn = GATE_UP_COLS, DOWN_COLS
    nj, nn = f // tf, d // tn
    assert GROUP_SUBS <= 2 * nj, "next group's rows are fetched two sub-blocks per gate/up step"

    def up_idx(g, s, ge, gs, gn, tl):
        act = jnp.minimum(gn[g], 1)
        return (ge[g], 0, jnp.minimum(s, nj - 1) * act + (nj - 1) * (1 - act))

    def down_idx(g, s, ge, gs, gn, tl):
        act = jnp.minimum(gn[g], 1)
        return (ge[g], 0, jnp.maximum(s - nj, 0) * act + (nn - 1) * (1 - act))

    def whole(g, s, ge, gs, gn, tl):
        return (0, 0, 0)

    grid_spec = pltpu.PrefetchScalarGridSpec(
        num_scalar_prefetch=4,
        grid=(n_groups, nj + nn),
        in_specs=[
            pl.BlockSpec(memory_space=pl.ANY),
            pl.BlockSpec((1, d, tf), up_idx),
            pl.BlockSpec((1, d, tf), up_idx),
            pl.BlockSpec((N_EXPERTS, nj, tf), whole),
            pl.BlockSpec((N_EXPERTS, nj, tf), whole),
            pl.BlockSpec((1, f, tn), down_idx),
            pl.BlockSpec((N_EXPERTS, nn, tn), whole),
        ],
        out_specs=pl.BlockSpec(memory_space=pl.ANY),
        scratch_shapes=[pltpu.VMEM((2, GROUP_ROWS, d), BF16),
                        pltpu.VMEM((nj, GROUP_ROWS, tf), BF16),
                        pltpu.VMEM((2, SUB_ROWS, d // 2), xg.dtype),
                        pltpu.VMEM((2, GROUP_ROWS, tn), F32),
                        pltpu.SMEM((2,), jnp.int32),
                        pltpu.SemaphoreType.DMA((2,)),
                        pltpu.SemaphoreType.DMA((2,))],
    )
    return pl.pallas_call(
        functools.partial(_expert_kernel, nj=nj, nn=nn),
        grid_spec=grid_spec,
        out_shape=jax.ShapeDtypeStruct((rows, d), F32),
        compiler_params=_params(("arbitrary", "arbitrary")),
        name="experts",
    )(gexp, gsub, gns, tail, xg, w_gate, w_up, b_gate.reshape(N_EXPERTS, nj, tf),
      b_up.reshape(N_EXPERTS, nj, tf), w_down, b_down.reshape(N_EXPERTS, nn, tn))


def _combine_kernel(dcur_ref, dnext_ref, route_ref, h_ref, g_ref, yg_ref, o_ref, buf_ref, sem, *, tt):
    i = pl.program_id(0)
    n_steps = pl.num_programs(0)
    slot = lax.rem(i, 2)

    def start_tile(dref, sl):
        def body(t, c):
            for k in range(TOP_K):
                pltpu.make_async_copy(yg_ref.at[pl.ds(dref[t * TOP_K + k], 1), :],
                                      buf_ref.at[sl, k, pl.ds(t, 1), :], sem.at[sl]).start()
            return c

        lax.fori_loop(0, tt, body, 0, unroll=4)

    @pl.when(i == 0)
    def _():
        start_tile(dcur_ref, 0)

    @pl.when(i + 1 < n_steps)
    def _():
        start_tile(dnext_ref, 1 - slot)

    for k in range(TOP_K):
        pltpu.make_async_copy(yg_ref.at[pl.ds(0, tt), :], buf_ref.at[slot, k], sem.at[slot]).wait()

    d = o_ref.shape[1]
    cw = 2 * LANES
    gates = [route_ref[:, k:k + 1] for k in range(TOP_K)]
    ssq = jnp.zeros((tt, 1), F32)
    for cs in range(0, d, cw):
        acc = h_ref[:, cs:cs + cw]
        for k in range(TOP_K):
            acc = acc + buf_ref[slot, k, :, cs:cs + cw] * gates[k]
        o_ref[:, cs:cs + cw] = acc
        ssq = ssq + jnp.sum(acc * acc, axis=-1, keepdims=True)
    scale = lax.rsqrt(ssq * (1.0 / d) + NORM_EPS)
    for cs in range(0, d, cw):
        o_ref[:, cs:cs + cw] = o_ref[:, cs:cs + cw] * scale * g_ref[:, cs:cs + cw]


def _combine(dest_flat, route, h2, g, yg, tt):
    n, d = h2.shape
    steps = n // tt
    return pl.pallas_call(
        functools.partial(_combine_kernel, tt=tt),
        grid=(steps,),
        in_specs=[pl.BlockSpec((tt * TOP_K,), lambda i: (i,), memory_space=pltpu.SMEM),
                  pl.BlockSpec((tt * TOP_K,), lambda i: (jnp.minimum(i + 1, steps - 1),),
                               memory_space=pltpu.SMEM),
                  pl.BlockSpec((tt, ROUTE_LANES), lambda i: (i, 0)),
                  pl.BlockSpec((tt, d), lambda i: (i, 0)),
                  pl.BlockSpec((1, d), lambda i: (0, 0)),
                  pl.BlockSpec(memory_space=pl.ANY)],
        out_specs=pl.BlockSpec((tt, d), lambda i: (i, 0)),
        out_shape=jax.ShapeDtypeStruct((n, d), F32),
        scratch_shapes=[pltpu.VMEM((2, TOP_K, tt, d), F32),
                        pltpu.SemaphoreType.DMA((2,))],
        compiler_params=_params(("arbitrary",)),
        name="combine",
    )(dest_flat, dest_flat, route, h2, g, yg)


def _retention_tables(dh):
    lg = jnp.log1p(-jnp.exp2(-5.0 - jnp.arange(RET_HEADS, dtype=F32)))
    i = jnp.arange(CHUNK, dtype=F32)
    rel = i[:, None] - i[None, :]
    causal = rel >= 0
    dmask = jnp.where(causal[None], jnp.exp(jnp.where(causal, rel, 0.0)[None] * lg[:, None, None]), 0.0)
    q_dec = jnp.exp((i + 1.0)[None, :] * lg[:, None])
    k_dec = jnp.exp((CHUNK - 1.0 - i)[None, :] * lg[:, None])
    c_dec = jnp.exp(CHUNK * lg)
    bcast = lambda t: jnp.broadcast_to(t[:, :, None], (RET_HEADS, CHUNK, LANES))
    return dmask, bcast(q_dec), bcast(k_dec), c_dec


def _rotary_tables(length, dh):
    half = dh // 2
    inv = ROPE_BASE ** (-jnp.arange(half, dtype=F32) / half)
    ang = jnp.arange(length, dtype=jnp.int32).astype(F32)[:, None] * inv[None, :]
    return jnp.cos(ang), jnp.sin(ang)


def kernel(x, meta_tokens, norm_mix, w_in, conv_w, w_out, norm_ffn, router_w, router_b,
           w_gate, b_gate, w_up, b_up, w_down, b_down, norm_final):
    bsz, seq, d = x.shape
    assert w_in.shape[0] == 1, "single-layer stack expected"
    assert meta_tokens.shape[0] == N_META and seq % CHUNK == 0
    r = d // 2
    dh = r // RET_HEADS
    assert dh // 2 == LANES
    n = bsz * seq
    assert all(n % t == 0 for t in (NORM_ROWS, PROJ_ROWS, ROUTER_ROWS, DISPATCH_ROWS, COMBINE_ROWS))
    assert d % PROJ_COLS == 0 and w_in.shape[2] % PROJ_COLS == 0 and r % CONV_COLS == 0
    assert w_gate.shape[3] % GATE_UP_COLS == 0 and d % DOWN_COLS == 0
    assert router_w.shape[2] == N_EXPERTS and w_gate.shape[1] == N_EXPERTS
    x2d = x.reshape(n, d)

    a = _prenorm(x2d, norm_mix[0][None, :], NORM_ROWS)
    a_meta = _prenorm(meta_tokens.astype(x.dtype), norm_mix[0][None, :], N_META)
    p, pm = _inproj(a, a_meta, w_in[0], PROJ_ROWS, PROJ_COLS)

    cos, sin = _rotary_tables(N_META + seq, dh)
    pad_meta = lambda t: jnp.pad(t, ((CHUNK - N_META, 0), (0, 0)))
    dmask, qdec, kdec, cdec = _retention_tables(dh)
    s0, u0 = _meta_state(pad_meta(pm), pad_meta(cos[:N_META]), pad_meta(sin[:N_META]), kdec, r, dh)
    mix = _mixer(p, cos[N_META:], sin[N_META:], dmask, qdec, kdec, cdec, conv_w[0], s0, u0,
                 bsz, seq, r, dh)
    h2 = _outproj(mix, w_out[0], x2d, PROJ_ROWS, PROJ_COLS)

    rw = jnp.pad(router_w[0], ((0, 0), (0, ROUTE_LANES - N_EXPERTS))).astype(BF16)
    rb = jnp.pad(router_b[0].astype(F32), (0, ROUTE_LANES - N_EXPERTS), constant_values=-jnp.inf)[None, :]
    m, route, cnt = _router(h2, norm_ffn[0][None, :], rw, rb, ROUTER_ROWS)

    side = route[:, TOP_K:3 * TOP_K].T.astype(jnp.int32)
    ids, rank = side[:TOP_K], side[TOP_K:]
    counts = cnt[0, :N_EXPERTS].astype(jnp.int32)

    total_sub = (n * TOP_K + N_EXPERTS * (SUB_ROWS - 1) + SUB_ROWS - 1) // SUB_ROWS
    psub = (counts + SUB_ROWS - 1) // SUB_ROWS
    pend = jnp.cumsum(psub) * SUB_ROWS
    pstart = pend - psub * SUB_ROWS
    first_row = jnp.zeros_like(ids)
    for e in range(N_EXPERTS):
        first_row = jnp.where(ids == e, pstart[e], first_row)
    dest_flat = (first_row + rank).T.reshape(-1).astype(jnp.int32)
    zlo = jnp.concatenate([pstart + counts, pend[-1:]]).astype(jnp.int32)
    zhi = jnp.concatenate([pend, jnp.full((1,), total_sub * SUB_ROWS)]).astype(jnp.int32)

    n_groups_max = N_EXPERTS + total_sub // GROUP_SUBS
    ngrp = (psub + GROUP_SUBS - 1) // GROUP_SUBS
    gend = jnp.cumsum(ngrp)
    gstart = gend - ngrp
    n_groups = gend[-1]
    gidx = jnp.arange(n_groups_max, dtype=jnp.int32)
    glast = jnp.minimum(gidx, n_groups - 1)
    gexp = jnp.minimum(jnp.searchsorted(gend, glast, side="right"), N_EXPERTS - 1).astype(jnp.int32)
    gq = glast - gstart[gexp]
    gsub = (pstart[gexp] // SUB_ROWS + gq * GROUP_SUBS).astype(jnp.int32)
    gns = jnp.where(gidx < n_groups, jnp.minimum(psub[gexp] - gq * GROUP_SUBS, GROUP_SUBS), 0).astype(jnp.int32)
    tail = jnp.stack([pend[-1] // SUB_ROWS, jnp.asarray(total_sub, pend.dtype)]).astype(jnp.int32)

    xg = _dispatch(dest_flat, zlo, zhi, m, total_sub * SUB_ROWS, DISPATCH_ROWS)
    yg = _experts(gexp, gsub, gns, tail, xg, w_gate[0], w_up[0], b_gate[0], b_up[0],
                  w_down[0], b_down[0], n_groups.astype(jnp.int32))
    out = _combine(dest_flat, route, h2, norm_final[None, :], yg, COMBINE_ROWS)
    return out.reshape(bsz, seq, d)
```

```python
import functools

import jax
import jax.numpy as jnp
from jax import lax
from jax.experimental import pallas as pl
from jax.experimental.pallas import tpu as pltpu

F32 = jnp.float32
BF16 = jnp.bfloat16

N_META = 16
RET_HEADS = 8
CHUNK = 128
CONV_K = 3
ROPE_BASE = 10000.0
N_EXPERTS = 32
TOP_K = 4
SWIGLU_LIMIT = 7.0
SWIGLU_ALPHA = 1.702
NORM_EPS = 1e-5
GN_EPS = 1e-6

LANES = 128
SUBLANES = 8
VMEM_LIMIT = 58 * 1024 * 1024

SUB_ROWS = 128
GROUP_SUBS = 9
GROUP_ROWS = GROUP_SUBS * SUB_ROWS
GATE_UP_COLS = 256
DOWN_COLS = 1024
PREFETCH_SUBS = 3
EXPERT_VMEM_LIMIT = 62 * 1024 * 1024
ROUTE_LANES = 128
NORM_ROWS = 512
PROJ_ROWS = 512
PROJ_COLS = 1024
ROUTER_ROWS = 256
DISPATCH_ROWS = 256
COMBINE_ROWS = 128
CONV_COLS = 512


def _params(sem, vmem=VMEM_LIMIT):
    return pltpu.CompilerParams(dimension_semantics=sem, vmem_limit_bytes=vmem)


def _prenorm_kernel(x_ref, g_ref, o_ref):
    x = x_ref[...]
    ms = jnp.mean(x * x, axis=-1, keepdims=True)
    o_ref[...] = (x * lax.rsqrt(ms + NORM_EPS) * g_ref[...]).astype(o_ref.dtype)


def _prenorm(x2d, g, tm):
    n, d = x2d.shape
    return pl.pallas_call(
        _prenorm_kernel,
        grid=(n // tm,),
        in_specs=[pl.BlockSpec((tm, d), lambda i: (i, 0)),
                  pl.BlockSpec((1, d), lambda i: (0, 0))],
        out_specs=pl.BlockSpec((tm, d), lambda i: (i, 0)),
        out_shape=jax.ShapeDtypeStruct((n, d), BF16),
        compiler_params=_params(("arbitrary",)),
        name="prenorm",
    )(x2d, g)


def _inproj_kernel(a_ref, am_ref, w_ref, o_ref, om_ref):
    @pl.when(pl.program_id(1) == 0)
    def _():
        om_ref[...] = jnp.dot(am_ref[...], w_ref[...].astype(BF16), preferred_element_type=F32)

    o_ref[...] = jnp.dot(a_ref[...], w_ref[...].astype(BF16), preferred_element_type=F32)


def _inproj(a, a_meta, w, tm, tn):
    n, d = a.shape
    c = w.shape[1]
    nm = a_meta.shape[0]
    return pl.pallas_call(
        _inproj_kernel,
        grid=(c // tn, n // tm),
        in_specs=[pl.BlockSpec((tm, d), lambda j, i: (i, 0)),
                  pl.BlockSpec((nm, d), lambda j, i: (0, 0)),
                  pl.BlockSpec((d, tn), lambda j, i: (0, j))],
        out_specs=[pl.BlockSpec((tm, tn), lambda j, i: (i, j)),
                   pl.BlockSpec((nm, tn), lambda j, i: (0, j))],
        out_shape=[jax.ShapeDtypeStruct((n, c), F32),
                   jax.ShapeDtypeStruct((nm, c), F32)],
        compiler_params=_params(("arbitrary", "arbitrary")),
        name="inproj",
    )(a, a_meta, w)


def _rotary_halves(ref, h, dh, cos, sin):
    half = dh // 2
    t1 = ref[:, h * dh:h * dh + half]
    t2 = ref[:, h * dh + half:(h + 1) * dh]
    return t1 * cos - t2 * sin, t2 * cos + t1 * sin


def _state_update(k1, k2, kdec, v_bf):
    kd = jnp.concatenate([k1 * kdec, k2 * kdec], axis=1).astype(BF16)
    return lax.dot_general(kd, v_bf, (((0,), (0,)), ((), ())), preferred_element_type=F32)


def _meta_kernel(k_ref, v_ref, cc_ref, ch_ref, cos_ref, sin_ref, kdec_ref, s0_ref, u0_ref, *, dh):
    cos = cos_ref[...]
    sin = sin_ref[...]
    scale = dh ** -0.5
    for h in range(RET_HEADS):
        k1, k2 = _rotary_halves(k_ref, h, dh, cos, sin)
        v_bf = v_ref[:, h * dh:(h + 1) * dh].astype(BF16)
        s0_ref[h] = _state_update(k1 * scale, k2 * scale, kdec_ref[h], v_bf)
    u0_ref[...] = cc_ref[CHUNK - SUBLANES:CHUNK, :] * ch_ref[CHUNK - SUBLANES:CHUNK, :]


def _meta_state(pm_pad, cosm, sinm, kdec, r, dh):
    col = lambda cb: pl.BlockSpec((CHUNK, r), lambda i: (0, cb))
    full2 = pl.BlockSpec((CHUNK, LANES), lambda i: (0, 0))
    return pl.pallas_call(
        functools.partial(_meta_kernel, dh=dh),
        grid=(1,),
        in_specs=[col(1), col(2), col(5), col(6), full2, full2,
                  pl.BlockSpec((RET_HEADS, CHUNK, LANES), lambda i: (0, 0, 0))],
        out_specs=[pl.BlockSpec((RET_HEADS, dh, dh), lambda i: (0, 0, 0)),
                   pl.BlockSpec((SUBLANES, r), lambda i: (0, 0))],
        out_shape=[jax.ShapeDtypeStruct((RET_HEADS, dh, dh), F32),
                   jax.ShapeDtypeStruct((SUBLANES, r), F32)],
        compiler_params=_params(("arbitrary",)),
        name="meta_state",
    )(pm_pad, pm_pad, pm_pad, pm_pad, cosm, sinm, kdec)


def _mixer_kernel(cdec_ref, q_ref, k_ref, v_ref, g_ref, cb_ref, cc_ref, ch_ref,
                  cos_ref, sin_ref, dmask_ref, qdec_ref, kdec_ref, cw_ref, s0_ref, u0_ref,
                  mix_ref, state_ref, uext_ref, *, dh, r):
    @pl.when(pl.program_id(1) == 0)
    def _():
        state_ref[...] = s0_ref[...]
        uext_ref[0:SUBLANES, :] = u0_ref[...]

    cos = cos_ref[...]
    sin = sin_ref[...]
    scale = dh ** -0.5
    for h in range(RET_HEADS):
        q1, q2 = _rotary_halves(q_ref, h, dh, cos, sin)
        k1, k2 = _rotary_halves(k_ref, h, dh, cos, sin)
        k1 = k1 * scale
        k2 = k2 * scale
        qb = jnp.concatenate([q1, q2], axis=1).astype(BF16)
        kb = jnp.concatenate([k1, k2], axis=1).astype(BF16)
        v_bf = v_ref[:, h * dh:(h + 1) * dh].astype(BF16)
        scores = lax.dot_general(qb, kb, (((1,), (1,)), ((), ())), preferred_element_type=F32)
        sm = (scores * dmask_ref[h]).astype(BF16)
        inner = jnp.dot(sm, v_bf, preferred_element_type=F32)
        qdec = qdec_ref[h]
        qd = jnp.concatenate([q1 * qdec, q2 * qdec], axis=1).astype(BF16)
        st = state_ref[h]
        cross = jnp.dot(qd, st.astype(BF16), preferred_element_type=F32)
        state_ref[h] = st * cdec_ref[h] + _state_update(k1, k2, kdec_ref[h], v_bf)
        y = inner + cross
        mu = jnp.mean(y, axis=-1, keepdims=True)
        yc = y - mu
        var = jnp.mean(yc * yc, axis=-1, keepdims=True)
        yn = yc * lax.rsqrt(var + GN_EPS)
        gh = g_ref[:, h * dh:(h + 1) * dh]
        mix_ref[:, h * dh:(h + 1) * dh] = (gh * jax.nn.sigmoid(gh) * yn).astype(mix_ref.dtype)

    cw = CONV_COLS
    for cs in range(0, r, cw):
        sl = slice(cs, cs + cw)
        u = cc_ref[:, sl] * ch_ref[:, sl]
        uext_ref[SUBLANES:SUBLANES + CHUNK, sl] = u
        u1 = uext_ref[SUBLANES - 1:SUBLANES - 1 + CHUNK, sl]
        u2 = uext_ref[SUBLANES - 2:SUBLANES - 2 + CHUNK, sl]
        conv = cw_ref[0:1, sl] * u2 + cw_ref[1:2, sl] * u1 + cw_ref[2:3, sl] * u
        mix_ref[:, r + cs:r + cs + cw] = (cb_ref[:, sl] * conv).astype(mix_ref.dtype)
        uext_ref[0:SUBLANES, sl] = uext_ref[CHUNK:CHUNK + SUBLANES, sl]


def _mixer(p, cos, sin, dmask, qdec, kdec, cdec, conv_w, s0, u0, bsz, seq, r, dh):
    nc = seq // CHUNK
    d = 2 * r
    col = lambda cb: pl.BlockSpec((CHUNK, r), lambda b, c: (b * nc + c, cb))
    tab = pl.BlockSpec((CHUNK, LANES), lambda b, c: (c, 0))
    hconst = pl.BlockSpec((RET_HEADS, CHUNK, LANES), lambda b, c: (0, 0, 0))
    return pl.pallas_call(
        functools.partial(_mixer_kernel, dh=dh, r=r),
        grid=(bsz, nc),
        in_specs=[pl.BlockSpec(memory_space=pltpu.SMEM),
                  col(0), col(1), col(2), col(3), col(4), col(5), col(6),
                  tab, tab, hconst, hconst, hconst,
                  pl.BlockSpec((CONV_K, r), lambda b, c: (0, 0)),
                  pl.BlockSpec((RET_HEADS, dh, dh), lambda b, c: (0, 0, 0)),
                  pl.BlockSpec((SUBLANES, r), lambda b, c: (0, 0))],
        out_specs=pl.BlockSpec((CHUNK, d), lambda b, c: (b * nc + c, 0)),
        out_shape=jax.ShapeDtypeStruct((bsz * seq, d), BF16),
        scratch_shapes=[pltpu.VMEM((RET_HEADS, dh, dh), F32),
                        pltpu.VMEM((CHUNK + 2 * SUBLANES, r), F32)],
        compiler_params=_params(("arbitrary", "arbitrary")),
        name="mixer",
    )(cdec, p, p, p, p, p, p, p, cos, sin, dmask, qdec, kdec, conv_w, s0, u0)


def _outproj_kernel(a_ref, w_ref, x_ref, o_ref):
    o_ref[...] = x_ref[...] + jnp.dot(a_ref[...], w_ref[...].astype(BF16), preferred_element_type=F32)


def _outproj(mix, w, x2d, tm, tn):
    n, d = mix.shape
    c = w.shape[1]
    return pl.pallas_call(
        _outproj_kernel,
        grid=(c // tn, n // tm),
        in_specs=[pl.BlockSpec((tm, d), lambda j, i: (i, 0)),
                  pl.BlockSpec((d, tn), lambda j, i: (0, j)),
                  pl.BlockSpec((tm, tn), lambda j, i: (i, j))],
        out_specs=pl.BlockSpec((tm, tn), lambda j, i: (i, j)),
        out_shape=jax.ShapeDtypeStruct((n, c), F32),
        compiler_params=_params(("arbitrary", "arbitrary")),
        name="outproj",
    )(mix, w, x2d)


def _pack_bf16_pairs(x):
    half = x.shape[1] // 2
    bits = lax.bitcast_convert_type(x.astype(BF16).astype(F32), jnp.uint32)
    return bits[:, half:] | (bits[:, :half] >> 16)


def _unpack_bf16_pairs(w):
    lo = lax.bitcast_convert_type(w << 16, F32)
    hi = lax.bitcast_convert_type(w & jnp.uint32(0xFFFF0000), F32)
    return lo.astype(BF16), hi.astype(BF16)


def _router_kernel(h_ref, g_ref, w_ref, b_ref, m_ref, route_ref, cnt_ref, carry_ref, *, tr):
    @pl.when(pl.program_id(0) == 0)
    def _():
        carry_ref[...] = jnp.zeros_like(carry_ref)

    x = h_ref[...]
    ms = jnp.mean(x * x, axis=-1, keepdims=True)
    m = x * lax.rsqrt(ms + NORM_EPS) * g_ref[...]
    m_ref[...] = _pack_bf16_pairs(m)

    logits = jnp.dot(m.astype(BF16), w_ref[...], preferred_element_type=F32) + b_ref[...]

    lane = lax.broadcasted_iota(jnp.int32, (tr, ROUTE_LANES), 1)
    lane_f = lane.astype(F32)
    work = logits
    vals, onehots = [], []
    for _ in range(TOP_K):
        mx = jnp.max(work, axis=-1, keepdims=True)
        idx = jnp.min(jnp.where(work == mx, lane_f, float(ROUTE_LANES)), axis=-1, keepdims=True)
        oh = lane_f == idx
        vals.append(mx)
        onehots.append(oh)
        work = jnp.where(oh, -jnp.inf, work)

    exps = [jnp.exp(v - vals[0]) for v in vals]
    denom = exps[0] + exps[1] + exps[2] + exps[3]
    gates = [e / denom for e in exps]

    chosen = onehots[0] | onehots[1] | onehots[2] | onehots[3]
    cmat = jnp.where(chosen, 1.0, 0.0).astype(BF16)
    row = lax.broadcasted_iota(jnp.int32, (tr, tr), 0)
    colm = lax.broadcasted_iota(jnp.int32, (tr, tr), 1)
    lower = jnp.where(colm < row, 1.0, 0.0).astype(BF16)
    carry = carry_ref[0:1, :]
    rank_e = jnp.dot(lower, cmat, preferred_element_type=F32) + carry
    new_carry = carry + jnp.sum(cmat.astype(F32), axis=0, keepdims=True)
    carry_ref[...] = jnp.broadcast_to(new_carry, carry_ref.shape)
    cnt_ref[...] = jnp.broadcast_to(new_carry, cnt_ref.shape)

    out = jnp.zeros((tr, ROUTE_LANES), F32)
    for k in range(TOP_K):
        idx_k = jnp.sum(jnp.where(onehots[k], lane_f, 0.0), axis=-1, keepdims=True)
        rank_k = jnp.sum(jnp.where(onehots[k], rank_e, 0.0), axis=-1, keepdims=True)
        out = jnp.where(lane == k, gates[k], out)
        out = jnp.where(lane == TOP_K + k, idx_k, out)
        out = jnp.where(lane == 2 * TOP_K + k, rank_k, out)
    route_ref[...] = out


def _router(h2, g, w, b, tr):
    n, d = h2.shape
    return pl.pallas_call(
        functools.partial(_router_kernel, tr=tr),
        grid=(n // tr,),
        in_specs=[pl.BlockSpec((tr, d), lambda i: (i, 0)),
                  pl.BlockSpec((1, d), lambda i: (0, 0)),
                  pl.BlockSpec((d, ROUTE_LANES), lambda i: (0, 0)),
                  pl.BlockSpec((1, ROUTE_LANES), lambda i: (0, 0))],
        out_specs=[pl.BlockSpec((tr, d // 2), lambda i: (i, 0)),
                   pl.BlockSpec((tr, ROUTE_LANES), lambda i: (i, 0)),
                   pl.BlockSpec((SUBLANES, ROUTE_LANES), lambda i: (0, 0))],
        out_shape=[jax.ShapeDtypeStruct((n, d // 2), jnp.uint32),
                   jax.ShapeDtypeStruct((n, ROUTE_LANES), F32),
                   jax.ShapeDtypeStruct((SUBLANES, ROUTE_LANES), F32)],
        scratch_shapes=[pltpu.VMEM((SUBLANES, ROUTE_LANES), F32)],
        compiler_params=_params(("arbitrary",)),
        name="router",
    )(h2, g, w, b)


def _row_copy(src_ref, src_row, dst_ref, dst_row, sem):
    return pltpu.make_async_copy(src_ref.at[pl.ds(src_row, 1), :], dst_ref.at[pl.ds(dst_row, 1), :], sem)


def _dispatch_kernel(dest_ref, zlo_ref, zhi_ref, m_ref, xg_ref, zero_ref, sem, zsem, *, tt, n_steps):
    @pl.when(pl.program_id(0) == 0)
    def _():
        zero_ref[...] = jnp.zeros_like(zero_ref)

        def range_body(e, carry):
            def zero_start(rr, c):
                _row_copy(zero_ref, 0, xg_ref, rr, zsem).start()
                return c

            lax.fori_loop(zlo_ref[e], zhi_ref[e], zero_start, 0)
            return carry

        lax.fori_loop(0, N_EXPERTS + 1, range_body, 0)

    @pl.when(pl.program_id(0) == pl.num_programs(0) - 1)
    def _():
        n_zero = xg_ref.shape[0] - TOP_K * tt * n_steps
        pltpu.make_async_copy(xg_ref.at[pl.ds(0, n_zero), :], xg_ref.at[pl.ds(0, n_zero), :], zsem).wait()

    def start_body(t, c):
        for k in range(TOP_K):
            _row_copy(m_ref, t, xg_ref, dest_ref[t * TOP_K + k], sem).start()
        return c

    lax.fori_loop(0, tt, start_body, 0)
    for k in range(TOP_K):
        pltpu.make_async_copy(m_ref, xg_ref.at[pl.ds(0, tt), :], sem).wait()


def _dispatch(dest_flat, zlo, zhi, m, rows, tt):
    n, d = m.shape
    return pl.pallas_call(
        functools.partial(_dispatch_kernel, tt=tt, n_steps=n // tt),
        grid=(n // tt,),
        in_specs=[pl.BlockSpec((tt * TOP_K,), lambda i: (i,), memory_space=pltpu.SMEM),
                  pl.BlockSpec(memory_space=pltpu.SMEM),
                  pl.BlockSpec(memory_space=pltpu.SMEM),
                  pl.BlockSpec((tt, d), lambda i: (i, 0))],
        out_specs=pl.BlockSpec(memory_space=pl.ANY),
        out_shape=jax.ShapeDtypeStruct((rows, d), m.dtype),
        scratch_shapes=[pltpu.VMEM((SUBLANES, d), m.dtype),
                        pltpu.SemaphoreType.DMA(()),
                        pltpu.SemaphoreType.DMA(())],
        compiler_params=_params(("arbitrary",)),
        name="dispatch",
    )(dest_flat, zlo, zhi, m)


def _for_row_blocks(nsub, block_fn):
    n8 = nsub // 8
    rem8 = nsub - n8 * 8
    merged = jnp.where((rem8 == 1) & (n8 >= 1), 1, 0)

    def body(i, c):
        start = pl.multiple_of(i * (8 * SUB_ROWS), 8 * SUB_ROWS)
        block_fn(start, 4 * SUB_ROWS)
        block_fn(start + 4 * SUB_ROWS, 4 * SUB_ROWS)
        return c

    lax.fori_loop(0, n8 - merged, body, 0)

    @pl.when(merged == 1)
    def _():
        start = pl.multiple_of((n8 - 1) * (8 * SUB_ROWS), 8 * SUB_ROWS)
        block_fn(start, 4 * SUB_ROWS)
        block_fn(start + 4 * SUB_ROWS, 5 * SUB_ROWS)

    rem8 = rem8 * (1 - merged)
    base8 = n8 * (8 * SUB_ROWS)

    @pl.when(rem8 >= 4)
    def _():
        block_fn(pl.multiple_of(base8, 4 * SUB_ROWS), 4 * SUB_ROWS)

    rem = rem8 % 4
    base = base8 + (rem8 // 4) * (4 * SUB_ROWS)

    @pl.when(rem >= 2)
    def _():
        block_fn(pl.multiple_of(base, SUB_ROWS), 2 * SUB_ROWS)

    @pl.when(rem % 2 == 1)
    def _():
        block_fn(pl.multiple_of(base + (rem // 2) * (2 * SUB_ROWS), SUB_ROWS), SUB_ROWS)


def _expert_kernel(gexp_ref, gsub_ref, gns_ref, tail_ref,
                   xg_ref, wg_ref, wu_ref, bg_ref, bu_ref, wd_ref, bd_ref, yg_ref,
                   xb_ref, hid_ref, stage_ref, ybuf_ref, pend_ref,
                   xsem, ysem, *, nj, nn):
    g = pl.program_id(0)
    s = pl.program_id(1)
    tf = GATE_UP_COLS
    tn = DOWN_COLS
    ns = gns_ref[g]
    row0 = gsub_ref[g] * SUB_ROWS

    def aligned(v, m):
        return v if isinstance(v, int) else pl.multiple_of(v, m)

    def y_copy(sub, slot, col, first_row):
        return pltpu.make_async_copy(
            ybuf_ref.at[slot, pl.ds(aligned(sub * SUB_ROWS, SUB_ROWS), SUB_ROWS), :],
            yg_ref.at[pl.ds(aligned(first_row + sub * SUB_ROWS, SUB_ROWS), SUB_ROWS),
                      pl.ds(aligned(col * tn, tn), tn)],
            ysem.at[slot])

    def x_copy(first_row, sub, slot):
        return pltpu.make_async_copy(
            xg_ref.at[pl.ds(aligned(first_row + sub * SUB_ROWS, SUB_ROWS), SUB_ROWS), :],
            stage_ref.at[slot], xsem.at[slot])

    def stage_to_rows(sub, slot):
        rows = pl.ds(aligned(sub * SUB_ROWS, SUB_ROWS), SUB_ROWS)
        lo, hi = _unpack_bf16_pairs(stage_ref[slot])
        half = lo.shape[1]
        xb_ref[rows, 0:half] = lo
        xb_ref[rows, half:2 * half] = hi

    def wait_pending(slot):
        def body(i, c):
            y_copy(0, slot, 0, 0).wait()
            return c

        lax.fori_loop(0, pend_ref[slot], body, 0)
        pend_ref[slot] = 0

    @pl.when((g == 0) & (s == 0))
    def _():
        pend_ref[0] = 0
        pend_ref[1] = 0
        ybuf_ref[0, 0:SUB_ROWS, :] = jnp.zeros((SUB_ROWS, tn), F32)
        lo = tail_ref[0]
        hi = tail_ref[1]

        def start_body(i, c):
            for col in range(nn):
                y_copy(0, 0, col, i * SUB_ROWS).start()
            return c

        def wait_body(i, c):
            for col in range(nn):
                y_copy(0, 0, col, i * SUB_ROWS).wait()
            return c

        lax.fori_loop(lo, hi, start_body, 0)
        lax.fori_loop(lo, hi, wait_body, 0)

        x_copy(row0, 0, 0).start()

        def first_body(sub, c):
            slot = lax.rem(sub, 2)

            @pl.when(sub + 1 < ns)
            def _():
                x_copy(row0, sub + 1, 1 - slot).start()

            x_copy(row0, sub, slot).wait()
            stage_to_rows(sub, slot)
            return c

        lax.fori_loop(0, ns, first_body, 0)

    @pl.when(ns > 0)
    def _():
        @pl.when(s < nj)
        def _():
            bg = bg_ref[gexp_ref[g], pl.ds(s, 1), :]
            bu = bu_ref[gexp_ref[g], pl.ds(s, 1), :]

            def block(start, rows_n):
                rows = pl.ds(start, rows_n)
                x = xb_ref[rows, :]
                gt = jnp.dot(x, wg_ref[0].astype(BF16), preferred_element_type=F32) + bg
                up = jnp.dot(x, wu_ref[0].astype(BF16), preferred_element_type=F32) + bu
                gt = jnp.minimum(gt, SWIGLU_LIMIT)
                up = jnp.clip(up, -SWIGLU_LIMIT, SWIGLU_LIMIT)
                hid = (up + 1.0) * gt * jax.nn.sigmoid(SWIGLU_ALPHA * gt)
                hid_ref[s, rows, :] = hid.astype(hid_ref.dtype)

            _for_row_blocks(ns, block)

        @pl.when(s >= nj)
        def _():
            col = s - nj
            slot = lax.rem(col, 2)
            bd = bd_ref[gexp_ref[g], pl.ds(col, 1), :]
            wait_pending(slot)

            last_g = pl.num_programs(0) - 1
            g_next = jnp.minimum(g + 1, last_g)
            ns_next = jnp.where(g < last_g, gns_ref[g_next], 0)
            row0_next = gsub_ref[g_next] * SUB_ROWS
            for k in range(PREFETCH_SUBS):
                @pl.when(PREFETCH_SUBS * col + k < ns_next)
                def _():
                    x_copy(row0_next, PREFETCH_SUBS * col + k, k).start()

            def block(start, rows_n):
                rows = pl.ds(start, rows_n)
                acc = jnp.dot(hid_ref[0, rows, :], wd_ref[0, 0:tf, :].astype(BF16),
                              preferred_element_type=F32)
                for c in range(1, nj):
                    acc = acc + jnp.dot(hid_ref[c, rows, :], wd_ref[0, c * tf:(c + 1) * tf, :].astype(BF16),
                                        preferred_element_type=F32)
                ybuf_ref[slot, rows, :] = acc + bd

            _for_row_blocks(ns, block)

            def start_body(sub, c):
                y_copy(sub, slot, col, row0).start()
                return c

            lax.fori_loop(0, ns, start_body, 0)
            pend_ref[slot] = ns

            for k in range(PREFETCH_SUBS):
                @pl.when(PREFETCH_SUBS * col + k < ns_next)
                def _():
                    x_copy(row0_next, PREFETCH_SUBS * col + k, k).wait()
                    stage_to_rows(PREFETCH_SUBS * col + k, k)

    @pl.when((g == pl.num_programs(0) - 1) & (s == nj + nn - 1))
    def _():
        wait_pending(0)
        wait_pending(1)


def _experts(gexp, gsub, gns, tail, xg, w_gate, w_up, b_gate, b_up, w_down, b_down, n_groups):
    rows = xg.shape[0]
    d = w_gate.shape[1]
    f = w_gate.shape[2]
    tf, tn = GATE_UP_COLS, DOWN_COLS
    nj, nn = f // tf, d // tn
    assert GROUP_SUBS <= PREFETCH_SUBS * nn, "the next group's rows are fetched under the down steps"

    def up_idx(g, s, ge, gs, gn, tl):
        act = jnp.minimum(gn[g], 1)
        return (ge[g], 0, jnp.minimum(s, nj - 1) * act + (nj - 1) * (1 - act))

    def down_idx(g, s, ge, gs, gn, tl):
        act = jnp.minimum(gn[g], 1)
        return (ge[g], 0, jnp.maximum(s - nj, 0) * act + (nn - 1) * (1 - act))

    def whole(g, s, ge, gs, gn, tl):
        return (0, 0, 0)

    grid_spec = pltpu.PrefetchScalarGridSpec(
        num_scalar_prefetch=4,
        grid=(n_groups, nj + nn),
        in_specs=[
            pl.BlockSpec(memory_space=pl.ANY),
            pl.BlockSpec((1, d, tf), up_idx),
            pl.BlockSpec((1, d, tf), up_idx),
            pl.BlockSpec((N_EXPERTS, nj, tf), whole),
            pl.BlockSpec((N_EXPERTS, nj, tf), whole),
            pl.BlockSpec((1, f, tn), down_idx),
            pl.BlockSpec((N_EXPERTS, nn, tn), whole),
        ],
        out_specs=pl.BlockSpec(memory_space=pl.ANY),
        scratch_shapes=[pltpu.VMEM((GROUP_ROWS, d), BF16),
                        pltpu.VMEM((nj, GROUP_ROWS, tf), BF16),
                        pltpu.VMEM((PREFETCH_SUBS, SUB_ROWS, d // 2), xg.dtype),
                        pltpu.VMEM((2, GROUP_ROWS, tn), F32),
                        pltpu.SMEM((2,), jnp.int32),
                        pltpu.SemaphoreType.DMA((PREFETCH_SUBS,)),
                        pltpu.SemaphoreType.DMA((2,))],
    )
    return pl.pallas_call(
        functools.partial(_expert_kernel, nj=nj, nn=nn),
        grid_spec=grid_spec,
        out_shape=jax.ShapeDtypeStruct((rows, d), F32),
        compiler_params=_params(("arbitrary", "arbitrary"), EXPERT_VMEM_LIMIT),
        name="experts",
    )(gexp, gsub, gns, tail, xg, w_gate, w_up, b_gate.reshape(N_EXPERTS, nj, tf),
      b_up.reshape(N_EXPERTS, nj, tf), w_down, b_down.reshape(N_EXPERTS, nn, tn))


def _combine_kernel(dcur_ref, dnext_ref, route_ref, h_ref, g_ref, yg_ref, o_ref, buf_ref, sem, *, tt):
    i = pl.program_id(0)
    n_steps = pl.num_programs(0)
    slot = lax.rem(i, 2)

    def start_tile(dref, sl):
        def body(t, c):
            for k in range(TOP_K):
                pltpu.make_async_copy(yg_ref.at[pl.ds(dref[t * TOP_K + k], 1), :],
                                      buf_ref.at[sl, k, pl.ds(t, 1), :], sem.at[sl]).start()
            return c

        lax.fori_loop(0, tt, body, 0, unroll=4)

    @pl.when(i == 0)
    def _():
        start_tile(dcur_ref, 0)

    @pl.when(i + 1 < n_steps)
    def _():
        start_tile(dnext_ref, 1 - slot)

    for k in range(TOP_K):
        pltpu.make_async_copy(yg_ref.at[pl.ds(0, tt), :], buf_ref.at[slot, k], sem.at[slot]).wait()

    d = o_ref.shape[1]
    cw = 2 * LANES
    gates = [route_ref[:, k:k + 1] for k in range(TOP_K)]
    ssq = jnp.zeros((tt, 1), F32)
    for cs in range(0, d, cw):
        acc = h_ref[:, cs:cs + cw]
        for k in range(TOP_K):
            acc = acc + buf_ref[slot, k, :, cs:cs + cw] * gates[k]
        o_ref[:, cs:cs + cw] = acc
        ssq = ssq + jnp.sum(acc * acc, axis=-1, keepdims=True)
    scale = lax.rsqrt(ssq * (1.0 / d) + NORM_EPS)
    for cs in range(0, d, cw):
        o_ref[:, cs:cs + cw] = o_ref[:, cs:cs + cw] * scale * g_ref[:, cs:cs + cw]


def _combine(dest_flat, route, h2, g, yg, tt):
    n, d = h2.shape
    steps = n // tt
    return pl.pallas_call(
        functools.partial(_combine_kernel, tt=tt),
        grid=(steps,),
        in_specs=[pl.BlockSpec((tt * TOP_K,), lambda i: (i,), memory_space=pltpu.SMEM),
                  pl.BlockSpec((tt * TOP_K,), lambda i: (jnp.minimum(i + 1, steps - 1),),
                               memory_space=pltpu.SMEM),
                  pl.BlockSpec((tt, ROUTE_LANES), lambda i: (i, 0)),
                  pl.BlockSpec((tt, d), lambda i: (i, 0)),
                  pl.BlockSpec((1, d), lambda i: (0, 0)),
                  pl.BlockSpec(memory_space=pl.ANY)],
        out_specs=pl.BlockSpec((tt, d), lambda i: (i, 0)),
        out_shape=jax.ShapeDtypeStruct((n, d), F32),
        scratch_shapes=[pltpu.VMEM((2, TOP_K, tt, d), F32),
                        pltpu.SemaphoreType.DMA((2,))],
        compiler_params=_params(("arbitrary",)),
        name="combine",
    )(dest_flat, dest_flat, route, h2, g, yg)


def _retention_tables(dh):
    lg = jnp.log1p(-jnp.exp2(-5.0 - jnp.arange(RET_HEADS, dtype=F32)))
    i = jnp.arange(CHUNK, dtype=F32)
    rel = i[:, None] - i[None, :]
    causal = rel >= 0
    dmask = jnp.where(causal[None], jnp.exp(jnp.where(causal, rel, 0.0)[None] * lg[:, None, None]), 0.0)
    q_dec = jnp.exp((i + 1.0)[None, :] * lg[:, None])
    k_dec = jnp.exp((CHUNK - 1.0 - i)[None, :] * lg[:, None])
    c_dec = jnp.exp(CHUNK * lg)
    bcast = lambda t: jnp.broadcast_to(t[:, :, None], (RET_HEADS, CHUNK, LANES))
    return dmask, bcast(q_dec), bcast(k_dec), c_dec


def _rotary_tables(length, dh):
    half = dh // 2
    inv = ROPE_BASE ** (-jnp.arange(half, dtype=F32) / half)
    ang = jnp.arange(length, dtype=jnp.int32).astype(F32)[:, None] * inv[None, :]
    return jnp.cos(ang), jnp.sin(ang)


def kernel(x, meta_tokens, norm_mix, w_in, conv_w, w_out, norm_ffn, router_w, router_b,
           w_gate, b_gate, w_up, b_up, w_down, b_down, norm_final):
    bsz, seq, d = x.shape
    assert w_in.shape[0] == 1, "single-layer stack expected"
    assert meta_tokens.shape[0] == N_META and seq % CHUNK == 0
    r = d // 2
    dh = r // RET_HEADS
    assert dh // 2 == LANES
    n = bsz * seq
    assert all(n % t == 0 for t in (NORM_ROWS, PROJ_ROWS, ROUTER_ROWS, DISPATCH_ROWS, COMBINE_ROWS))
    assert d % PROJ_COLS == 0 and w_in.shape[2] % PROJ_COLS == 0 and r % CONV_COLS == 0
    assert w_gate.shape[3] % GATE_UP_COLS == 0 and d % DOWN_COLS == 0
    assert router_w.shape[2] == N_EXPERTS and w_gate.shape[1] == N_EXPERTS
    x2d = x.reshape(n, d)

    a = _prenorm(x2d, norm_mix[0][None, :], NORM_ROWS)
    a_meta = _prenorm(meta_tokens.astype(x.dtype), norm_mix[0][None, :], N_META)
    p, pm = _inproj(a, a_meta, w_in[0], PROJ_ROWS, PROJ_COLS)

    cos, sin = _rotary_tables(N_META + seq, dh)
    pad_meta = lambda t: jnp.pad(t, ((CHUNK - N_META, 0), (0, 0)))
    dmask, qdec, kdec, cdec = _retention_tables(dh)
    s0, u0 = _meta_state(pad_meta(pm), pad_meta(cos[:N_META]), pad_meta(sin[:N_META]), kdec, r, dh)
    mix = _mixer(p, cos[N_META:], sin[N_META:], dmask, qdec, kdec, cdec, conv_w[0], s0, u0,
                 bsz, seq, r, dh)
    h2 = _outproj(mix, w_out[0], x2d, PROJ_ROWS, PROJ_COLS)

    rw = jnp.pad(router_w[0], ((0, 0), (0, ROUTE_LANES - N_EXPERTS))).astype(BF16)
    rb = jnp.pad(router_b[0].astype(F32), (0, ROUTE_LANES - N_EXPERTS), constant_values=-jnp.inf)[None, :]
    m, route, cnt = _router(h2, norm_ffn[0][None, :], rw, rb, ROUTER_ROWS)

    side = route[:, TOP_K:3 * TOP_K].T.astype(jnp.int32)
    ids, rank = side[:TOP_K], side[TOP_K:]
    counts = cnt[0, :N_EXPERTS].astype(jnp.int32)

    total_sub = (n * TOP_K + N_EXPERTS * (SUB_ROWS - 1) + SUB_ROWS - 1) // SUB_ROWS
    psub = (counts + SUB_ROWS - 1) // SUB_ROWS
    pend = jnp.cumsum(psub) * SUB_ROWS
    pstart = pend - psub * SUB_ROWS
    first_row = jnp.zeros_like(ids)
    for e in range(N_EXPERTS):
        first_row = jnp.where(ids == e, pstart[e], first_row)
    dest_flat = (first_row + rank).T.reshape(-1).astype(jnp.int32)
    zlo = jnp.concatenate([pstart + counts, pend[-1:]]).astype(jnp.int32)
    zhi = jnp.concatenate([pend, jnp.full((1,), total_sub * SUB_ROWS)]).astype(jnp.int32)

    n_groups_max = N_EXPERTS + total_sub // GROUP_SUBS
    ngrp = (psub + GROUP_SUBS - 1) // GROUP_SUBS
    gend = jnp.cumsum(ngrp)
    gstart = gend - ngrp
    n_groups = gend[-1]
    gidx = jnp.arange(n_groups_max, dtype=jnp.int32)
    glast = jnp.minimum(gidx, n_groups - 1)
    gexp = jnp.minimum(jnp.searchsorted(gend, glast, side="right"), N_EXPERTS - 1).astype(jnp.int32)
    gq = glast - gstart[gexp]
    gsub = (pstart[gexp] // SUB_ROWS + gq * GROUP_SUBS).astype(jnp.int32)
    gns = jnp.where(gidx < n_groups, jnp.minimum(psub[gexp] - gq * GROUP_SUBS, GROUP_SUBS), 0).astype(jnp.int32)
    tail = jnp.stack([pend[-1] // SUB_ROWS, jnp.asarray(total_sub, pend.dtype)]).astype(jnp.int32)

    xg = _dispatch(dest_flat, zlo, zhi, m, total_sub * SUB_ROWS, DISPATCH_ROWS)
    yg = _experts(gexp, gsub, gns, tail, xg, w_gate[0], w_up[0], b_gate[0], b_up[0],
                  w_down[0], b_down[0], n_groups.astype(jnp.int32))
    out = _combine(dest_flat, route, h2, norm_final[None, :], yg, COMBINE_ROWS)
    return out.reshape(bsz, seq, d)
```
